```python
import jax
import jax.numpy as jnp
from jax import lax
import numpy as np

D_MODEL = 2048
BATCH = 4
SEQ = 2048
DEPTH = 4
DEC_BATCH = 128
DEC_SEQ = 4
PAST_LEN = 16384
PAGE_SIZE = 128

PLE_DIM = 256
CONV_W = 4
CHUNK = 64
EPS = 1e-6

GDN_HEAD = 128
GDN_QK_HEADS = D_MODEL // GDN_HEAD
GDN_V_HEADS = 2 * GDN_QK_HEADS
GDN_REP = GDN_V_HEADS // GDN_QK_HEADS
GDN_QK_W = GDN_QK_HEADS * GDN_HEAD
GDN_V_W = GDN_V_HEADS * GDN_HEAD
GDN_CONV_DIM = 2 * GDN_QK_W + GDN_V_W
GDN_IN = GDN_CONV_DIM + GDN_V_W + 2 * GDN_V_HEADS

SSD_D_INNER = 2 * D_MODEL
SSD_HEAD = 64
SSD_HEADS = SSD_D_INNER // SSD_HEAD
SSD_STATE = 128
SSD_GROUPS = 8
SSD_HPG = SSD_HEADS // SSD_GROUPS
SSD_GN = SSD_GROUPS * SSD_STATE
SSD_CONV_DIM = SSD_D_INNER + 2 * SSD_GN
SSD_IN = SSD_D_INNER + SSD_CONV_DIM + SSD_HEADS
SSD_NORM_GROUP = SSD_D_INNER // SSD_GROUPS

N_GDN_LAYERS = (DEPTH + 1) // 2
N_SSD_LAYERS = DEPTH // 2

kernel_name = "gdn_mamba2_hybrid_decode_step"


def rmsnorm(x, w):
    xf = x.astype(jnp.float32)
    y = xf * lax.rsqrt(jnp.mean(xf * xf, axis=-1, keepdims=True) + EPS)
    return y * w.astype(jnp.float32)


def l2norm(x):
    return x * lax.rsqrt(jnp.sum(x * x, axis=-1, keepdims=True) + EPS)


def causal_conv(u, buf, w):
    L = u.shape[1]
    full = jnp.concatenate([buf.astype(u.dtype), u], axis=1)
    y = full[:, 0:L] * w[0]
    for j in range(1, CONV_W):
        y = y + full[:, j:j + L] * w[j]
    return y, full[:, L:]


def chunk_decay(acs):
    idx = jnp.arange(acs.shape[-1])
    incl = idx[:, None] >= idx[None, :]
    diff = acs[..., :, None] - acs[..., None, :]
    return jnp.exp(jnp.where(incl, diff, -jnp.inf))


def gdn_chunked(q, k, v, beta, g, S0):
    Bsz, L, H, _ = q.shape
    NC = L // CHUNK

    def chunks(t):
        return t.reshape((Bsz, NC, CHUNK, H) + t.shape[3:]).swapaxes(2, 3)

    q, k, v, beta, g = (chunks(t) for t in (q, k, v, beta, g))
    g = jnp.cumsum(g, axis=-1)
    decay = chunk_decay(g)
    eye = jnp.eye(CHUNK, dtype=jnp.float32)
    strict = jnp.tril(jnp.ones((CHUNK, CHUNK), jnp.float32), -1)
    k_beta = k * beta[..., None]
    a_mat = jnp.einsum("bnhid,bnhjd->bnhij", k_beta, k) * decay * strict
    t_mat = lax.linalg.triangular_solve(eye + a_mat, jnp.broadcast_to(eye, a_mat.shape),
                                        left_side=True, lower=True, unit_diagonal=True)
    u = jnp.einsum("bnhij,bnhje->bnhie", t_mat, v * beta[..., None])
    w = jnp.einsum("bnhij,bnhjd->bnhid", t_mat, k_beta * jnp.exp(g)[..., None])
    qk = jnp.einsum("bnhid,bnhjd->bnhij", q, k) * decay
    g_last = g[..., -1]
    k_dec = k * jnp.exp(g_last[..., None] - g)[..., None]
    q_dec = q * jnp.exp(g)[..., None]

    def step(S, inp):
        u_c, w_c, qk_c, qd_c, kd_c, gl_c = inp
        v_new = u_c - jnp.einsum("bhcd,bhde->bhce", w_c, S)
        o = jnp.einsum("bhcd,bhde->bhce", qd_c, S) + jnp.einsum("bhij,bhje->bhie", qk_c, v_new)
        S = S * jnp.exp(gl_c)[..., None, None] + jnp.einsum("bhcd,bhce->bhde", kd_c, v_new)
        return S, o

    scan_in = tuple(t.swapaxes(0, 1) for t in (u, w, qk, q_dec, k_dec, g_last))
    S, o = lax.scan(step, S0, scan_in)
    o = o.transpose(1, 0, 3, 2, 4).reshape(Bsz, L, H, -1)
    return o, S


def gdn_recurrent(q, k, v, beta, g, S0):
    def step(S, inp):
        q_t, k_t, v_t, b_t, g_t = inp
        S = S * jnp.exp(g_t)[..., None, None]
        kv = jnp.einsum("bhd,bhde->bhe", k_t, S)
        S = S + jnp.einsum("bhd,bhe->bhde", k_t, (v_t - kv) * b_t[..., None])
        return S, jnp.einsum("bhd,bhde->bhe", q_t, S)

    scan_in = tuple(t.swapaxes(0, 1) for t in (q, k, v, beta, g))
    S, o = lax.scan(step, S0, scan_in)
    return o.swapaxes(0, 1), S


def gdn_mixer(hn, conv_buf, S0, w_in, conv_w, A_log, dt_bias, norm_w, w_out, chunked):
    f32 = jnp.float32
    Bsz, L, _ = hn.shape
    proj = hn @ w_in
    o1 = GDN_CONV_DIM
    o2 = o1 + GDN_V_W
    o3 = o2 + GDN_V_HEADS
    qkv, z, a, b = proj[..., :o1], proj[..., o1:o2], proj[..., o2:o3], proj[..., o3:]
    qkv, new_buf = causal_conv(qkv, conv_buf, conv_w)
    qkv = jax.nn.silu(qkv).astype(f32)
    q = qkv[..., :GDN_QK_W].reshape(Bsz, L, GDN_QK_HEADS, GDN_HEAD)
    k = qkv[..., GDN_QK_W:2 * GDN_QK_W].reshape(Bsz, L, GDN_QK_HEADS, GDN_HEAD)
    v = qkv[..., 2 * GDN_QK_W:].reshape(Bsz, L, GDN_V_HEADS, GDN_HEAD)
    q = jnp.repeat(l2norm(q) * (GDN_HEAD ** -0.5), GDN_REP, axis=2)
    k = jnp.repeat(l2norm(k), GDN_REP, axis=2)
    beta = jax.nn.sigmoid(b.astype(f32))
    g = -jnp.exp(A_log.astype(f32)) * jax.nn.softplus(a.astype(f32) + dt_bias.astype(f32))
    mix = gdn_chunked if chunked else gdn_recurrent
    o, S = mix(q, k, v, beta, g, S0.astype(f32))
    zf = z.astype(f32).reshape(Bsz, L, GDN_V_HEADS, GDN_HEAD)
    o = rmsnorm(o, norm_w) * jax.nn.silu(zf)
    out = o.reshape(Bsz, L, GDN_V_W).astype(hn.dtype) @ w_out
    return out, new_buf, S


def ssd_chunked(xs, dt, A, Bm, Cm, S0):
    Bsz, L = xs.shape[:2]
    NC = L // CHUNK
    Xc = (xs * dt[..., None]).reshape(Bsz, NC, CHUNK, SSD_GROUPS, SSD_HPG, SSD_HEAD)
    Bc = Bm.reshape(Bsz, NC, CHUNK, SSD_GROUPS, SSD_STATE)
    Cc = Cm.reshape(Bsz, NC, CHUNK, SSD_GROUPS, SSD_STATE)
    acs = jnp.cumsum((dt * A).reshape(Bsz, NC, CHUNK, SSD_GROUPS, SSD_HPG).transpose(0, 3, 4, 1, 2), axis=-1)
    decay = chunk_decay(acs)
    cb = jnp.einsum("bclgn,bcsgn->bgcls", Cc, Bc)
    y_diag = jnp.einsum("bgcls,bgrcls,bcsgrp->bclgrp", cb, decay, Xc)

    def step(S, inp):
        c_c, b_c, x_c, a_c = inp
        y_off = jnp.einsum("blgn,bgrpn,bgrl->blgrp", c_c, S, jnp.exp(a_c))
        a_last = a_c[..., -1]
        S = S * jnp.exp(a_last)[..., None, None] + jnp.einsum(
            "blgn,bgrl,blgrp->bgrpn", b_c, jnp.exp(a_last[..., None] - a_c), x_c)
        return S, y_off

    scan_in = (Cc.swapaxes(0, 1), Bc.swapaxes(0, 1), Xc.swapaxes(0, 1), acs.transpose(3, 0, 1, 2, 4))
    S, y_off = lax.scan(step, S0, scan_in)
    y = (y_diag + y_off.swapaxes(0, 1)).reshape(Bsz, L, SSD_GROUPS, SSD_HPG, SSD_HEAD)
    return y, S


def ssd_recurrent(xs, dt, A, Bm, Cm, S0):
    def step(S, inp):
        x_t, dt_t, b_t, c_t = inp
        S = S * jnp.exp(dt_t * A)[..., None, None] + jnp.einsum("bgrp,bgn->bgrpn", x_t * dt_t[..., None], b_t)
        return S, jnp.einsum("bgrpn,bgn->bgrp", S, c_t)

    scan_in = tuple(t.swapaxes(0, 1) for t in (xs, dt, Bm, Cm))
    S, y = lax.scan(step, S0, scan_in)
    return y.swapaxes(0, 1), S


def ssd_mixer(hn, conv_buf, S0, w_in, conv_w, conv_b, A_log, dt_bias, D_skip, norm_w, w_out, chunked):
    f32 = jnp.float32
    Bsz, L, _ = hn.shape
    proj = hn @ w_in
    o1 = SSD_D_INNER
    o2 = o1 + SSD_CONV_DIM
    z, xbc, dt_raw = proj[..., :o1], proj[..., o1:o2], proj[..., o2:]
    xbc, new_buf = causal_conv(xbc, conv_buf, conv_w)
    xbc = jax.nn.silu(xbc + conv_b).astype(f32)
    xs = xbc[..., :SSD_D_INNER].reshape(Bsz, L, SSD_GROUPS, SSD_HPG, SSD_HEAD)
    Bm = xbc[..., SSD_D_INNER:SSD_D_INNER + SSD_GN].reshape(Bsz, L, SSD_GROUPS, SSD_STATE)
    Cm = xbc[..., SSD_D_INNER + SSD_GN:].reshape(Bsz, L, SSD_GROUPS, SSD_STATE)
    dt = jax.nn.softplus(dt_raw.astype(f32) + dt_bias.astype(f32)).reshape(Bsz, L, SSD_GROUPS, SSD_HPG)
    A = -jnp.exp(A_log.astype(f32)).reshape(SSD_GROUPS, SSD_HPG)
    S0 = S0.astype(f32).reshape(Bsz, SSD_GROUPS, SSD_HPG, SSD_HEAD, SSD_STATE)
    mix = ssd_chunked if chunked else ssd_recurrent
    y, S = mix(xs, dt, A, Bm, Cm, S0)
    y = y + D_skip.astype(f32).reshape(SSD_GROUPS, SSD_HPG)[..., None] * xs
    y = y.reshape(Bsz, L, SSD_D_INNER) * jax.nn.silu(z.astype(f32))
    yg = y.reshape(Bsz, L, SSD_GROUPS, SSD_NORM_GROUP)
    yg = yg * lax.rsqrt(jnp.mean(yg * yg, axis=-1, keepdims=True) + EPS)
    y = yg.reshape(Bsz, L, SSD_D_INNER) * norm_w.astype(f32)
    out = y.astype(hn.dtype) @ w_out
    return out, new_buf, S.reshape(Bsz, SSD_HEADS, SSD_HEAD, SSD_STATE)


def ple_add(h, p_i, w_proj, w_gate, norm_w):
    e = rmsnorm(p_i @ w_proj, norm_w)
    gate = jax.nn.sigmoid((h @ w_gate).astype(jnp.float32))
    return h + (e * gate).astype(h.dtype)


def trunk(x, p, gdn_conv0, gdn_S0, ssd_conv0, ssd_S0, weights, chunked):
    (norm_w, gdn_w_in, gdn_conv_w, gdn_A_log, gdn_dt_bias, gdn_norm_w, gdn_w_out,
     ssd_w_in, ssd_conv_w, ssd_conv_b, ssd_A_log, ssd_dt_bias, ssd_D, ssd_norm_w, ssd_w_out,
     ple_w_proj, ple_w_gate, ple_norm_w, final_norm_w) = weights
    h = x
    gdn_states, gdn_convs, ssd_states, ssd_convs = [], [], [], []
    for i in range(DEPTH):
        j = i // 2
        hn = rmsnorm(h, norm_w[i]).astype(h.dtype)
        if i % 2 == 0:
            mix, buf, S = gdn_mixer(hn, gdn_conv0[j], gdn_S0[j], gdn_w_in[j], gdn_conv_w[j], gdn_A_log[j],
                                    gdn_dt_bias[j], gdn_norm_w[j], gdn_w_out[j], chunked)
            gdn_states.append(S)
            gdn_convs.append(buf)
        else:
            mix, buf, S = ssd_mixer(hn, ssd_conv0[j], ssd_S0[j], ssd_w_in[j], ssd_conv_w[j], ssd_conv_b[j],
                                    ssd_A_log[j], ssd_dt_bias[j], ssd_D[j], ssd_norm_w[j], ssd_w_out[j], chunked)
            ssd_states.append(S)
            ssd_convs.append(buf)
        h = h + mix
        h = ple_add(h, p[i], ple_w_proj[i], ple_w_gate[i], ple_norm_w[i])
    y = rmsnorm(h, final_norm_w).astype(x.dtype)
    return y, jnp.stack(gdn_states), jnp.stack(gdn_convs), jnp.stack(ssd_states), jnp.stack(ssd_convs)


def setup_inputs(seed: int = 0) -> dict:
    key = jax.random.key(seed)
    ks = jax.random.split(key, 32)
    f32 = jnp.float32
    NG, NS = N_GDN_LAYERS, N_SSD_LAYERS

    def nrm(k, shape, scale):
        return jax.random.normal(k, shape, f32) * scale

    def dt_bias_init(k, shape):
        dt = jnp.exp(jax.random.uniform(k, shape, f32, np.log(1e-3), np.log(1e-1)))
        return dt + jnp.log(-jnp.expm1(-dt))

    return {
        "x_prompt": nrm(ks[0], (BATCH, SEQ, D_MODEL), 1.0),
        "x_sample": nrm(ks[1], (DEC_BATCH, DEC_SEQ, D_MODEL), 1.0),
        "state_gdn": nrm(ks[2], (NG, DEC_BATCH, GDN_V_HEADS, GDN_HEAD, GDN_HEAD), 0.1),
        "cache_gdn_conv": nrm(ks[3], (NG, DEC_BATCH, CONV_W - 1, GDN_CONV_DIM), 1.0),
        "state_ssd": nrm(ks[4], (NS, DEC_BATCH, SSD_HEADS, SSD_HEAD, SSD_STATE), 0.1),
        "cache_ssd_conv": nrm(ks[5], (NS, DEC_BATCH, CONV_W - 1, SSD_CONV_DIM), 1.0),
        "p_prompt": nrm(ks[6], (DEPTH, BATCH, SEQ, PLE_DIM), 1.0),
        "p_sample": nrm(ks[7], (DEPTH, DEC_BATCH, DEC_SEQ, PLE_DIM), 1.0),
        "norm_w": 1.0 + nrm(ks[8], (DEPTH, D_MODEL), 0.02),
        "gdn_w_in": nrm(ks[9], (NG, D_MODEL, GDN_IN), D_MODEL ** -0.5),
        "gdn_conv_w": nrm(ks[10], (NG, CONV_W, GDN_CONV_DIM), CONV_W ** -0.5),
        "gdn_A_log": jnp.log(jax.random.uniform(ks[11], (NG, GDN_V_HEADS), f32, 1.0, 16.0)),
        "gdn_dt_bias": dt_bias_init(ks[12], (NG, GDN_V_HEADS)),
        "gdn_norm_w": 1.0 + nrm(ks[13], (NG, GDN_HEAD), 0.02),
        "gdn_w_out": nrm(ks[14], (NG, GDN_V_W, D_MODEL), GDN_V_W ** -0.5),
        "ssd_w_in": nrm(ks[15], (NS, D_MODEL, SSD_IN), D_MODEL ** -0.5),
        "ssd_conv_w": nrm(ks[16], (NS, CONV_W, SSD_CONV_DIM), CONV_W ** -0.5),
        "ssd_conv_b": nrm(ks[17], (NS, SSD_CONV_DIM), 0.02),
        "ssd_A_log": jnp.log(jax.random.uniform(ks[18], (NS, SSD_HEADS), f32, 1.0, 16.0)),
        "ssd_dt_bias": dt_bias_init(ks[19], (NS, SSD_HEADS)),
        "ssd_D": 1.0 + nrm(ks[20], (NS, SSD_HEADS), 0.02),
        "ssd_norm_w": 1.0 + nrm(ks[21], (NS, SSD_D_INNER), 0.02),
        "ssd_w_out": nrm(ks[22], (NS, SSD_D_INNER, D_MODEL), SSD_D_INNER ** -0.5),
        "ple_w_proj": nrm(ks[23], (DEPTH, PLE_DIM, D_MODEL), PLE_DIM ** -0.5),
        "ple_w_gate": nrm(ks[24], (DEPTH, D_MODEL, D_MODEL), D_MODEL ** -0.5),
        "ple_norm_w": 1.0 + nrm(ks[25], (DEPTH, D_MODEL), 0.02),
        "final_norm_w": 1.0 + nrm(ks[26], (D_MODEL,), 0.02),
    }


def reference(x_prompt, x_sample, state_gdn, cache_gdn_conv, state_ssd, cache_ssd_conv, p_prompt, p_sample,
              norm_w, gdn_w_in, gdn_conv_w, gdn_A_log, gdn_dt_bias, gdn_norm_w, gdn_w_out,
              ssd_w_in, ssd_conv_w, ssd_conv_b, ssd_A_log, ssd_dt_bias, ssd_D, ssd_norm_w, ssd_w_out,
              ple_w_proj, ple_w_gate, ple_norm_w, final_norm_w):
    weights = (norm_w, gdn_w_in, gdn_conv_w, gdn_A_log, gdn_dt_bias, gdn_norm_w, gdn_w_out,
               ssd_w_in, ssd_conv_w, ssd_conv_b, ssd_A_log, ssd_dt_bias, ssd_D, ssd_norm_w, ssd_w_out,
               ple_w_proj, ple_w_gate, ple_norm_w, final_norm_w)
    bp = x_prompt.shape[0]
    gdn_S0 = jnp.zeros((N_GDN_LAYERS, bp, GDN_V_HEADS, GDN_HEAD, GDN_HEAD), jnp.float32)
    gdn_c0 = jnp.zeros((N_GDN_LAYERS, bp, CONV_W - 1, GDN_CONV_DIM), x_prompt.dtype)
    ssd_S0 = jnp.zeros((N_SSD_LAYERS, bp, SSD_HEADS, SSD_HEAD, SSD_STATE), jnp.float32)
    ssd_c0 = jnp.zeros((N_SSD_LAYERS, bp, CONV_W - 1, SSD_CONV_DIM), x_prompt.dtype)
    y_prompt, sg_p, cg_p, ss_p, cs_p = trunk(x_prompt, p_prompt, gdn_c0, gdn_S0, ssd_c0, ssd_S0, weights, True)
    y_sample, sg_s, cg_s, ss_s, cs_s = trunk(x_sample, p_sample, cache_gdn_conv, state_gdn,
                                             cache_ssd_conv, state_ssd, weights, False)
    return (y_prompt, y_sample,
            sg_p.astype(state_gdn.dtype), cg_p.astype(cache_gdn_conv.dtype),
            ss_p.astype(state_ssd.dtype), cs_p.astype(cache_ssd_conv.dtype),
            sg_s.astype(state_gdn.dtype), cg_s.astype(cache_gdn_conv.dtype),
            ss_s.astype(state_ssd.dtype), cs_s.astype(cache_ssd_conv.dtype))
```

```python
import functools

import jax
import jax.numpy as jnp
from jax import lax
from jax.experimental import pallas as pl
from jax.experimental.pallas import tpu as pltpu

F32 = jnp.float32
BF16 = jnp.bfloat16

EPS = 1e-6
CONV_W = 4
D_MODEL = 2048
PLE_DIM = 256

GDN_HEAD = 128
GDN_QK_HEADS = 16
GDN_V_HEADS = 32
GDN_QK_W = GDN_QK_HEADS * GDN_HEAD
GDN_V_W = GDN_V_HEADS * GDN_HEAD
GDN_CONV_DIM = 2 * GDN_QK_W + GDN_V_W
GDN_MAIN = GDN_CONV_DIM + GDN_V_W

SSD_D_INNER = 4096
SSD_HEAD = 64
SSD_HEADS = 64
SSD_STATE = 128
SSD_GROUPS = 8
SSD_HPG = SSD_HEADS // SSD_GROUPS
SSD_GN = SSD_GROUPS * SSD_STATE
SSD_CONV_DIM = SSD_D_INNER + 2 * SSD_GN
SSD_MAIN = SSD_D_INNER + SSD_CONV_DIM
SSD_GW = SSD_HPG * SSD_HEAD

LANES = 128
SUBLANES = 8
CHUNK_T = 128
SEQ_PAD = 8
SEQS_PER_CHUNK = CHUNK_T // SEQ_PAD
INV_BLOCK = 16
ROW_TILE = 512
COL_TILE = 512
VMEM_LIMIT = 56 * 1024 * 1024
NEG_BIG = -1e30


def _cparams(sem):
    return pltpu.CompilerParams(dimension_semantics=sem, vmem_limit_bytes=VMEM_LIMIT)


def _dot(a, b):
    return jnp.dot(a.astype(BF16), b.astype(BF16), preferred_element_type=F32)


def _dot_nt(a, b):
    return lax.dot_general(a.astype(BF16), b.astype(BF16), (((1,), (1,)), ((), ())),
                           preferred_element_type=F32)


def _split3(x):
    x1 = x.astype(BF16)
    r1 = x - x1.astype(F32)
    x2 = r1.astype(BF16)
    r2 = r1 - x2.astype(F32)
    return [x1, x2, r2.astype(BF16)]


def _dot3(a, b):
    a1 = a.astype(BF16)
    a2 = (a - a1.astype(F32)).astype(BF16)
    b1 = b.astype(BF16)
    b2 = (b - b1.astype(F32)).astype(BF16)
    d = functools.partial(jnp.dot, preferred_element_type=F32)
    return d(a1, b1) + (d(a1, b2) + d(a2, b1))


def _softplus(x):
    return jnp.maximum(x, 0.0) + jnp.log1p(jnp.exp(-jnp.abs(x)))


def _iota2(shape, dim):
    return lax.broadcasted_iota(jnp.int32, shape, dim)


def _pick_lane(x, idx):
    lane = _iota2(x.shape, 1)
    return jnp.sum(jnp.where(lane == idx, x, 0.0), axis=1, keepdims=True)


def _chunk_masks(sample):
    ii = _iota2((CHUNK_T, CHUNK_T), 0)
    jj = _iota2((CHUNK_T, CHUNK_T), 1)
    incl = ii >= jj
    strict = ii > jj
    if sample:
        same = (ii // SEQ_PAD) == (jj // SEQ_PAD)
        incl = incl & same
        strict = strict & same
    return incl, strict


def _seq_last_rows(x):
    rows, cols = x.shape
    x3 = x.reshape(rows // SEQ_PAD, SEQ_PAD, cols)
    last = jnp.broadcast_to(x3[:, SEQ_PAD - 1:SEQ_PAD, :], x3.shape)
    return last.reshape(rows, cols)


def _unit_lower_inverse(a, sample):
    ii = _iota2(a.shape, 0)
    jj = _iota2(a.shape, 1)
    eye = (ii == jj).astype(F32)

    def neumann(x, squarings):
        t = eye - x
        p = x
        for _ in range(squarings):
            p = _dot3(p, p)
            t = t + _dot3(t, p)
        return t

    if sample:
        assert SEQ_PAD == 8
        return neumann(a, 2)
    assert INV_BLOCK == 16 and CHUNK_T // INV_BLOCK == 8
    blk = (ii // INV_BLOCK) == (jj // INV_BLOCK)
    d = jnp.where(blk, a, 0.0)
    e = a - d
    td = neumann(d, 3)
    n = _dot3(td, e)
    return _dot3(neumann(n, 2), td)


def _causal_conv(buf_ref, x, w_ref):
    t = x.shape[0]
    buf_ref[SUBLANES:SUBLANES + t, :] = x
    w = w_ref[...]
    y = buf_ref[pl.ds(SUBLANES - 3, t), :] * w[0:1, :]
    for j in range(1, CONV_W):
        y = y + buf_ref[pl.ds(SUBLANES - 3 + j, t), :] * w[j:j + 1, :]
    buf_ref[0:SUBLANES, :] = buf_ref[t:t + SUBLANES, :]
    return y


def _zero_conv_history(bufs):
    for b in bufs:
        b[0:SUBLANES, :] = jnp.zeros((SUBLANES, b.shape[1]), F32)


def _real_row_mask(shape, dim):
    return (_iota2(shape, dim) % SEQ_PAD) >= (SEQ_PAD // 2)


def _rmsnorm_rows(x, w):
    return x * lax.rsqrt(jnp.mean(x * x, axis=-1, keepdims=True) + EPS) * w


def _norm_kernel(x_ref, w_ref, o_ref):
    o_ref[...] = _rmsnorm_rows(x_ref[...], w_ref[...]).astype(o_ref.dtype)


def _norm_rows(x, w, out_dtype):
    m, d = x.shape
    return pl.pallas_call(
        _norm_kernel,
        grid=(m // ROW_TILE,),
        in_specs=[pl.BlockSpec((ROW_TILE, d), lambda i: (i, 0)),
                  pl.BlockSpec((1, d), lambda i: (0, 0))],
        out_specs=pl.BlockSpec((ROW_TILE, d), lambda i: (i, 0)),
        out_shape=jax.ShapeDtypeStruct((m, d), out_dtype),
        compiler_params=_cparams(("parallel",)),
        name="norm_rows",
    )(x, w.reshape(1, d))


def _ple_embed_kernel(p_ref, w_ref, nw_ref, o_ref):
    e = _dot(p_ref[...], w_ref[...])
    o_ref[...] = _rmsnorm_rows(e, nw_ref[...])


def _ple_embed(p, w_proj, norm_w):
    m = p.shape[0]
    return pl.pallas_call(
        _ple_embed_kernel,
        grid=(m // ROW_TILE,),
        in_specs=[pl.BlockSpec((ROW_TILE, PLE_DIM), lambda i: (i, 0)),
                  pl.BlockSpec((PLE_DIM, D_MODEL), lambda i: (0, 0)),
                  pl.BlockSpec((1, D_MODEL), lambda i: (0, 0))],
        out_specs=pl.BlockSpec((ROW_TILE, D_MODEL), lambda i: (i, 0)),
        out_shape=jax.ShapeDtypeStruct((m, D_MODEL), F32),
        compiler_params=_cparams(("parallel",)),
        name="ple_embed",
    )(p, w_proj, norm_w.reshape(1, D_MODEL))


def _mm_kernel(x_ref, w_ref, *rest, epilogue):
    wbf_ref = rest[-1]
    refs = rest[:-1]

    @pl.when(pl.program_id(1) == 0)
    def _():
        wbf_ref[...] = w_ref[...].astype(BF16)

    acc = jnp.dot(x_ref[...].astype(BF16), wbf_ref[...], preferred_element_type=F32)
    if epilogue == "plain":
        (o_ref,) = refs
        o_ref[...] = acc
    elif epilogue == "residual":
        res_ref, o_ref, obf_ref = refs
        h = res_ref[...] + acc
        o_ref[...] = h
        obf_ref[...] = h.astype(BF16)
    elif epilogue == "ple_gate":
        res_ref, e_ref, o_ref = refs
        o_ref[...] = res_ref[...] + e_ref[...] * jax.nn.sigmoid(acc)
    else:
        raise ValueError(epilogue)


def _matmul(x, w, n_cols, epilogue, extras=()):
    m, k = x.shape
    tile = pl.BlockSpec((ROW_TILE, COL_TILE), lambda j, i: (i, j))
    in_specs = [pl.BlockSpec((ROW_TILE, k), lambda j, i: (i, 0)),
                pl.BlockSpec((k, COL_TILE), lambda j, i: (0, j))] + [tile] * len(extras)
    out_shape = [jax.ShapeDtypeStruct((m, n_cols), F32)]
    if epilogue == "residual":
        out_shape.append(jax.ShapeDtypeStruct((m, n_cols), BF16))
    out = pl.pallas_call(
        functools.partial(_mm_kernel, epilogue=epilogue),
        grid=(n_cols // COL_TILE, m // ROW_TILE),
        in_specs=in_specs,
        out_specs=[tile] * len(out_shape),
        out_shape=out_shape,
        scratch_shapes=[pltpu.VMEM((k, COL_TILE), BF16)],
        compiler_params=_cparams(("parallel", "arbitrary")),
        name="matmul_" + epilogue,
    )(x, w, *extras)
    return out if epilogue == "residual" else out[0]


def _cumsum_operands(n_prompt_tiles):
    is_prompt = pl.program_id(0) < n_prompt_tiles
    shift = jnp.where(is_prompt, CHUNK_T.bit_length() - 1, SEQ_PAD.bit_length() - 1)
    ii = _iota2((ROW_TILE, ROW_TILE), 0)
    jj = _iota2((ROW_TILE, ROW_TILE), 1)
    same = lax.shift_right_logical(ii, shift) == lax.shift_right_logical(jj, shift)
    lower = (same & (jj <= ii)).astype(F32)
    upper = (same & (ii <= jj)).astype(F32)
    return is_prompt, lower, upper


def _dot_f32(a, b):
    return jnp.dot(a, b, preferred_element_type=F32, precision=lax.Precision.HIGHEST)


def _gdn_gate_kernel(x_ref, wa_ref, wb_ref, wat_ref, alog_ref, dtb_ref, alogt_ref, dtbt_ref,
                     beta_ref, gc_ref, gct_ref, *, n_prompt_tiles):
    x = x_ref[...]
    is_prompt, lower, upper = _cumsum_operands(n_prompt_tiles)
    live_rows = is_prompt | _real_row_mask((ROW_TILE, LANES), 0)
    live_cols = is_prompt | _real_row_mask((GDN_V_HEADS, ROW_TILE), 1)
    a = _dot(x, wa_ref[...])
    g = jnp.where(live_rows, -jnp.exp(alog_ref[...]) * _softplus(a + dtb_ref[...]), 0.0)
    at = _dot_nt(wat_ref[...], x)
    gt = jnp.where(live_cols, -jnp.exp(alogt_ref[...]) * _softplus(at + dtbt_ref[...]), 0.0)
    beta_ref[...] = jnp.where(live_rows, jax.nn.sigmoid(_dot(x, wb_ref[...])), 0.0)
    gc_ref[...] = _dot_f32(lower, g)
    gct_ref[...] = _dot_f32(gt, upper)


def _pad_lanes(x):
    return jnp.pad(x, ((0, 0), (0, LANES - x.shape[1])))


def _gdn_gates(hn, w_in, a_log, dt_bias, n_prompt_rows):
    m = hn.shape[0]
    wa = w_in[:, GDN_MAIN:GDN_MAIN + GDN_V_HEADS]
    wb = w_in[:, GDN_MAIN + GDN_V_HEADS:]
    row = pl.BlockSpec((ROW_TILE, LANES), lambda i: (i, 0))
    full = lambda shape: pl.BlockSpec(shape, lambda i: (0, 0))
    return pl.pallas_call(
        functools.partial(_gdn_gate_kernel, n_prompt_tiles=n_prompt_rows // ROW_TILE),
        grid=(m // ROW_TILE,),
        in_specs=[pl.BlockSpec((ROW_TILE, D_MODEL), lambda i: (i, 0)),
                  full((D_MODEL, LANES)), full((D_MODEL, LANES)), full((GDN_V_HEADS, D_MODEL)),
                  full((1, LANES)), full((1, LANES)), full((GDN_V_HEADS, 1)), full((GDN_V_HEADS, 1))],
        out_specs=[row, row, pl.BlockSpec((GDN_V_HEADS, ROW_TILE), lambda i: (0, i))],
        out_shape=[jax.ShapeDtypeStruct((m, LANES), F32)] * 2
        + [jax.ShapeDtypeStruct((GDN_V_HEADS, m), F32)],
        compiler_params=_cparams(("parallel",)),
        name="gdn_gates",
    )(hn, _pad_lanes(wa), _pad_lanes(wb), wa.T,
      _pad_lanes(a_log.reshape(1, -1)), _pad_lanes(dt_bias.reshape(1, -1)),
      a_log.reshape(-1, 1), dt_bias.reshape(-1, 1))


def _ssd_gate_kernel(x_ref, w_ref, wt_ref, alog_ref, dtb_ref, alogt_ref, dtbt_ref,
                     dt_ref, acs_ref, acst_ref, *, n_prompt_tiles):
    x = x_ref[...]
    is_prompt, lower, upper = _cumsum_operands(n_prompt_tiles)
    live_rows = is_prompt | _real_row_mask((ROW_TILE, LANES), 0)
    live_cols = is_prompt | _real_row_mask((SSD_HEADS, ROW_TILE), 1)
    dt = jnp.where(live_rows, _softplus(_dot(x, w_ref[...]) + dtb_ref[...]), 0.0)
    dtt = jnp.where(live_cols, _softplus(_dot_nt(wt_ref[...], x) + dtbt_ref[...]), 0.0)
    dt_ref[...] = dt
    acs_ref[...] = _dot_f32(lower, dt * -jnp.exp(alog_ref[...]))
    acst_ref[...] = _dot_f32(dtt * -jnp.exp(alogt_ref[...]), upper)


def _ssd_gates(hn, w_in, a_log, dt_bias, n_prompt_rows):
    m = hn.shape[0]
    w = w_in[:, SSD_MAIN:]
    row = pl.BlockSpec((ROW_TILE, LANES), lambda i: (i, 0))
    full = lambda shape: pl.BlockSpec(shape, lambda i: (0, 0))
    return pl.pallas_call(
        functools.partial(_ssd_gate_kernel, n_prompt_tiles=n_prompt_rows // ROW_TILE),
        grid=(m // ROW_TILE,),
        in_specs=[pl.BlockSpec((ROW_TILE, D_MODEL), lambda i: (i, 0)),
                  full((D_MODEL, LANES)), full((SSD_HEADS, D_MODEL)),
                  full((1, LANES)), full((1, LANES)), full((SSD_HEADS, 1)), full((SSD_HEADS, 1))],
        out_specs=[row, row, pl.BlockSpec((SSD_HEADS, ROW_TILE), lambda i: (0, i))],
        out_shape=[jax.ShapeDtypeStruct((m, LANES), F32)] * 2
        + [jax.ShapeDtypeStruct((SSD_HEADS, m), F32)],
        compiler_params=_cparams(("parallel",)),
        name="ssd_gates",
    )(hn, _pad_lanes(w), w.T,
      _pad_lanes(a_log.reshape(1, -1)), _pad_lanes(dt_bias.reshape(1, -1)),
      a_log.reshape(-1, 1), dt_bias.reshape(-1, 1))


def _l2norm_rows(x):
    return x * lax.rsqrt(jnp.sum(x * x, axis=-1, keepdims=True) + EPS)


def _gdn_kernel(*refs, sample):
    if sample:
        (q_ref, k_ref, v_ref, z_ref, beta_ref, gc_ref, gct_ref, cwq_ref, cwk_ref, cwv_ref, nw_ref,
         cq_ref, ck_ref, cv_ref, s0_ref, y_ref, so_ref, bq, bk, bv) = refs
        real = _real_row_mask((CHUNK_T, 1), 0)
        _zero_conv_history((bq, bk, bv))
        xq = jnp.where(real, q_ref[...], cq_ref[...])
        xk = jnp.where(real, k_ref[...], ck_ref[...])
        xv = jnp.where(real, v_ref[...], cv_ref[...])
    else:
        (q_ref, k_ref, v_ref, z_ref, beta_ref, gc_ref, gct_ref, cwq_ref, cwk_ref, cwv_ref, nw_ref,
         y_ref, so_ref, bq, bk, bv, s_scr) = refs

        @pl.when(pl.program_id(2) == 0)
        def _():
            _zero_conv_history((bq, bk, bv))
            s_scr[...] = jnp.zeros(s_scr.shape, F32)
        xq, xk, xv = q_ref[...], k_ref[...], v_ref[...]
    hq = pl.program_id(1)

    q = _l2norm_rows(jax.nn.silu(_causal_conv(bq, xq, cwq_ref))) * (GDN_HEAD ** -0.5)
    k = _l2norm_rows(jax.nn.silu(_causal_conv(bk, xk, cwk_ref)))
    v2 = jax.nn.silu(_causal_conv(bv, xv, cwv_ref))
    z2 = z_ref[...]

    incl, strict = _chunk_masks(sample)
    kk = _dot_nt(k, k)
    qk = _dot_nt(q, k)
    beta_tile, gc_tile = beta_ref[...], gc_ref[...]
    nw = nw_ref[...]

    for j in range(2):
        hv = 2 * hq + j
        cols = slice(j * GDN_HEAD, (j + 1) * GDN_HEAD)
        beta_c = _pick_lane(beta_tile, hv)
        gc_c = _pick_lane(gc_tile, hv)
        gc_r = gct_ref[pl.ds(hv % SUBLANES, 1), :]
        decay = jnp.exp(jnp.where(incl, gc_c - gc_r, NEG_BIG))
        a_mat = jnp.where(strict, beta_c * kk * decay, 0.0)
        t_mat = _unit_lower_inverse(a_mat, sample)
        egc = jnp.exp(gc_c)
        rhs = jnp.concatenate([v2[:, cols] * beta_c, k * (beta_c * egc)], axis=1)
        uw = _dot(t_mat, rhs)
        u, w = uw[:, :GDN_HEAD], uw[:, GDN_HEAD:]
        qk_d = qk * decay
        q_dec = q * egc
        gl_c = _seq_last_rows(gc_c) if sample else gc_c[CHUNK_T - 1:CHUNK_T, :]
        kd_t = (k * jnp.exp(gl_c - gc_c)).T

        if sample:
            ws, qs = [], []
            for s in range(SEQS_PER_CHUNK):
                rows = slice(s * SEQ_PAD, (s + 1) * SEQ_PAD)
                wq = _dot(jnp.concatenate([w[rows], q_dec[rows]], axis=0), s0_ref[s, cols, :])
                ws.append(wq[:SEQ_PAD])
                qs.append(wq[SEQ_PAD:])
            v_new = u - jnp.concatenate(ws, axis=0)
            o = jnp.concatenate(qs, axis=0) + _dot(qk_d, v_new)
            v_new_bf = v_new.astype(BF16)
            seq_of_lane = _iota2((GDN_HEAD, CHUNK_T), 1) // SEQ_PAD
            for s in range(SEQS_PER_CHUNK):
                dec = jnp.exp(gl_c[s * SEQ_PAD:s * SEQ_PAD + 1, :])
                upd = _dot(jnp.where(seq_of_lane == s, kd_t, 0.0), v_new_bf)
                so_ref[s, cols, :] = s0_ref[s, cols, :] * dec + upd
        else:
            s_old = s_scr[j]
            wq = _dot(jnp.concatenate([w, q_dec], axis=0), s_old)
            v_new = u - wq[:CHUNK_T]
            o = wq[CHUNK_T:] + _dot(qk_d, v_new)
            s_new = s_old * jnp.exp(gl_c) + _dot(kd_t, v_new)
            s_scr[j] = s_new
            so_ref[0, cols, :] = s_new

        y = _rmsnorm_rows(o, nw) * jax.nn.silu(z2[:, cols])
        y_ref[:, cols] = y.astype(y_ref.dtype)


def _gdn_mixer(proj, gates, conv_w, norm_w, n_prompt_rows, prompt_len, cache8=None, state=None):
    sample = state is not None
    beta, gc, gct = gates
    hb = GDN_HEAD
    qkb = GDN_QK_W // hb
    vb = (2 * GDN_QK_W) // (2 * hb)
    zb = GDN_CONV_DIM // (2 * hb)
    if sample:
        n_seq = state.shape[0]
        grid = (n_seq // SEQS_PER_CHUNK, GDN_QK_HEADS)
        base = n_prompt_rows // CHUNK_T
        rb = lambda i, h: base + i
        n_rows = n_seq * SEQ_PAD
        sem = ("parallel", "parallel")
    else:
        n_batch = n_prompt_rows // prompt_len
        chunks = prompt_len // CHUNK_T
        grid = (n_batch, GDN_QK_HEADS, chunks)
        rb = lambda b, h, c: b * chunks + c
        n_rows = n_prompt_rows
        sem = ("parallel", "parallel", "arbitrary")

    def spec(width, col_off, row_fn=rb):
        return pl.BlockSpec((CHUNK_T, width), lambda *a: (row_fn(*a), col_off + a[1]))

    row128 = pl.BlockSpec((CHUNK_T, LANES), lambda *a: (rb(*a), 0))
    in_specs = [spec(hb, 0), spec(hb, qkb), spec(2 * hb, vb), spec(2 * hb, zb),
                row128, row128,
                pl.BlockSpec((SUBLANES, CHUNK_T), lambda *a: ((2 * a[1]) // SUBLANES, rb(*a))),
                pl.BlockSpec((CONV_W, hb), lambda *a: (0, a[1])),
                pl.BlockSpec((CONV_W, hb), lambda *a: (0, qkb + a[1])),
                pl.BlockSpec((CONV_W, 2 * hb), lambda *a: (0, vb + a[1])),
                pl.BlockSpec((1, hb), lambda *a: (0, 0))]
    args = [proj, proj, proj, proj, beta, gc, gct, conv_w, conv_w, conv_w, norm_w.reshape(1, hb)]
    scratch = [pltpu.VMEM((CHUNK_T + SUBLANES, hb), F32), pltpu.VMEM((CHUNK_T + SUBLANES, hb), F32),
               pltpu.VMEM((CHUNK_T + SUBLANES, 2 * hb), F32)]
    if sample:
        crb = lambda i, h: i
        s_spec = pl.BlockSpec((SEQS_PER_CHUNK, 2 * hb, hb), lambda i, h: (i, h, 0))
        in_specs += [spec(hb, 0, crb), spec(hb, qkb, crb), spec(2 * hb, vb, crb), s_spec]
        state2 = state.reshape(n_seq, GDN_V_HEADS * hb, hb)
        args += [cache8, cache8, cache8, state2]
        s_shape = jax.ShapeDtypeStruct(state2.shape, F32)
    else:
        s_spec = pl.BlockSpec((1, 2 * hb, hb), lambda b, h, c: (b, h, 0))
        s_shape = jax.ShapeDtypeStruct((n_batch, GDN_V_HEADS * hb, hb), F32)
        scratch.append(pltpu.VMEM((2, hb, hb), F32))
    y_spec = pl.BlockSpec((CHUNK_T, 2 * hb), lambda *a: (rb(*a) - (base if sample else 0), a[1]))
    y, s_out = pl.pallas_call(
        functools.partial(_gdn_kernel, sample=sample),
        grid=grid,
        in_specs=in_specs,
        out_specs=[y_spec, s_spec],
        out_shape=[jax.ShapeDtypeStruct((n_rows, GDN_V_W), BF16), s_shape],
        scratch_shapes=scratch,
        compiler_params=_cparams(sem),
        name="gdn_sample" if sample else "gdn_prompt",
    )(*args)
    return y, s_out.reshape(s_out.shape[0], GDN_V_HEADS, hb, hb)


def _ssd_kernel(*refs, sample):
    if sample:
        (z_ref, x_ref, b_ref, c_ref, dt_ref, acs_ref, acst_ref, cwx_ref, cwb_ref, cwc_ref,
         cbx_ref, cbb_ref, cbc_ref, de_ref, nw_ref, cx_ref, cb_ref, cc_ref, s0_ref,
         y_ref, so_ref, bx, bb, bc) = refs
        real = _real_row_mask((CHUNK_T, 1), 0)
        _zero_conv_history((bx, bb, bc))
        xx = jnp.where(real, x_ref[...], cx_ref[...])
        xb = jnp.where(real, b_ref[...], cb_ref[...])
        xc = jnp.where(real, c_ref[...], cc_ref[...])
    else:
        (z_ref, x_ref, b_ref, c_ref, dt_ref, acs_ref, acst_ref, cwx_ref, cwb_ref, cwc_ref,
         cbx_ref, cbb_ref, cbc_ref, de_ref, nw_ref,
         y_ref, so_ref, bx, bb, bc, s_scr) = refs

        @pl.when(pl.program_id(2) == 0)
        def _():
            _zero_conv_history((bx, bb, bc))
            s_scr[...] = jnp.zeros(s_scr.shape, F32)
        xx, xb, xc = x_ref[...], b_ref[...], c_ref[...]
    grp = pl.program_id(1)

    xg = jax.nn.silu(_causal_conv(bx, xx, cwx_ref) + cbx_ref[...])
    bg = jax.nn.silu(_causal_conv(bb, xb, cwb_ref) + cbb_ref[...])
    cg = jax.nn.silu(_causal_conv(bc, xc, cwc_ref) + cbc_ref[...])

    dt_tile, acs_tile = dt_ref[...], acs_ref[...]
    sel = (_iota2((LANES, SSD_GW), 0) == grp * SSD_HPG + _iota2((LANES, SSD_GW), 1) // SSD_HEAD)
    parts = jnp.concatenate(_split3(dt_tile) + _split3(acs_tile), axis=0)
    ex = jnp.dot(parts, sel.astype(BF16), preferred_element_type=F32)
    t = CHUNK_T
    dt_e = ex[0:t] + ex[t:2 * t] + ex[2 * t:3 * t]
    acs_e = ex[3 * t:4 * t] + ex[4 * t:5 * t] + ex[5 * t:6 * t]
    acs_last_e = _seq_last_rows(acs_e) if sample else acs_e[t - 1:t, :]

    xdt = xg * dt_e
    incl, _ = _chunk_masks(sample)
    cb = _dot_nt(cg, bg)
    lane_lo = _iota2((t, 2 * SSD_HEAD), 1) < SSD_HEAD
    pairs = []
    for pr in range(SSD_HPG // 2):
        m = []
        for r in (2 * pr, 2 * pr + 1):
            acs_c = _pick_lane(acs_tile, grp * SSD_HPG + r)
            acs_r = acst_ref[r:r + 1, :]
            m.append(cb * jnp.exp(jnp.where(incl, acs_c - acs_r, NEG_BIG)))
        xpair = xdt[:, pr * 2 * SSD_HEAD:(pr + 1) * 2 * SSD_HEAD]
        rhs = jnp.concatenate([jnp.where(lane_lo, xpair, 0.0), jnp.where(lane_lo, 0.0, xpair)],
                              axis=0)
        pairs.append(_dot(jnp.concatenate(m, axis=1), rhs))
    y_diag = jnp.concatenate(pairs, axis=1)

    xd_t = (xdt * jnp.exp(acs_last_e - acs_e)).T

    def head_scale(col):
        eb = jnp.broadcast_to(jnp.exp(acst_ref[:, col:col + 1]), (SSD_HPG, SSD_STATE))
        return jnp.concatenate([jnp.broadcast_to(eb[r:r + 1, :], (SSD_HEAD, SSD_STATE))
                                for r in range(SSD_HPG)], axis=0)

    if sample:
        offs = []
        zeros = jnp.zeros((SEQ_PAD, SSD_STATE), F32)
        seq_of_lane = _iota2((SSD_GW, t), 1) // SEQ_PAD
        bg_bf = bg.astype(BF16)
        for s in range(SEQS_PER_CHUNK):
            rows = slice(s * SEQ_PAD, (s + 1) * SEQ_PAD)
            s_old = s0_ref[s]
            offs.append(_dot_nt(jnp.concatenate([cg[rows], zeros], axis=0), s_old)[:SEQ_PAD])
            upd = _dot(jnp.where(seq_of_lane == s, xd_t, 0.0), bg_bf)
            so_ref[s] = s_old * head_scale((s + 1) * SEQ_PAD - 1) + upd
        y_off = jnp.concatenate(offs, axis=0)
    else:
        s_old = s_scr[...]
        y_off = _dot_nt(cg, s_old)
        s_new = s_old * head_scale(t - 1) + _dot(xd_t, bg)
        s_scr[...] = s_new
        so_ref[0] = s_new

    y = y_diag + y_off * jnp.exp(acs_e) + de_ref[...] * xg
    y = y * jax.nn.silu(z_ref[...])
    y = y * lax.rsqrt(jnp.mean(y * y, axis=-1, keepdims=True) + EPS) * nw_ref[...]
    y_ref[...] = y.astype(y_ref.dtype)


def _ssd_mixer(proj, gates, conv_w, conv_b, d_skip, norm_w, n_prompt_rows, prompt_len,
               cache8=None, state=None):
    sample = state is not None
    dt, acs, acst = gates
    gw, n = SSD_GW, SSD_STATE
    xb = SSD_D_INNER // gw
    bb = (2 * SSD_D_INNER) // n
    cb = bb + SSD_GN // n
    wbb = SSD_D_INNER // n
    wcb = wbb + SSD_GN // n
    if sample:
        n_seq = state.shape[0]
        grid = (n_seq // SEQS_PER_CHUNK, SSD_GROUPS)
        base = n_prompt_rows // CHUNK_T
        rb = lambda i, g: base + i
        n_rows = n_seq * SEQ_PAD
        sem = ("parallel", "parallel")
    else:
        n_batch = n_prompt_rows // prompt_len
        chunks = prompt_len // CHUNK_T
        grid = (n_batch, SSD_GROUPS, chunks)
        base = 0
        rb = lambda b, g, c: b * chunks + c
        n_rows = n_prompt_rows
        sem = ("parallel", "parallel", "arbitrary")

    def spec(width, col_off, row_fn=rb):
        return pl.BlockSpec((CHUNK_T, width), lambda *a: (row_fn(*a), col_off + a[1]))

    def wspec(rows, width, col_off):
        return pl.BlockSpec((rows, width), lambda *a: (0, col_off + a[1]))

    row128 = pl.BlockSpec((CHUNK_T, LANES), lambda *a: (rb(*a), 0))
    in_specs = [spec(gw, 0), spec(gw, xb), spec(n, bb), spec(n, cb), row128, row128,
                pl.BlockSpec((SSD_HPG, CHUNK_T), lambda *a: (a[1], rb(*a))),
                wspec(CONV_W, gw, 0), wspec(CONV_W, n, wbb), wspec(CONV_W, n, wcb),
                wspec(1, gw, 0), wspec(1, n, wbb), wspec(1, n, wcb),
                wspec(1, gw, 0), wspec(1, gw, 0)]
    conv_b2 = conv_b.reshape(1, -1)
    d_e = jnp.repeat(d_skip, SSD_HEAD).reshape(1, SSD_D_INNER)
    args = [proj, proj, proj, proj, dt, acs, acst, conv_w, conv_w, conv_w,
            conv_b2, conv_b2, conv_b2, d_e, norm_w.reshape(1, SSD_D_INNER)]
    scratch = [pltpu.VMEM((CHUNK_T + SUBLANES, gw), F32), pltpu.VMEM((CHUNK_T + SUBLANES, n), F32),
               pltpu.VMEM((CHUNK_T + SUBLANES, n), F32)]
    if sample:
        crb = lambda i, g: i
        s_spec = pl.BlockSpec((SEQS_PER_CHUNK, gw, n), lambda i, g: (i, g, 0))
        in_specs += [spec(gw, 0, crb), spec(n, wbb, crb), spec(n, wcb, crb), s_spec]
        state2 = state.reshape(n_seq, SSD_HEADS * SSD_HEAD, n)
        args += [cache8, cache8, cache8, state2]
        s_shape = jax.ShapeDtypeStruct(state2.shape, F32)
    else:
        s_spec = pl.BlockSpec((1, gw, n), lambda b, g, c: (b, g, 0))
        s_shape = jax.ShapeDtypeStruct((n_batch, SSD_HEADS * SSD_HEAD, n), F32)
        scratch.append(pltpu.VMEM((gw, n), F32))
    y_spec = pl.BlockSpec((CHUNK_T, gw), lambda *a: (rb(*a) - base, a[1]))
    y, s_out = pl.pallas_call(
        functools.partial(_ssd_kernel, sample=sample),
        grid=grid,
        in_specs=in_specs,
        out_specs=[y_spec, s_spec],
        out_shape=[jax.ShapeDtypeStruct((n_rows, SSD_D_INNER), BF16), s_shape],
        scratch_shapes=scratch,
        compiler_params=_cparams(sem),
        name="ssd_sample" if sample else "ssd_prompt",
    )(*args)
    return y, s_out.reshape(s_out.shape[0], SSD_HEADS, SSD_HEAD, n)


def _pad_sequences(x, n_real):
    pads = [(0, 0)] * (x.ndim - 2) + [(SEQ_PAD - n_real, 0), (0, 0)]
    xp = jnp.pad(x, pads)
    return xp.reshape(x.shape[:-3] + (x.shape[-3] * SEQ_PAD, x.shape[-1]))


def _history_rows(cache):
    n_real = SEQ_PAD // 2
    lo = SEQ_PAD - n_real - (CONV_W - 1)
    xp = jnp.pad(cache, ((0, 0), (lo, n_real), (0, 0)))
    return xp.reshape(cache.shape[0] * SEQ_PAD, cache.shape[2])


def kernel(x_prompt, x_sample, state_gdn, cache_gdn_conv, state_ssd, cache_ssd_conv, p_prompt, p_sample,
           norm_w, gdn_w_in, gdn_conv_w, gdn_A_log, gdn_dt_bias, gdn_norm_w, gdn_w_out,
           ssd_w_in, ssd_conv_w, ssd_conv_b, ssd_A_log, ssd_dt_bias, ssd_D, ssd_norm_w, ssd_w_out,
           ple_w_proj, ple_w_gate, ple_norm_w, final_norm_w):
    bp, lp, d = x_prompt.shape
    bs, ls, _ = x_sample.shape
    depth = norm_w.shape[0]
    assert d == D_MODEL and ls == SEQ_PAD // 2 and lp % CHUNK_T == 0 and bs % SEQS_PER_CHUNK == 0
    mp, ms = bp * lp, bs * SEQ_PAD
    assert mp % ROW_TILE == 0 and ms % ROW_TILE == 0

    h = jnp.concatenate([x_prompt.reshape(mp, d), _pad_sequences(x_sample, ls)], axis=0)
    p = jnp.concatenate([p_prompt.reshape(depth, mp, PLE_DIM), _pad_sequences(p_sample, ls)], axis=1)

    def conv_caches(proj, lo, hi):
        c_p = proj[:mp].reshape(bp, lp, -1)[:, lp - (CONV_W - 1):, lo:hi]
        c_s = proj[mp:].reshape(bs, SEQ_PAD, -1)[:, SEQ_PAD - (CONV_W - 1):, lo:hi]
        return c_p, c_s

    outs = {k: [] for k in ("sg_p", "cg_p", "ss_p", "cs_p", "sg_s", "cg_s", "ss_s", "cs_s")}
    hn = _norm_rows(h, norm_w[0], BF16)
    for i in range(depth):
        j = i // 2
        if i % 2 == 0:
            proj = _matmul(hn, gdn_w_in[j], GDN_MAIN, "plain")
            gates = _gdn_gates(hn, gdn_w_in[j], gdn_A_log[j], gdn_dt_bias[j], mp)
            y_p, s_p = _gdn_mixer(proj, gates, gdn_conv_w[j], gdn_norm_w[j], mp, lp)
            y_s, s_s = _gdn_mixer(proj, gates, gdn_conv_w[j], gdn_norm_w[j], mp, lp,
                                  _history_rows(cache_gdn_conv[j]), state_gdn[j])
            c_p, c_s = conv_caches(proj, 0, GDN_CONV_DIM)
            w_out = gdn_w_out[j]
            keys = ("sg_p", "cg_p", "sg_s", "cg_s")
        else:
            proj = _matmul(hn, ssd_w_in[j], SSD_MAIN, "plain")
            gates = _ssd_gates(hn, ssd_w_in[j], ssd_A_log[j], ssd_dt_bias[j], mp)
            common = (proj, gates, ssd_conv_w[j], ssd_conv_b[j], ssd_D[j], ssd_norm_w[j], mp, lp)
            y_p, s_p = _ssd_mixer(*common)
            y_s, s_s = _ssd_mixer(*common, _history_rows(cache_ssd_conv[j]), state_ssd[j])
            c_p, c_s = conv_caches(proj, SSD_D_INNER, SSD_MAIN)
            w_out = ssd_w_out[j]
            keys = ("ss_p", "cs_p", "ss_s", "cs_s")
        for key, val in zip(keys, (s_p, c_p, s_s, c_s)):
            outs[key].append(val)
        y = jnp.concatenate([y_p, y_s], axis=0)
        h_mid, h_mid_bf = _matmul(y, w_out, D_MODEL, "residual", extras=(h,))
        e = _ple_embed(p[i], ple_w_proj[i], ple_norm_w[i])
        h = _matmul(h_mid_bf, ple_w_gate[i], D_MODEL, "ple_gate", extras=(h_mid, e))
        last = i + 1 == depth
        hn = _norm_rows(h, final_norm_w if last else norm_w[i + 1], F32 if last else BF16)

    y_prompt = hn[:mp].reshape(bp, lp, d)
    y_sample = hn[mp:].reshape(bs, SEQ_PAD, d)[:, SEQ_PAD - ls:, :]
    st = {k: jnp.stack(v) for k, v in outs.items()}
    return (y_prompt, y_sample, st["sg_p"], st["cg_p"], st["ss_p"], st["cs_p"],
            st["sg_s"], st["cg_s"], st["ss_s"], st["cs_s"])
```

```python
import functools
from typing import NamedTuple

import jax
import jax.numpy as jnp
from jax import lax
from jax.experimental import pallas as pl
from jax.experimental.pallas import tpu as pltpu

F32 = jnp.float32
BF16 = jnp.bfloat16

EPS = 1e-6
CONV_W = 4
D_MODEL = 2048
PLE_DIM = 256

GDN_HEAD = 128
GDN_QK_HEADS = 16
GDN_V_HEADS = 32
GDN_QK_W = GDN_QK_HEADS * GDN_HEAD
GDN_V_W = GDN_V_HEADS * GDN_HEAD
GDN_CONV_DIM = 2 * GDN_QK_W + GDN_V_W
GDN_MAIN = GDN_CONV_DIM + GDN_V_W

SSD_D_INNER = 4096
SSD_HEAD = 64
SSD_HEADS = 64
SSD_STATE = 128
SSD_GROUPS = 8
SSD_HPG = SSD_HEADS // SSD_GROUPS
SSD_GN = SSD_GROUPS * SSD_STATE
SSD_CONV_DIM = SSD_D_INNER + 2 * SSD_GN
SSD_MAIN = SSD_D_INNER + SSD_CONV_DIM
SSD_GW = SSD_HPG * SSD_HEAD

LANES = 128
SUBLANES = 8
CHUNK_T = 128
SEQ_PAD = 8
SEQS_PER_CHUNK = CHUNK_T // SEQ_PAD
INV_BLOCK = 16
GDN_HEADS_PER_STEP = 2
ROW_TILE = 512
COL_TILE = 512
VMEM_LIMIT = 56 * 1024 * 1024
NEG_BIG = -1e30


def _cparams(sem):
    return pltpu.CompilerParams(dimension_semantics=sem, vmem_limit_bytes=VMEM_LIMIT)


def _dot(a, b):
    return jnp.dot(a.astype(BF16), b.astype(BF16), preferred_element_type=F32)


def _dot_nt(a, b):
    return lax.dot_general(a.astype(BF16), b.astype(BF16), (((1,), (1,)), ((), ())),
                           preferred_element_type=F32)


def _split3(x):
    x1 = x.astype(BF16)
    r1 = x - x1.astype(F32)
    x2 = r1.astype(BF16)
    r2 = r1 - x2.astype(F32)
    return [x1, x2, r2.astype(BF16)]


def _softplus(x):
    return jnp.maximum(x, 0.0) + jnp.log1p(jnp.exp(-jnp.abs(x)))


def _iota2(shape, dim):
    return lax.broadcasted_iota(jnp.int32, shape, dim)


def _pick_lane(x, idx):
    lane = _iota2(x.shape, 1)
    return jnp.sum(jnp.where(lane == idx, x, 0.0), axis=1, keepdims=True)


def _chunk_masks(sample):
    ii = _iota2((CHUNK_T, CHUNK_T), 0)
    jj = _iota2((CHUNK_T, CHUNK_T), 1)
    incl = ii >= jj
    strict = ii > jj
    if sample:
        same = (ii // SEQ_PAD) == (jj // SEQ_PAD)
        incl = incl & same
        strict = strict & same
    return incl, strict


def _seq_last_rows(x):
    rows, cols = x.shape
    x3 = x.reshape(rows // SEQ_PAD, SEQ_PAD, cols)
    last = jnp.broadcast_to(x3[:, SEQ_PAD - 1:SEQ_PAD, :], x3.shape)
    return last.reshape(rows, cols)


def _unit_lower_inverses(mats, sample):
    ii = _iota2(mats[0].shape, 0)
    jj = _iota2(mats[0].shape, 1)
    eye = (ii == jj).astype(F32)

    def neumann(xs, squarings):
        ts = [eye - x for x in xs]
        ps = xs
        for _ in range(squarings):
            ps = [_dot(p, p) for p in ps]
            ts = [t + _dot(t, p) for t, p in zip(ts, ps)]
        return ts

    if sample:
        assert SEQ_PAD == 8
        return neumann(mats, 2)
    assert INV_BLOCK == 16 and CHUNK_T // INV_BLOCK == 8
    blk = (ii // INV_BLOCK) == (jj // INV_BLOCK)
    ds = [jnp.where(blk, a, 0.0) for a in mats]
    tds = neumann(ds, 3)
    ns = [_dot(td, a - d) for td, a, d in zip(tds, mats, ds)]
    return [_dot(p, td) for p, td in zip(neumann(ns, 2), tds)]


def _causal_conv(buf_ref, x, w_ref):
    t = x.shape[0]
    buf_ref[SUBLANES:SUBLANES + t, :] = x
    w = w_ref[...]
    y = buf_ref[pl.ds(SUBLANES - 3, t), :] * w[0:1, :]
    for j in range(1, CONV_W):
        y = y + buf_ref[pl.ds(SUBLANES - 3 + j, t), :] * w[j:j + 1, :]
    buf_ref[0:SUBLANES, :] = buf_ref[t:t + SUBLANES, :]
    return y


def _zero_conv_history(bufs):
    for b in bufs:
        b[0:SUBLANES, :] = jnp.zeros((SUBLANES, b.shape[1]), F32)


def _real_row_mask(shape, dim):
    return (_iota2(shape, dim) % SEQ_PAD) >= (SEQ_PAD // 2)


def _rmsnorm_rows(x, w):
    return x * lax.rsqrt(jnp.mean(x * x, axis=-1, keepdims=True) + EPS) * w


def _norm_kernel(x_ref, w_ref, o_ref):
    o_ref[...] = _rmsnorm_rows(x_ref[...], w_ref[...]).astype(o_ref.dtype)


def _norm_rows(x, w, out_dtype):
    m, d = x.shape
    return pl.pallas_call(
        _norm_kernel,
        grid=(m // ROW_TILE,),
        in_specs=[pl.BlockSpec((ROW_TILE, d), lambda i: (i, 0)),
                  pl.BlockSpec((1, d), lambda i: (0, 0))],
        out_specs=pl.BlockSpec((ROW_TILE, d), lambda i: (i, 0)),
        out_shape=jax.ShapeDtypeStruct((m, d), out_dtype),
        compiler_params=_cparams(("parallel",)),
        name="norm_rows",
    )(x, w.reshape(1, d))


def _ple_embed_kernel(p_ref, w_ref, nw_ref, o_ref):
    e = _dot(p_ref[...], w_ref[...])
    o_ref[...] = _rmsnorm_rows(e, nw_ref[...])


def _ple_embed(p, w_proj, layer, norm_w):
    m = p.shape[1]
    return pl.pallas_call(
        _ple_embed_kernel,
        grid=(m // ROW_TILE,),
        in_specs=[pl.BlockSpec((None, ROW_TILE, PLE_DIM), lambda i: (layer, i, 0)),
                  pl.BlockSpec((None, PLE_DIM, D_MODEL), lambda i: (layer, 0, 0)),
                  pl.BlockSpec((1, D_MODEL), lambda i: (0, 0))],
        out_specs=pl.BlockSpec((ROW_TILE, D_MODEL), lambda i: (i, 0)),
        out_shape=jax.ShapeDtypeStruct((m, D_MODEL), F32),
        compiler_params=_cparams(("parallel",)),
        name="ple_embed",
    )(p, w_proj, norm_w.reshape(1, D_MODEL))


def _mm_kernel(x_ref, w_ref, *rest, epilogue):
    wbf_ref = rest[-1]
    refs = rest[:-1]

    @pl.when(pl.program_id(1) == 0)
    def _():
        wbf_ref[...] = w_ref[...].astype(BF16)

    acc = jnp.dot(x_ref[...].astype(BF16), wbf_ref[...], preferred_element_type=F32)
    if epilogue == "plain":
        (o_ref,) = refs
        o_ref[...] = acc
    elif epilogue == "residual":
        res_ref, o_ref, obf_ref = refs
        h = res_ref[...] + acc
        o_ref[...] = h
        obf_ref[...] = h.astype(BF16)
    elif epilogue == "ple_gate":
        res_ref, e_ref, o_ref = refs
        o_ref[...] = res_ref[...] + e_ref[...] * jax.nn.sigmoid(acc)
    else:
        raise ValueError(epilogue)


def _matmul(x, w, layer, n_cols, epilogue, extras=()):
    m, k = x.shape
    tile = pl.BlockSpec((ROW_TILE, COL_TILE), lambda j, i: (i, j))
    in_specs = [pl.BlockSpec((ROW_TILE, k), lambda j, i: (i, 0)),
                pl.BlockSpec((None, k, COL_TILE), lambda j, i: (layer, 0, j))] + [tile] * len(extras)
    out_shape = [jax.ShapeDtypeStruct((m, n_cols), F32)]
    if epilogue == "residual":
        out_shape.append(jax.ShapeDtypeStruct((m, n_cols), BF16))
    out = pl.pallas_call(
        functools.partial(_mm_kernel, epilogue=epilogue),
        grid=(n_cols // COL_TILE, m // ROW_TILE),
        in_specs=in_specs,
        out_specs=[tile] * len(out_shape),
        out_shape=out_shape,
        scratch_shapes=[pltpu.VMEM((k, COL_TILE), BF16)],
        compiler_params=_cparams(("parallel", "arbitrary")),
        name="matmul_" + epilogue,
    )(x, w, *extras)
    return out if epilogue == "residual" else out[0]


def _cumsum_operands(n_prompt_tiles):
    is_prompt = pl.program_id(0) < n_prompt_tiles
    shift = jnp.where(is_prompt, CHUNK_T.bit_length() - 1, SEQ_PAD.bit_length() - 1)
    ii = _iota2((ROW_TILE, ROW_TILE), 0)
    jj = _iota2((ROW_TILE, ROW_TILE), 1)
    same = lax.shift_right_logical(ii, shift) == lax.shift_right_logical(jj, shift)
    lower = (same & (jj <= ii)).astype(F32)
    upper = (same & (ii <= jj)).astype(F32)
    return is_prompt, lower, upper


def _dot_f32(a, b):
    return jnp.dot(a, b, preferred_element_type=F32, precision=lax.Precision.HIGHEST)


def _gdn_gate_kernel(x_ref, wa_ref, wb_ref, wat_ref, alog_ref, dtb_ref, alogt_ref, dtbt_ref,
                     beta_ref, gc_ref, gct_ref, *, n_prompt_tiles):
    x = x_ref[...]
    is_prompt, lower, upper = _cumsum_operands(n_prompt_tiles)
    live_rows = is_prompt | _real_row_mask((ROW_TILE, LANES), 0)
    live_cols = is_prompt | _real_row_mask((GDN_V_HEADS, ROW_TILE), 1)
    a = _dot(x, wa_ref[...])
    g = jnp.where(live_rows, -jnp.exp(alog_ref[...]) * _softplus(a + dtb_ref[...]), 0.0)
    at = _dot_nt(wat_ref[...], x)
    gt = jnp.where(live_cols, -jnp.exp(alogt_ref[...]) * _softplus(at + dtbt_ref[...]), 0.0)
    beta_ref[...] = jnp.where(live_rows, jax.nn.sigmoid(_dot(x, wb_ref[...])), 0.0)
    gc_ref[...] = _dot_f32(lower, g)
    gct_ref[...] = _dot_f32(gt, upper)


def _pad_lanes(x):
    return jnp.pad(x, ((0, 0), (0, LANES - x.shape[1])))


def _gdn_gates(hn, w_ab, a_log, dt_bias, n_prompt_rows):
    m = hn.shape[0]
    wa = w_ab[:, :GDN_V_HEADS]
    wb = w_ab[:, GDN_V_HEADS:]
    row = pl.BlockSpec((ROW_TILE, LANES), lambda i: (i, 0))
    full = lambda shape: pl.BlockSpec(shape, lambda i: (0, 0))
    return pl.pallas_call(
        functools.partial(_gdn_gate_kernel, n_prompt_tiles=n_prompt_rows // ROW_TILE),
        grid=(m // ROW_TILE,),
        in_specs=[pl.BlockSpec((ROW_TILE, D_MODEL), lambda i: (i, 0)),
                  full((D_MODEL, LANES)), full((D_MODEL, LANES)), full((GDN_V_HEADS, D_MODEL)),
                  full((1, LANES)), full((1, LANES)), full((GDN_V_HEADS, 1)), full((GDN_V_HEADS, 1))],
        out_specs=[row, row, pl.BlockSpec((GDN_V_HEADS, ROW_TILE), lambda i: (0, i))],
        out_shape=[jax.ShapeDtypeStruct((m, LANES), F32)] * 2
        + [jax.ShapeDtypeStruct((GDN_V_HEADS, m), F32)],
        compiler_params=_cparams(("parallel",)),
        name="gdn_gates",
    )(hn, _pad_lanes(wa), _pad_lanes(wb), wa.T,
      _pad_lanes(a_log.reshape(1, -1)), _pad_lanes(dt_bias.reshape(1, -1)),
      a_log.reshape(-1, 1), dt_bias.reshape(-1, 1))


def _ssd_gate_kernel(x_ref, w_ref, wt_ref, alog_ref, dtb_ref, alogt_ref, dtbt_ref,
                     dt_ref, acs_ref, acst_ref, *, n_prompt_tiles):
    x = x_ref[...]
    is_prompt, lower, upper = _cumsum_operands(n_prompt_tiles)
    live_rows = is_prompt | _real_row_mask((ROW_TILE, LANES), 0)
    live_cols = is_prompt | _real_row_mask((SSD_HEADS, ROW_TILE), 1)
    dt = jnp.where(live_rows, _softplus(_dot(x, w_ref[...]) + dtb_ref[...]), 0.0)
    dtt = jnp.where(live_cols, _softplus(_dot_nt(wt_ref[...], x) + dtbt_ref[...]), 0.0)
    dt_ref[...] = dt
    acs_ref[...] = _dot_f32(lower, dt * -jnp.exp(alog_ref[...]))
    acst_ref[...] = _dot_f32(dtt * -jnp.exp(alogt_ref[...]), upper)


def _ssd_gates(hn, w, a_log, dt_bias, n_prompt_rows):
    m = hn.shape[0]
    row = pl.BlockSpec((ROW_TILE, LANES), lambda i: (i, 0))
    full = lambda shape: pl.BlockSpec(shape, lambda i: (0, 0))
    return pl.pallas_call(
        functools.partial(_ssd_gate_kernel, n_prompt_tiles=n_prompt_rows // ROW_TILE),
        grid=(m // ROW_TILE,),
        in_specs=[pl.BlockSpec((ROW_TILE, D_MODEL), lambda i: (i, 0)),
                  full((D_MODEL, LANES)), full((SSD_HEADS, D_MODEL)),
                  full((1, LANES)), full((1, LANES)), full((SSD_HEADS, 1)), full((SSD_HEADS, 1))],
        out_specs=[row, row, pl.BlockSpec((SSD_HEADS, ROW_TILE), lambda i: (0, i))],
        out_shape=[jax.ShapeDtypeStruct((m, LANES), F32)] * 2
        + [jax.ShapeDtypeStruct((SSD_HEADS, m), F32)],
        compiler_params=_cparams(("parallel",)),
        name="ssd_gates",
    )(hn, _pad_lanes(w), w.T,
      _pad_lanes(a_log.reshape(1, -1)), _pad_lanes(dt_bias.reshape(1, -1)),
      a_log.reshape(-1, 1), dt_bias.reshape(-1, 1))


def _l2norm_rows(x):
    return x * lax.rsqrt(jnp.sum(x * x, axis=-1, keepdims=True) + EPS)


def _gdn_kernel(*refs, heads, n_prompt_chunks, chunks_per_seq, n_prev):
    (q_ref, k_ref, v_ref, z_ref, beta_ref, gc_ref, gct_ref, cwq_ref, cwk_ref, cwv_ref, nw_ref,
     cq_ref, ck_ref, cv_ref, s0_ref) = refs[:15]
    y_ref, sp_ref, ss_ref, bq, bk, bv, s_scr = refs[15 + n_prev:]
    hg = pl.program_id(0)
    step = pl.program_id(1)
    is_prompt = step < n_prompt_chunks
    gct_row0 = (2 * heads * hg) % gct_ref.shape[0]

    def body(sample, xq, xk, xv):
        qc = jax.nn.silu(_causal_conv(bq, xq, cwq_ref))
        kc = jax.nn.silu(_causal_conv(bk, xk, cwk_ref))
        vc = jax.nn.silu(_causal_conv(bv, xv, cwv_ref))
        incl, strict = _chunk_masks(sample)
        beta_tile, gc_tile = beta_ref[...], gc_ref[...]
        nw = nw_ref[...]
        nv = 2 * heads
        vcols = [slice(lv * GDN_HEAD, (lv + 1) * GDN_HEAD) for lv in range(nv)]
        qs_ = [_l2norm_rows(qc[:, vcols[hh]]) * (GDN_HEAD ** -0.5) for hh in range(heads)]
        ks_ = [_l2norm_rows(kc[:, vcols[hh]]) for hh in range(heads)]
        kks = [_dot_nt(k, k) for k in ks_]
        qks = [_dot_nt(q, k) for q, k in zip(qs_, ks_)]
        beta_c = [_pick_lane(beta_tile, nv * hg + lv) for lv in range(nv)]
        gc_c = [_pick_lane(gc_tile, nv * hg + lv) for lv in range(nv)]
        gc_r = [gct_ref[pl.ds(gct_row0 + lv, 1), :] for lv in range(nv)]
        decay = [jnp.exp(jnp.where(incl, c - r, NEG_BIG)) for c, r in zip(gc_c, gc_r)]
        a_mats = [jnp.where(strict, beta_c[lv] * kks[lv // 2] * decay[lv], 0.0) for lv in range(nv)]
        t_mats = _unit_lower_inverses(a_mats, sample)
        egc = [jnp.exp(c) for c in gc_c]
        uw = [_dot(t_mats[lv], jnp.concatenate([vc[:, vcols[lv]] * beta_c[lv],
                                                ks_[lv // 2] * (beta_c[lv] * egc[lv])], axis=1))
              for lv in range(nv)]
        u = [x[:, :GDN_HEAD] for x in uw]
        w = [x[:, GDN_HEAD:] for x in uw]
        qk_d = [qks[lv // 2] * decay[lv] for lv in range(nv)]
        q_dec = [qs_[lv // 2] * egc[lv] for lv in range(nv)]
        gl_c = [_seq_last_rows(c) if sample else c[CHUNK_T - 1:CHUNK_T, :] for c in gc_c]
        kd_t = [(ks_[lv // 2] * jnp.exp(gl_c[lv] - gc_c[lv])).T for lv in range(nv)]

        if sample:
            wq = [[_dot(jnp.concatenate([w[lv][s * SEQ_PAD:(s + 1) * SEQ_PAD],
                                         q_dec[lv][s * SEQ_PAD:(s + 1) * SEQ_PAD]], axis=0),
                        s0_ref[s, vcols[lv], :]) for s in range(SEQS_PER_CHUNK)] for lv in range(nv)]
            v_new = [u[lv] - jnp.concatenate([x[:SEQ_PAD] for x in wq[lv]], axis=0) for lv in range(nv)]
            o = [jnp.concatenate([x[SEQ_PAD:] for x in wq[lv]], axis=0) + _dot(qk_d[lv], v_new[lv])
                 for lv in range(nv)]
            seq_of_lane = _iota2((GDN_HEAD, CHUNK_T), 1) // SEQ_PAD
            for lv in range(nv):
                v_new_bf = v_new[lv].astype(BF16)
                for s in range(SEQS_PER_CHUNK):
                    dec = jnp.exp(gl_c[lv][s * SEQ_PAD:s * SEQ_PAD + 1, :])
                    upd = _dot(jnp.where(seq_of_lane == s, kd_t[lv], 0.0), v_new_bf)
                    ss_ref[s, vcols[lv], :] = s0_ref[s, vcols[lv], :] * dec + upd
        else:
            s_old = [s_scr[lv] for lv in range(nv)]
            wq = [_dot(jnp.concatenate([w[lv], q_dec[lv]], axis=0), s_old[lv]) for lv in range(nv)]
            v_new = [u[lv] - wq[lv][:CHUNK_T] for lv in range(nv)]
            o = [wq[lv][CHUNK_T:] + _dot(qk_d[lv], v_new[lv]) for lv in range(nv)]
            for lv in range(nv):
                s_new = s_old[lv] * jnp.exp(gl_c[lv]) + _dot(kd_t[lv], v_new[lv])
                s_scr[lv] = s_new
                sp_ref[vcols[lv], :] = s_new

        for lv in range(nv):
            y = _rmsnorm_rows(o[lv], nw) * jax.nn.silu(z_ref[:, vcols[lv]])
            y_ref[:, vcols[lv]] = y.astype(y_ref.dtype)

    @pl.when(is_prompt)
    def _():
        @pl.when(step % chunks_per_seq == 0)
        def _():
            _zero_conv_history((bq, bk, bv))
            s_scr[...] = jnp.zeros(s_scr.shape, F32)
        body(False, q_ref[...], k_ref[...], v_ref[...])

    @pl.when(jnp.logical_not(is_prompt))
    def _():
        real = _real_row_mask((CHUNK_T, 1), 0)
        _zero_conv_history((bq, bk, bv))
        body(True, jnp.where(real, q_ref[...], cq_ref[...]), jnp.where(real, k_ref[...], ck_ref[...]),
             jnp.where(real, v_ref[...], cv_ref[...]))


class _Dims(NamedTuple):
    n_prompt_seqs: int
    chunks_per_seq: int
    n_sample_seqs: int

    @property
    def n_prompt_chunks(self):
        return self.n_prompt_seqs * self.chunks_per_seq

    @property
    def n_steps(self):
        return self.n_prompt_chunks + self.n_sample_seqs // SEQS_PER_CHUNK

    @property
    def n_rows(self):
        return self.n_steps * CHUNK_T


def _call_mixer(body, name, dims, layer, n_groups, in_specs, args, prev, *, y_width, y_block,
                state_rows, state_block, state_cols, n_layers, scratch):
    n_pc, cps = dims.n_prompt_chunks, dims.chunks_per_seq
    in_specs = list(in_specs) + [pl.BlockSpec(memory_space=pl.ANY)] * len(prev)
    n_in = len(args)
    out_specs = [
        pl.BlockSpec((CHUNK_T, y_block), lambda g, t: (t, g)),
        pl.BlockSpec((None, None, state_block, state_cols),
                     lambda g, t: (layer, jnp.minimum(t // cps, dims.n_prompt_seqs - 1), g, 0)),
        pl.BlockSpec((None, SEQS_PER_CHUNK, state_block, state_cols),
                     lambda g, t: (layer, jnp.maximum(t - n_pc, 0), g, 0))]
    out_shape = [jax.ShapeDtypeStruct((dims.n_rows, y_width), BF16),
                 jax.ShapeDtypeStruct((n_layers, dims.n_prompt_seqs, state_rows, state_cols), F32),
                 jax.ShapeDtypeStruct((n_layers, dims.n_sample_seqs, state_rows, state_cols), F32)]
    return pl.pallas_call(
        functools.partial(body, n_prev=len(prev)),
        grid=(n_groups, dims.n_steps),
        in_specs=in_specs,
        out_specs=out_specs,
        out_shape=out_shape,
        input_output_aliases={n_in + i: 1 + i for i in range(len(prev))},
        scratch_shapes=scratch,
        compiler_params=_cparams(("parallel", "arbitrary")),
        name=name,
    )(*args, *prev)


def _gdn_mixer(proj, gates, conv_w, norm_w, cache8, states, layer, prev, dims):
    beta, gc, gct = gates
    hb, heads = GDN_HEAD, GDN_HEADS_PER_STEP
    qw, vw = heads * hb, 2 * heads * hb
    qkb = GDN_QK_W // qw
    vb = (2 * GDN_QK_W) // vw
    zb = GDN_CONV_DIM // vw
    gct_rows = max(SUBLANES, 2 * heads)
    n_pc = dims.n_prompt_chunks
    sample_blk = lambda t: jnp.maximum(t - n_pc, 0)

    def spec(width, col_off, row_fn=lambda t: t):
        return pl.BlockSpec((CHUNK_T, width), lambda g, t: (row_fn(t), col_off + g))

    row128 = pl.BlockSpec((CHUNK_T, LANES), lambda g, t: (t, 0))
    in_specs = [spec(qw, 0), spec(qw, qkb), spec(vw, vb), spec(vw, zb), row128, row128,
                pl.BlockSpec((gct_rows, CHUNK_T), lambda g, t: ((2 * heads * g) // gct_rows, t)),
                pl.BlockSpec((CONV_W, qw), lambda g, t: (0, g)),
                pl.BlockSpec((CONV_W, qw), lambda g, t: (0, qkb + g)),
                pl.BlockSpec((CONV_W, vw), lambda g, t: (0, vb + g)),
                pl.BlockSpec((1, hb), lambda g, t: (0, 0)),
                spec(qw, 0, sample_blk), spec(qw, qkb, sample_blk), spec(vw, vb, sample_blk),
                pl.BlockSpec((None, SEQS_PER_CHUNK, vw, hb), lambda g, t: (layer, sample_blk(t), g, 0))]
    args = [proj, proj, proj, proj, beta, gc, gct, conv_w, conv_w, conv_w, norm_w.reshape(1, hb),
            cache8, cache8, cache8, states]
    return _call_mixer(
        functools.partial(_gdn_kernel, heads=heads, n_prompt_chunks=dims.n_prompt_chunks,
                          chunks_per_seq=dims.chunks_per_seq),
        "gdn_mixer", dims, layer, GDN_QK_HEADS // heads, in_specs, args, prev,
        y_width=GDN_V_W, y_block=vw, state_rows=GDN_V_HEADS * hb, state_block=vw, state_cols=hb,
        n_layers=states.shape[0],
        scratch=[pltpu.VMEM((CHUNK_T + SUBLANES, qw), F32), pltpu.VMEM((CHUNK_T + SUBLANES, qw), F32),
                 pltpu.VMEM((CHUNK_T + SUBLANES, vw), F32), pltpu.VMEM((2 * heads, hb, hb), F32)])


def _ssd_kernel(*refs, n_prompt_chunks, chunks_per_seq, n_prev):
    (z_ref, x_ref, b_ref, c_ref, dt_ref, acs_ref, acst_ref, cwx_ref, cwb_ref, cwc_ref,
     cbx_ref, cbb_ref, cbc_ref, de_ref, nw_ref, cx_ref, cb_ref, cc_ref, s0_ref) = refs[:19]
    y_ref, sp_ref, ss_ref, bx, bb, bc, s_scr = refs[19 + n_prev:]
    grp = pl.program_id(0)
    step = pl.program_id(1)
    is_prompt = step < n_prompt_chunks
    t = CHUNK_T

    def head_scale(col):
        eb = jnp.broadcast_to(jnp.exp(acst_ref[:, col:col + 1]), (SSD_HPG, SSD_STATE))
        return jnp.concatenate([jnp.broadcast_to(eb[r:r + 1, :], (SSD_HEAD, SSD_STATE))
                                for r in range(SSD_HPG)], axis=0)

    def body(sample, xx, xb, xc):
        xg = jax.nn.silu(_causal_conv(bx, xx, cwx_ref) + cbx_ref[...])
        bg = jax.nn.silu(_causal_conv(bb, xb, cwb_ref) + cbb_ref[...])
        cg = jax.nn.silu(_causal_conv(bc, xc, cwc_ref) + cbc_ref[...])

        dt_tile, acs_tile = dt_ref[...], acs_ref[...]
        sel = (_iota2((LANES, SSD_GW), 0) == grp * SSD_HPG + _iota2((LANES, SSD_GW), 1) // SSD_HEAD)
        parts = jnp.concatenate(_split3(dt_tile) + _split3(acs_tile), axis=0)
        ex = jnp.dot(parts, sel.astype(BF16), preferred_element_type=F32)
        dt_e = ex[0:t] + ex[t:2 * t] + ex[2 * t:3 * t]
        acs_e = ex[3 * t:4 * t] + ex[4 * t:5 * t] + ex[5 * t:6 * t]
        acs_last_e = _seq_last_rows(acs_e) if sample else acs_e[t - 1:t, :]

        xdt = xg * dt_e
        incl, _ = _chunk_masks(sample)
        cb = _dot_nt(cg, bg)
        lane_lo = _iota2((t, 2 * SSD_HEAD), 1) < SSD_HEAD
        pairs = []
        for pr in range(SSD_HPG // 2):
            m = []
            for r in (2 * pr, 2 * pr + 1):
                acs_c = _pick_lane(acs_tile, grp * SSD_HPG + r)
                acs_r = acst_ref[r:r + 1, :]
                m.append(cb * jnp.exp(jnp.where(incl, acs_c - acs_r, NEG_BIG)))
            xpair = xdt[:, pr * 2 * SSD_HEAD:(pr + 1) * 2 * SSD_HEAD]
            rhs = jnp.concatenate([jnp.where(lane_lo, xpair, 0.0), jnp.where(lane_lo, 0.0, xpair)],
                                  axis=0)
            pairs.append(_dot(jnp.concatenate(m, axis=1), rhs))
        y_diag = jnp.concatenate(pairs, axis=1)

        xd_t = (xdt * jnp.exp(acs_last_e - acs_e)).T
        if sample:
            offs = []
            zeros = jnp.zeros((SEQ_PAD, SSD_STATE), F32)
            seq_of_lane = _iota2((SSD_GW, t), 1) // SEQ_PAD
            bg_bf = bg.astype(BF16)
            for s in range(SEQS_PER_CHUNK):
                rows = slice(s * SEQ_PAD, (s + 1) * SEQ_PAD)
                s_old = s0_ref[s]
                offs.append(_dot_nt(jnp.concatenate([cg[rows], zeros], axis=0), s_old)[:SEQ_PAD])
                upd = _dot(jnp.where(seq_of_lane == s, xd_t, 0.0), bg_bf)
                ss_ref[s] = s_old * head_scale((s + 1) * SEQ_PAD - 1) + upd
            y_off = jnp.concatenate(offs, axis=0)
        else:
            s_old = s_scr[...]
            y_off = _dot_nt(cg, s_old)
            s_new = s_old * head_scale(t - 1) + _dot(xd_t, bg)
            s_scr[...] = s_new
            sp_ref[...] = s_new

        y = y_diag + y_off * jnp.exp(acs_e) + de_ref[...] * xg
        y = y * jax.nn.silu(z_ref[...])
        y = y * lax.rsqrt(jnp.mean(y * y, axis=-1, keepdims=True) + EPS) * nw_ref[...]
        y_ref[...] = y.astype(y_ref.dtype)

    @pl.when(is_prompt)
    def _():
        @pl.when(step % chunks_per_seq == 0)
        def _():
            _zero_conv_history((bx, bb, bc))
            s_scr[...] = jnp.zeros(s_scr.shape, F32)
        body(False, x_ref[...], b_ref[...], c_ref[...])

    @pl.when(jnp.logical_not(is_prompt))
    def _():
        real = _real_row_mask((CHUNK_T, 1), 0)
        _zero_conv_history((bx, bb, bc))
        body(True, jnp.where(real, x_ref[...], cx_ref[...]), jnp.where(real, b_ref[...], cb_ref[...]),
             jnp.where(real, c_ref[...], cc_ref[...]))


def _ssd_mixer(proj, gates, conv_w, conv_b, d_skip, norm_w, cache8, states, layer, prev, dims):
    dt, acs, acst = gates
    gw, n = SSD_GW, SSD_STATE
    xb = SSD_D_INNER // gw
    bb = (2 * SSD_D_INNER) // n
    cb = bb + SSD_GN // n
    wbb = SSD_D_INNER // n
    wcb = wbb + SSD_GN // n
    n_pc = dims.n_prompt_chunks
    sample_blk = lambda t: jnp.maximum(t - n_pc, 0)

    def spec(width, col_off, row_fn=lambda t: t):
        return pl.BlockSpec((CHUNK_T, width), lambda g, t: (row_fn(t), col_off + g))

    def wspec(rows, width, col_off):
        return pl.BlockSpec((rows, width), lambda g, t: (0, col_off + g))

    row128 = pl.BlockSpec((CHUNK_T, LANES), lambda g, t: (t, 0))
    in_specs = [spec(gw, 0), spec(gw, xb), spec(n, bb), spec(n, cb), row128, row128,
                pl.BlockSpec((SSD_HPG, CHUNK_T), lambda g, t: (g, t)),
                wspec(CONV_W, gw, 0), wspec(CONV_W, n, wbb), wspec(CONV_W, n, wcb),
                wspec(1, gw, 0), wspec(1, n, wbb), wspec(1, n, wcb),
                wspec(1, gw, 0), wspec(1, gw, 0),
                spec(gw, 0, sample_blk), spec(n, wbb, sample_blk), spec(n, wcb, sample_blk),
                pl.BlockSpec((None, SEQS_PER_CHUNK, gw, n), lambda g, t: (layer, sample_blk(t), g, 0))]
    conv_b2 = conv_b.reshape(1, -1)
    d_e = jnp.repeat(d_skip, SSD_HEAD).reshape(1, SSD_D_INNER)
    args = [proj, proj, proj, proj, dt, acs, acst, conv_w, conv_w, conv_w,
            conv_b2, conv_b2, conv_b2, d_e, norm_w.reshape(1, SSD_D_INNER),
            cache8, cache8, cache8, states]
    return _call_mixer(
        functools.partial(_ssd_kernel, n_prompt_chunks=dims.n_prompt_chunks,
                          chunks_per_seq=dims.chunks_per_seq),
        "ssd_mixer", dims, layer, SSD_GROUPS, in_specs, args, prev,
        y_width=SSD_D_INNER, y_block=gw, state_rows=SSD_HEADS * SSD_HEAD, state_block=gw,
        state_cols=n, n_layers=states.shape[0],
        scratch=[pltpu.VMEM((CHUNK_T + SUBLANES, gw), F32), pltpu.VMEM((CHUNK_T + SUBLANES, n), F32),
                 pltpu.VMEM((CHUNK_T + SUBLANES, n), F32), pltpu.VMEM((gw, n), F32)])


def _pad_sequences(x, n_real):
    pads = [(0, 0)] * (x.ndim - 2) + [(SEQ_PAD - n_real, 0), (0, 0)]
    xp = jnp.pad(x, pads)
    return xp.reshape(x.shape[:-3] + (x.shape[-3] * SEQ_PAD, x.shape[-1]))


def _history_rows(cache):
    n_real = SEQ_PAD // 2
    lo = SEQ_PAD - n_real - (CONV_W - 1)
    xp = jnp.pad(cache, ((0, 0), (lo, n_real), (0, 0)))
    return xp.reshape(cache.shape[0] * SEQ_PAD, cache.shape[2])


def kernel(x_prompt, x_sample, state_gdn, cache_gdn_conv, state_ssd, cache_ssd_conv, p_prompt, p_sample,
           norm_w, gdn_w_in, gdn_conv_w, gdn_A_log, gdn_dt_bias, gdn_norm_w, gdn_w_out,
           ssd_w_in, ssd_conv_w, ssd_conv_b, ssd_A_log, ssd_dt_bias, ssd_D, ssd_norm_w, ssd_w_out,
           ple_w_proj, ple_w_gate, ple_norm_w, final_norm_w):
    bp, lp, d = x_prompt.shape
    bs, ls, _ = x_sample.shape
    depth = norm_w.shape[0]
    assert d == D_MODEL and ls == SEQ_PAD // 2 and lp % CHUNK_T == 0 and bs % SEQS_PER_CHUNK == 0
    mp, ms = bp * lp, bs * SEQ_PAD
    assert mp % ROW_TILE == 0 and ms % ROW_TILE == 0

    h = jnp.concatenate([x_prompt.reshape(mp, d), _pad_sequences(x_sample, ls)], axis=0)
    p = jnp.concatenate([p_prompt.reshape(depth, mp, PLE_DIM), _pad_sequences(p_sample, ls)], axis=1)

    def conv_caches(proj, lo, hi):
        c_p = proj[:mp].reshape(bp, lp, -1)[:, lp - (CONV_W - 1):, lo:hi]
        c_s = proj[mp:].reshape(bs, SEQ_PAD, -1)[:, SEQ_PAD - (CONV_W - 1):, lo:hi]
        return c_p, c_s

    dims = _Dims(n_prompt_seqs=bp, chunks_per_seq=lp // CHUNK_T, n_sample_seqs=bs)
    gdn_states = state_gdn.reshape(state_gdn.shape[0], bs, GDN_V_HEADS * GDN_HEAD, GDN_HEAD)
    ssd_states = state_ssd.reshape(state_ssd.shape[0], bs, SSD_HEADS * SSD_HEAD, SSD_STATE)
    caches = {k: [] for k in ("cg_p", "cs_p", "cg_s", "cs_s")}
    gdn_prev, ssd_prev = (), ()
    hn = _norm_rows(h, norm_w[0], BF16)
    for i in range(depth):
        j = i // 2
        if i % 2 == 0:
            proj = _matmul(hn, gdn_w_in, j, GDN_MAIN, "plain")
            gates = _gdn_gates(hn, gdn_w_in[j, :, GDN_MAIN:], gdn_A_log[j], gdn_dt_bias[j], mp)
            y, *gdn_prev = _gdn_mixer(proj, gates, gdn_conv_w[j], gdn_norm_w[j],
                                      _history_rows(cache_gdn_conv[j]), gdn_states, j, gdn_prev, dims)
            c_p, c_s = conv_caches(proj, 0, GDN_CONV_DIM)
            w_out = gdn_w_out
            keys = ("cg_p", "cg_s")
        else:
            proj = _matmul(hn, ssd_w_in, j, SSD_MAIN, "plain")
            gates = _ssd_gates(hn, ssd_w_in[j, :, SSD_MAIN:], ssd_A_log[j], ssd_dt_bias[j], mp)
            y, *ssd_prev = _ssd_mixer(proj, gates, ssd_conv_w[j], ssd_conv_b[j], ssd_D[j], ssd_norm_w[j],
                                      _history_rows(cache_ssd_conv[j]), ssd_states, j, ssd_prev, dims)
            c_p, c_s = conv_caches(proj, SSD_D_INNER, SSD_MAIN)
            w_out = ssd_w_out
            keys = ("cs_p", "cs_s")
        for key, val in zip(keys, (c_p, c_s)):
            caches[key].append(val)
        h_mid, h_mid_bf = _matmul(y, w_out, j, D_MODEL, "residual", extras=(h,))
        e = _ple_embed(p, ple_w_proj, i, ple_norm_w[i])
        h = _matmul(h_mid_bf, ple_w_gate, i, D_MODEL, "ple_gate", extras=(h_mid, e))
        last = i + 1 == depth
        hn = _norm_rows(h, final_norm_w if last else norm_w[i + 1], F32 if last else BF16)

    y_prompt = hn[:mp].reshape(bp, lp, d)
    y_sample = hn[mp:].reshape(bs, SEQ_PAD, d)[:, SEQ_PAD - ls:, :]
    cc = {k: jnp.stack(v) for k, v in caches.items()}
    sg_p, sg_s = (s.reshape(s.shape[:2] + (GDN_V_HEADS, GDN_HEAD, GDN_HEAD)) for s in gdn_prev)
    ss_p, ss_s = (s.reshape(s.shape[:2] + (SSD_HEADS, SSD_HEAD, SSD_STATE)) for s in ssd_prev)
    return (y_prompt, y_sample, sg_p, cc["cg_p"], ss_p, cc["cs_p"], sg_s, cc["cg_s"], ss_s, cc["cs_s"])
```

```python
import functools
from typing import NamedTuple

import jax
import jax.numpy as jnp
from jax import lax
from jax.experimental import pallas as pl
from jax.experimental.pallas import tpu as pltpu

F32 = jnp.float32
BF16 = jnp.bfloat16

EPS = 1e-6
CONV_W = 4
D_MODEL = 2048
PLE_DIM = 256

GDN_HEAD = 128
GDN_QK_HEADS = 16
GDN_V_HEADS = 32
GDN_QK_W = GDN_QK_HEADS * GDN_HEAD
GDN_V_W = GDN_V_HEADS * GDN_HEAD
GDN_CONV_DIM = 2 * GDN_QK_W + GDN_V_W
GDN_MAIN = GDN_CONV_DIM + GDN_V_W

SSD_D_INNER = 4096
SSD_HEAD = 64
SSD_HEADS = 64
SSD_STATE = 128
SSD_GROUPS = 8
SSD_HPG = SSD_HEADS // SSD_GROUPS
SSD_GN = SSD_GROUPS * SSD_STATE
SSD_CONV_DIM = SSD_D_INNER + 2 * SSD_GN
SSD_MAIN = SSD_D_INNER + SSD_CONV_DIM
SSD_GW = SSD_HPG * SSD_HEAD

LANES = 128
SUBLANES = 8
CHUNK_T = 128
SEQ_PAD = 8
SEQS_PER_CHUNK = CHUNK_T // SEQ_PAD
INV_BLOCK = 16
GDN_HEADS_PER_STEP = 4
ROW_TILE = 512
MM_TILES = {"plain": (1024, 2048), "residual": (512, 1024), "ple_gate": (1024, 1024)}
VMEM_LIMIT = 56 * 1024 * 1024
NEG_BIG = -1e30


def _cparams(sem):
    return pltpu.CompilerParams(dimension_semantics=sem, vmem_limit_bytes=VMEM_LIMIT)


def _dot(a, b):
    return jnp.dot(a.astype(BF16), b.astype(BF16), preferred_element_type=F32)


def _dot_nt(a, b):
    return lax.dot_general(a.astype(BF16), b.astype(BF16), (((1,), (1,)), ((), ())),
                           preferred_element_type=F32)


def _split3(x):
    x1 = x.astype(BF16)
    r1 = x - x1.astype(F32)
    x2 = r1.astype(BF16)
    r2 = r1 - x2.astype(F32)
    return [x1, x2, r2.astype(BF16)]


def _softplus(x):
    return jnp.maximum(x, 0.0) + jnp.log1p(jnp.exp(-jnp.abs(x)))


def _iota2(shape, dim):
    return lax.broadcasted_iota(jnp.int32, shape, dim)


def _pick_lane(x, idx):
    lane = _iota2(x.shape, 1)
    return jnp.sum(jnp.where(lane == idx, x, 0.0), axis=1, keepdims=True)


def _chunk_masks(sample):
    ii = _iota2((CHUNK_T, CHUNK_T), 0)
    jj = _iota2((CHUNK_T, CHUNK_T), 1)
    incl = ii >= jj
    strict = ii > jj
    if sample:
        same = (ii // SEQ_PAD) == (jj // SEQ_PAD)
        incl = incl & same
        strict = strict & same
    return incl, strict


def _seq_last_rows(x):
    rows, cols = x.shape
    x3 = x.reshape(rows // SEQ_PAD, SEQ_PAD, cols)
    last = jnp.broadcast_to(x3[:, SEQ_PAD - 1:SEQ_PAD, :], x3.shape)
    return last.reshape(rows, cols)


def _unit_lower_inverses(mats, sample):
    ii = _iota2(mats[0].shape, 0)
    jj = _iota2(mats[0].shape, 1)
    eye = (ii == jj).astype(F32)

    def neumann(xs, squarings):
        ts = [eye - x for x in xs]
        ps = xs
        for _ in range(squarings):
            ps = [_dot(p, p) for p in ps]
            ts = [t + _dot(t, p) for t, p in zip(ts, ps)]
        return ts

    if sample:
        assert SEQ_PAD == 8
        return neumann(mats, 2)
    assert INV_BLOCK == 16 and CHUNK_T // INV_BLOCK == 8
    blk = (ii // INV_BLOCK) == (jj // INV_BLOCK)
    ds = [jnp.where(blk, a, 0.0) for a in mats]
    tds = neumann(ds, 3)
    ns = [_dot(td, a - d) for td, a, d in zip(tds, mats, ds)]
    return [_dot(p, td) for p, td in zip(neumann(ns, 2), tds)]


def _causal_conv(buf_ref, x, w_ref):
    t = x.shape[0]
    buf_ref[SUBLANES:SUBLANES + t, :] = x
    w = w_ref[...]
    y = buf_ref[pl.ds(SUBLANES - 3, t), :] * w[0:1, :]
    for j in range(1, CONV_W):
        y = y + buf_ref[pl.ds(SUBLANES - 3 + j, t), :] * w[j:j + 1, :]
    buf_ref[0:SUBLANES, :] = buf_ref[t:t + SUBLANES, :]
    return y


def _zero_conv_history(bufs):
    for b in bufs:
        b[0:SUBLANES, :] = jnp.zeros((SUBLANES, b.shape[1]), F32)


def _real_row_mask(shape, dim):
    return (_iota2(shape, dim) % SEQ_PAD) >= (SEQ_PAD // 2)


def _rmsnorm_rows(x, w):
    return x * lax.rsqrt(jnp.mean(x * x, axis=-1, keepdims=True) + EPS) * w


def _norm_kernel(x_ref, w_ref, o_ref):
    o_ref[...] = _rmsnorm_rows(x_ref[...], w_ref[...]).astype(o_ref.dtype)


def _norm_rows(x, w, out_dtype):
    m, d = x.shape
    return pl.pallas_call(
        _norm_kernel,
        grid=(m // ROW_TILE,),
        in_specs=[pl.BlockSpec((ROW_TILE, d), lambda i: (i, 0)),
                  pl.BlockSpec((1, d), lambda i: (0, 0))],
        out_specs=pl.BlockSpec((ROW_TILE, d), lambda i: (i, 0)),
        out_shape=jax.ShapeDtypeStruct((m, d), out_dtype),
        compiler_params=_cparams(("parallel",)),
        name="norm_rows",
    )(x, w.reshape(1, d))


def _ple_embed_kernel(p_ref, w_ref, nw_ref, o_ref):
    e = _dot(p_ref[...], w_ref[...])
    o_ref[...] = _rmsnorm_rows(e, nw_ref[...])


def _ple_embed(p, w_proj, layer, norm_w):
    m = p.shape[1]
    return pl.pallas_call(
        _ple_embed_kernel,
        grid=(m // ROW_TILE,),
        in_specs=[pl.BlockSpec((None, ROW_TILE, PLE_DIM), lambda i: (layer, i, 0)),
                  pl.BlockSpec((None, PLE_DIM, D_MODEL), lambda i: (layer, 0, 0)),
                  pl.BlockSpec((1, D_MODEL), lambda i: (0, 0))],
        out_specs=pl.BlockSpec((ROW_TILE, D_MODEL), lambda i: (i, 0)),
        out_shape=jax.ShapeDtypeStruct((m, D_MODEL), F32),
        compiler_params=_cparams(("parallel",)),
        name="ple_embed",
    )(p, w_proj, norm_w.reshape(1, D_MODEL))


def _mm_kernel(x_ref, w_ref, *rest, epilogue):
    wbf_ref = rest[-1]
    refs = rest[:-1]

    @pl.when(pl.program_id(1) == 0)
    def _():
        wbf_ref[...] = w_ref[...].astype(BF16)

    acc = jnp.dot(x_ref[...].astype(BF16), wbf_ref[...], preferred_element_type=F32)
    if epilogue == "plain":
        (o_ref,) = refs
        o_ref[...] = acc
    elif epilogue == "residual":
        res_ref, o_ref, obf_ref = refs
        h = res_ref[...] + acc
        o_ref[...] = h
        obf_ref[...] = h.astype(BF16)
    elif epilogue == "ple_gate":
        res_ref, e_ref, o_ref = refs
        o_ref[...] = res_ref[...] + e_ref[...] * jax.nn.sigmoid(acc)
    else:
        raise ValueError(epilogue)


def _matmul(x, w, layer, n_cols, epilogue, extras=()):
    m, k = x.shape
    tm, tn = MM_TILES[epilogue]
    assert m % tm == 0 and n_cols % tn == 0
    tile = pl.BlockSpec((tm, tn), lambda j, i: (i, j))
    w_spec = pl.BlockSpec((None, k, tn), lambda j, i: (layer, 0, j), pipeline_mode=pl.Buffered(1))
    in_specs = [pl.BlockSpec((tm, k), lambda j, i: (i, 0)), w_spec] + [tile] * len(extras)
    out_shape = [jax.ShapeDtypeStruct((m, n_cols), F32)]
    if epilogue == "residual":
        out_shape.append(jax.ShapeDtypeStruct((m, n_cols), BF16))
    out = pl.pallas_call(
        functools.partial(_mm_kernel, epilogue=epilogue),
        grid=(n_cols // tn, m // tm),
        in_specs=in_specs,
        out_specs=[tile] * len(out_shape),
        out_shape=out_shape,
        scratch_shapes=[pltpu.VMEM((k, tn), BF16)],
        compiler_params=_cparams(("parallel", "arbitrary")),
        name="matmul_" + epilogue,
    )(x, w, *extras)
    return out if epilogue == "residual" else out[0]


def _cumsum_operands(n_prompt_tiles):
    is_prompt = pl.program_id(0) < n_prompt_tiles
    shift = jnp.where(is_prompt, CHUNK_T.bit_length() - 1, SEQ_PAD.bit_length() - 1)
    ii = _iota2((ROW_TILE, ROW_TILE), 0)
    jj = _iota2((ROW_TILE, ROW_TILE), 1)
    same = lax.shift_right_logical(ii, shift) == lax.shift_right_logical(jj, shift)
    lower = (same & (jj <= ii)).astype(F32)
    upper = (same & (ii <= jj)).astype(F32)
    return is_prompt, lower, upper


def _dot_f32(a, b):
    return jnp.dot(a, b, preferred_element_type=F32, precision=lax.Precision.HIGHEST)


def _gdn_gate_kernel(x_ref, wa_ref, wb_ref, alog_ref, dtb_ref, beta_ref, gc_ref, gct_ref, *,
                     n_prompt_tiles):
    x = x_ref[...]
    is_prompt, lower, upper = _cumsum_operands(n_prompt_tiles)
    live_rows = is_prompt | _real_row_mask((ROW_TILE, LANES), 0)
    a = _dot(x, wa_ref[...])
    g = jnp.where(live_rows, -jnp.exp(alog_ref[...]) * _softplus(a + dtb_ref[...]), 0.0)
    beta_ref[...] = jnp.where(live_rows, jax.nn.sigmoid(_dot(x, wb_ref[...])), 0.0)
    gc_ref[...] = _dot_f32(lower, g)
    gct_ref[...] = _dot_f32(g.T[:GDN_V_HEADS], upper)


def _pad_lanes(x):
    return jnp.pad(x, ((0, 0), (0, LANES - x.shape[1])))


def _gdn_gates(hn, w_ab, a_log, dt_bias, n_prompt_rows):
    m = hn.shape[0]
    wa = w_ab[:, :GDN_V_HEADS]
    wb = w_ab[:, GDN_V_HEADS:]
    row = pl.BlockSpec((ROW_TILE, LANES), lambda i: (i, 0))
    full = lambda shape: pl.BlockSpec(shape, lambda i: (0, 0))
    return pl.pallas_call(
        functools.partial(_gdn_gate_kernel, n_prompt_tiles=n_prompt_rows // ROW_TILE),
        grid=(m // ROW_TILE,),
        in_specs=[pl.BlockSpec((ROW_TILE, D_MODEL), lambda i: (i, 0)),
                  full((D_MODEL, LANES)), full((D_MODEL, LANES)), full((1, LANES)), full((1, LANES))],
        out_specs=[row, row, pl.BlockSpec((GDN_V_HEADS, ROW_TILE), lambda i: (0, i))],
        out_shape=[jax.ShapeDtypeStruct((m, LANES), F32)] * 2
        + [jax.ShapeDtypeStruct((GDN_V_HEADS, m), F32)],
        compiler_params=_cparams(("parallel",)),
        name="gdn_gates",
    )(hn, _pad_lanes(wa), _pad_lanes(wb),
      _pad_lanes(a_log.reshape(1, -1)), _pad_lanes(dt_bias.reshape(1, -1)))


def _ssd_gate_kernel(x_ref, w_ref, alog_ref, dtb_ref, dt_ref, acs_ref, acst_ref, *, n_prompt_tiles):
    x = x_ref[...]
    is_prompt, lower, upper = _cumsum_operands(n_prompt_tiles)
    live_rows = is_prompt | _real_row_mask((ROW_TILE, LANES), 0)
    dt = jnp.where(live_rows, _softplus(_dot(x, w_ref[...]) + dtb_ref[...]), 0.0)
    a = dt * -jnp.exp(alog_ref[...])
    dt_ref[...] = dt
    acs_ref[...] = _dot_f32(lower, a)
    acst_ref[...] = _dot_f32(a.T[:SSD_HEADS], upper)


def _ssd_gates(hn, w, a_log, dt_bias, n_prompt_rows):
    m = hn.shape[0]
    row = pl.BlockSpec((ROW_TILE, LANES), lambda i: (i, 0))
    full = lambda shape: pl.BlockSpec(shape, lambda i: (0, 0))
    return pl.pallas_call(
        functools.partial(_ssd_gate_kernel, n_prompt_tiles=n_prompt_rows // ROW_TILE),
        grid=(m // ROW_TILE,),
        in_specs=[pl.BlockSpec((ROW_TILE, D_MODEL), lambda i: (i, 0)),
                  full((D_MODEL, LANES)), full((1, LANES)), full((1, LANES))],
        out_specs=[row, row, pl.BlockSpec((SSD_HEADS, ROW_TILE), lambda i: (0, i))],
        out_shape=[jax.ShapeDtypeStruct((m, LANES), F32)] * 2
        + [jax.ShapeDtypeStruct((SSD_HEADS, m), F32)],
        compiler_params=_cparams(("parallel",)),
        name="ssd_gates",
    )(hn, _pad_lanes(w), _pad_lanes(a_log.reshape(1, -1)), _pad_lanes(dt_bias.reshape(1, -1)))


def _l2norm_rows(x):
    return x * lax.rsqrt(jnp.sum(x * x, axis=-1, keepdims=True) + EPS)


def _gdn_kernel(*refs, heads, n_prompt_chunks, chunks_per_seq, n_prev):
    (q_ref, k_ref, v_ref, z_ref, beta_ref, gc_ref, gct_ref, cwq_ref, cwk_ref, cwv_ref, nw_ref,
     cq_ref, ck_ref, cv_ref, s0_ref) = refs[:15]
    y_ref, sp_ref, ss_ref, bq, bk, bv, s_scr = refs[15 + n_prev:]
    hg = pl.program_id(0)
    step = pl.program_id(1)
    is_prompt = step < n_prompt_chunks
    gct_row0 = (2 * heads * hg) % gct_ref.shape[0]

    def body(sample, xq, xk, xv):
        qc = jax.nn.silu(_causal_conv(bq, xq, cwq_ref))
        kc = jax.nn.silu(_causal_conv(bk, xk, cwk_ref))
        vc = jax.nn.silu(_causal_conv(bv, xv, cwv_ref))
        incl, strict = _chunk_masks(sample)
        beta_tile, gc_tile = beta_ref[...], gc_ref[...]
        nw = nw_ref[...]
        nv = 2 * heads
        vcols = [slice(lv * GDN_HEAD, (lv + 1) * GDN_HEAD) for lv in range(nv)]
        qs_ = [_l2norm_rows(qc[:, vcols[hh]]) * (GDN_HEAD ** -0.5) for hh in range(heads)]
        ks_ = [_l2norm_rows(kc[:, vcols[hh]]) for hh in range(heads)]
        kks = [_dot_nt(k, k) for k in ks_]
        qks = [_dot_nt(q, k) for q, k in zip(qs_, ks_)]
        beta_c = [_pick_lane(beta_tile, nv * hg + lv) for lv in range(nv)]
        gc_c = [_pick_lane(gc_tile, nv * hg + lv) for lv in range(nv)]
        gc_r = [gct_ref[pl.ds(gct_row0 + lv, 1), :] for lv in range(nv)]
        decay = [jnp.exp(jnp.where(incl, c - r, NEG_BIG)) for c, r in zip(gc_c, gc_r)]
        a_mats = [jnp.where(strict, beta_c[lv] * kks[lv // 2] * decay[lv], 0.0) for lv in range(nv)]
        t_mats = _unit_lower_inverses(a_mats, sample)
        egc = [jnp.exp(c) for c in gc_c]
        uw = [_dot(t_mats[lv], jnp.concatenate([vc[:, vcols[lv]] * beta_c[lv],
                                                ks_[lv // 2] * (beta_c[lv] * egc[lv])], axis=1))
              for lv in range(nv)]
        u = [x[:, :GDN_HEAD] for x in uw]
        w = [x[:, GDN_HEAD:] for x in uw]
        qk_d = [qks[lv // 2] * decay[lv] for lv in range(nv)]
        q_dec = [qs_[lv // 2] * egc[lv] for lv in range(nv)]
        gl_c = [_seq_last_rows(c) if sample else c[CHUNK_T - 1:CHUNK_T, :] for c in gc_c]
        kd_t = [(ks_[lv // 2] * jnp.exp(gl_c[lv] - gc_c[lv])).T for lv in range(nv)]

        if sample:
            wq = [[_dot(jnp.concatenate([w[lv][s * SEQ_PAD:(s + 1) * SEQ_PAD],
                                         q_dec[lv][s * SEQ_PAD:(s + 1) * SEQ_PAD]], axis=0),
                        s0_ref[s, vcols[lv], :]) for s in range(SEQS_PER_CHUNK)] for lv in range(nv)]
            v_new = [u[lv] - jnp.concatenate([x[:SEQ_PAD] for x in wq[lv]], axis=0) for lv in range(nv)]
            o = [jnp.concatenate([x[SEQ_PAD:] for x in wq[lv]], axis=0) + _dot(qk_d[lv], v_new[lv])
                 for lv in range(nv)]
            seq_of_lane = _iota2((GDN_HEAD, CHUNK_T), 1) // SEQ_PAD
            for lv in range(nv):
                v_new_bf = v_new[lv].astype(BF16)
                for s in range(SEQS_PER_CHUNK):
                    dec = jnp.exp(gl_c[lv][s * SEQ_PAD:s * SEQ_PAD + 1, :])
                    upd = _dot(jnp.where(seq_of_lane == s, kd_t[lv], 0.0), v_new_bf)
                    ss_ref[s, vcols[lv], :] = s0_ref[s, vcols[lv], :] * dec + upd
        else:
            s_old = [s_scr[lv] for lv in range(nv)]
            wq = [_dot(jnp.concatenate([w[lv], q_dec[lv]], axis=0), s_old[lv]) for lv in range(nv)]
            v_new = [u[lv] - wq[lv][:CHUNK_T] for lv in range(nv)]
            o = [wq[lv][CHUNK_T:] + _dot(qk_d[lv], v_new[lv]) for lv in range(nv)]
            for lv in range(nv):
                s_new = s_old[lv] * jnp.exp(gl_c[lv]) + _dot(kd_t[lv], v_new[lv])
                s_scr[lv] = s_new
                sp_ref[vcols[lv], :] = s_new

        for lv in range(nv):
            y = _rmsnorm_rows(o[lv], nw) * jax.nn.silu(z_ref[:, vcols[lv]])
            y_ref[:, vcols[lv]] = y.astype(y_ref.dtype)

    @pl.when(is_prompt)
    def _():
        @pl.when(step % chunks_per_seq == 0)
        def _():
            _zero_conv_history((bq, bk, bv))
            s_scr[...] = jnp.zeros(s_scr.shape, F32)
        body(False, q_ref[...], k_ref[...], v_ref[...])

    @pl.when(jnp.logical_not(is_prompt))
    def _():
        real = _real_row_mask((CHUNK_T, 1), 0)
        _zero_conv_history((bq, bk, bv))
        body(True, jnp.where(real, q_ref[...], cq_ref[...]), jnp.where(real, k_ref[...], ck_ref[...]),
             jnp.where(real, v_ref[...], cv_ref[...]))


class _Dims(NamedTuple):
    n_prompt_seqs: int
    chunks_per_seq: int
    n_sample_seqs: int

    @property
    def n_prompt_chunks(self):
        return self.n_prompt_seqs * self.chunks_per_seq

    @property
    def n_steps(self):
        return self.n_prompt_chunks + self.n_sample_seqs // SEQS_PER_CHUNK

    @property
    def n_rows(self):
        return self.n_steps * CHUNK_T


def _call_mixer(body, name, dims, layer, n_groups, in_specs, args, prev, *, y_width, y_block,
                state_rows, state_block, state_cols, n_layers, scratch):
    n_pc, cps = dims.n_prompt_chunks, dims.chunks_per_seq
    in_specs = list(in_specs) + [pl.BlockSpec(memory_space=pl.ANY)] * len(prev)
    n_in = len(args)
    out_specs = [
        pl.BlockSpec((CHUNK_T, y_block), lambda g, t: (t, g)),
        pl.BlockSpec((None, None, state_block, state_cols),
                     lambda g, t: (layer, jnp.minimum(t // cps, dims.n_prompt_seqs - 1), g, 0)),
        pl.BlockSpec((None, SEQS_PER_CHUNK, state_block, state_cols),
                     lambda g, t: (layer, jnp.maximum(t - n_pc, 0), g, 0))]
    out_shape = [jax.ShapeDtypeStruct((dims.n_rows, y_width), BF16),
                 jax.ShapeDtypeStruct((n_layers, dims.n_prompt_seqs, state_rows, state_cols), F32),
                 jax.ShapeDtypeStruct((n_layers, dims.n_sample_seqs, state_rows, state_cols), F32)]
    return pl.pallas_call(
        functools.partial(body, n_prev=len(prev)),
        grid=(n_groups, dims.n_steps),
        in_specs=in_specs,
        out_specs=out_specs,
        out_shape=out_shape,
        input_output_aliases={n_in + i: 1 + i for i in range(len(prev))},
        scratch_shapes=scratch,
        compiler_params=_cparams(("parallel", "arbitrary")),
        name=name,
    )(*args, *prev)


def _gdn_mixer(proj, gates, conv_w, norm_w, cache8, states, layer, prev, dims):
    beta, gc, gct = gates
    hb, heads = GDN_HEAD, GDN_HEADS_PER_STEP
    qw, vw = heads * hb, 2 * heads * hb
    qkb = GDN_QK_W // qw
    vb = (2 * GDN_QK_W) // vw
    zb = GDN_CONV_DIM // vw
    gct_rows = max(SUBLANES, 2 * heads)
    n_pc = dims.n_prompt_chunks
    sample_blk = lambda t: jnp.maximum(t - n_pc, 0)

    def spec(width, col_off, row_fn=lambda t: t):
        return pl.BlockSpec((CHUNK_T, width), lambda g, t: (row_fn(t), col_off + g))

    row128 = pl.BlockSpec((CHUNK_T, LANES), lambda g, t: (t, 0))
    in_specs = [spec(qw, 0), spec(qw, qkb), spec(vw, vb), spec(vw, zb), row128, row128,
                pl.BlockSpec((gct_rows, CHUNK_T), lambda g, t: ((2 * heads * g) // gct_rows, t)),
                pl.BlockSpec((CONV_W, qw), lambda g, t: (0, g)),
                pl.BlockSpec((CONV_W, qw), lambda g, t: (0, qkb + g)),
                pl.BlockSpec((CONV_W, vw), lambda g, t: (0, vb + g)),
                pl.BlockSpec((1, hb), lambda g, t: (0, 0)),
                spec(qw, 0, sample_blk), spec(qw, qkb, sample_blk), spec(vw, vb, sample_blk),
                pl.BlockSpec((None, SEQS_PER_CHUNK, vw, hb), lambda g, t: (layer, sample_blk(t), g, 0))]
    args = [proj, proj, proj, proj, beta, gc, gct, conv_w, conv_w, conv_w, norm_w.reshape(1, hb),
            cache8, cache8, cache8, states]
    return _call_mixer(
        functools.partial(_gdn_kernel, heads=heads, n_prompt_chunks=dims.n_prompt_chunks,
                          chunks_per_seq=dims.chunks_per_seq),
        "gdn_mixer", dims, layer, GDN_QK_HEADS // heads, in_specs, args, prev,
        y_width=GDN_V_W, y_block=vw, state_rows=GDN_V_HEADS * hb, state_block=vw, state_cols=hb,
        n_layers=states.shape[0],
        scratch=[pltpu.VMEM((CHUNK_T + SUBLANES, qw), F32), pltpu.VMEM((CHUNK_T + SUBLANES, qw), F32),
                 pltpu.VMEM((CHUNK_T + SUBLANES, vw), F32), pltpu.VMEM((2 * heads, hb, hb), F32)])


def _ssd_kernel(*refs, n_prompt_chunks, chunks_per_seq, n_prev):
    (z_ref, x_ref, b_ref, c_ref, dt_ref, acs_ref, acst_ref, cwx_ref, cwb_ref, cwc_ref,
     cbx_ref, cbb_ref, cbc_ref, de_ref, nw_ref, cx_ref, cb_ref, cc_ref, s0_ref) = refs[:19]
    y_ref, sp_ref, ss_ref, bx, bb, bc, s_scr = refs[19 + n_prev:]
    grp = pl.program_id(0)
    step = pl.program_id(1)
    is_prompt = step < n_prompt_chunks
    t = CHUNK_T

    def head_scale(col):
        eb = jnp.broadcast_to(jnp.exp(acst_ref[:, col:col + 1]), (SSD_HPG, SSD_STATE))
        return jnp.concatenate([jnp.broadcast_to(eb[r:r + 1, :], (SSD_HEAD, SSD_STATE))
                                for r in range(SSD_HPG)], axis=0)

    def body(sample, xx, xb, xc):
        xg = jax.nn.silu(_causal_conv(bx, xx, cwx_ref) + cbx_ref[...])
        bg = jax.nn.silu(_causal_conv(bb, xb, cwb_ref) + cbb_ref[...])
        cg = jax.nn.silu(_causal_conv(bc, xc, cwc_ref) + cbc_ref[...])

        dt_tile, acs_tile = dt_ref[...], acs_ref[...]
        sel = (_iota2((LANES, SSD_GW), 0) == grp * SSD_HPG + _iota2((LANES, SSD_GW), 1) // SSD_HEAD)
        parts = jnp.concatenate(_split3(dt_tile) + _split3(acs_tile), axis=0)
        ex = jnp.dot(parts, sel.astype(BF16), preferred_element_type=F32)
        dt_e = ex[0:t] + ex[t:2 * t] + ex[2 * t:3 * t]
        acs_e = ex[3 * t:4 * t] + ex[4 * t:5 * t] + ex[5 * t:6 * t]
        acs_last_e = _seq_last_rows(acs_e) if sample else acs_e[t - 1:t, :]

        xdt = xg * dt_e
        incl, _ = _chunk_masks(sample)
        cb = _dot_nt(cg, bg)
        lane_lo = _iota2((t, 2 * SSD_HEAD), 1) < SSD_HEAD
        pairs = []
        for pr in range(SSD_HPG // 2):
            m = []
            for r in (2 * pr, 2 * pr + 1):
                acs_c = _pick_lane(acs_tile, grp * SSD_HPG + r)
                acs_r = acst_ref[r:r + 1, :]
                m.append(cb * jnp.exp(jnp.where(incl, acs_c - acs_r, NEG_BIG)))
            xpair = xdt[:, pr * 2 * SSD_HEAD:(pr + 1) * 2 * SSD_HEAD]
            rhs = jnp.concatenate([jnp.where(lane_lo, xpair, 0.0), jnp.where(lane_lo, 0.0, xpair)],
                                  axis=0)
            pairs.append(_dot(jnp.concatenate(m, axis=1), rhs))
        y_diag = jnp.concatenate(pairs, axis=1)

        xd_t = (xdt * jnp.exp(acs_last_e - acs_e)).T
        if sample:
            offs = []
            zeros = jnp.zeros((SEQ_PAD, SSD_STATE), F32)
            seq_of_lane = _iota2((SSD_GW, t), 1) // SEQ_PAD
            bg_bf = bg.astype(BF16)
            for s in range(SEQS_PER_CHUNK):
                rows = slice(s * SEQ_PAD, (s + 1) * SEQ_PAD)
                s_old = s0_ref[s]
                offs.append(_dot_nt(jnp.concatenate([cg[rows], zeros], axis=0), s_old)[:SEQ_PAD])
                upd = _dot(jnp.where(seq_of_lane == s, xd_t, 0.0), bg_bf)
                ss_ref[s] = s_old * head_scale((s + 1) * SEQ_PAD - 1) + upd
            y_off = jnp.concatenate(offs, axis=0)
        else:
            s_old = s_scr[...]
            y_off = _dot_nt(cg, s_old)
            s_new = s_old * head_scale(t - 1) + _dot(xd_t, bg)
            s_scr[...] = s_new
            sp_ref[...] = s_new

        y = y_diag + y_off * jnp.exp(acs_e) + de_ref[...] * xg
        y = y * jax.nn.silu(z_ref[...])
        y = y * lax.rsqrt(jnp.mean(y * y, axis=-1, keepdims=True) + EPS) * nw_ref[...]
        y_ref[...] = y.astype(y_ref.dtype)

    @pl.when(is_prompt)
    def _():
        @pl.when(step % chunks_per_seq == 0)
        def _():
            _zero_conv_history((bx, bb, bc))
            s_scr[...] = jnp.zeros(s_scr.shape, F32)
        body(False, x_ref[...], b_ref[...], c_ref[...])

    @pl.when(jnp.logical_not(is_prompt))
    def _():
        real = _real_row_mask((CHUNK_T, 1), 0)
        _zero_conv_history((bx, bb, bc))
        body(True, jnp.where(real, x_ref[...], cx_ref[...]), jnp.where(real, b_ref[...], cb_ref[...]),
             jnp.where(real, c_ref[...], cc_ref[...]))


def _ssd_mixer(proj, gates, conv_w, conv_b, d_skip, norm_w, cache8, states, layer, prev, dims):
    dt, acs, acst = gates
    gw, n = SSD_GW, SSD_STATE
    xb = SSD_D_INNER // gw
    bb = (2 * SSD_D_INNER) // n
    cb = bb + SSD_GN // n
    wbb = SSD_D_INNER // n
    wcb = wbb + SSD_GN // n
    n_pc = dims.n_prompt_chunks
    sample_blk = lambda t: jnp.maximum(t - n_pc, 0)

    def spec(width, col_off, row_fn=lambda t: t):
        return pl.BlockSpec((CHUNK_T, width), lambda g, t: (row_fn(t), col_off + g))

    def wspec(rows, width, col_off):
        return pl.BlockSpec((rows, width), lambda g, t: (0, col_off + g))

    row128 = pl.BlockSpec((CHUNK_T, LANES), lambda g, t: (t, 0))
    in_specs = [spec(gw, 0), spec(gw, xb), spec(n, bb), spec(n, cb), row128, row128,
                pl.BlockSpec((SSD_HPG, CHUNK_T), lambda g, t: (g, t)),
                wspec(CONV_W, gw, 0), wspec(CONV_W, n, wbb), wspec(CONV_W, n, wcb),
                wspec(1, gw, 0), wspec(1, n, wbb), wspec(1, n, wcb),
                wspec(1, gw, 0), wspec(1, gw, 0),
                spec(gw, 0, sample_blk), spec(n, wbb, sample_blk), spec(n, wcb, sample_blk),
                pl.BlockSpec((None, SEQS_PER_CHUNK, gw, n), lambda g, t: (layer, sample_blk(t), g, 0))]
    conv_b2 = conv_b.reshape(1, -1)
    d_e = jnp.repeat(d_skip, SSD_HEAD).reshape(1, SSD_D_INNER)
    args = [proj, proj, proj, proj, dt, acs, acst, conv_w, conv_w, conv_w,
            conv_b2, conv_b2, conv_b2, d_e, norm_w.reshape(1, SSD_D_INNER),
            cache8, cache8, cache8, states]
    return _call_mixer(
        functools.partial(_ssd_kernel, n_prompt_chunks=dims.n_prompt_chunks,
                          chunks_per_seq=dims.chunks_per_seq),
        "ssd_mixer", dims, layer, SSD_GROUPS, in_specs, args, prev,
        y_width=SSD_D_INNER, y_block=gw, state_rows=SSD_HEADS * SSD_HEAD, state_block=gw,
        state_cols=n, n_layers=states.shape[0],
        scratch=[pltpu.VMEM((CHUNK_T + SUBLANES, gw), F32), pltpu.VMEM((CHUNK_T + SUBLANES, n), F32),
                 pltpu.VMEM((CHUNK_T + SUBLANES, n), F32), pltpu.VMEM((gw, n), F32)])


def _pad_sequences(x, n_real):
    pads = [(0, 0)] * (x.ndim - 2) + [(SEQ_PAD - n_real, 0), (0, 0)]
    xp = jnp.pad(x, pads)
    return xp.reshape(x.shape[:-3] + (x.shape[-3] * SEQ_PAD, x.shape[-1]))


def _history_rows(cache):
    n_real = SEQ_PAD // 2
    lo = SEQ_PAD - n_real - (CONV_W - 1)
    xp = jnp.pad(cache, ((0, 0), (lo, n_real), (0, 0)))
    return xp.reshape(cache.shape[0] * SEQ_PAD, cache.shape[2])


def kernel(x_prompt, x_sample, state_gdn, cache_gdn_conv, state_ssd, cache_ssd_conv, p_prompt, p_sample,
           norm_w, gdn_w_in, gdn_conv_w, gdn_A_log, gdn_dt_bias, gdn_norm_w, gdn_w_out,
           ssd_w_in, ssd_conv_w, ssd_conv_b, ssd_A_log, ssd_dt_bias, ssd_D, ssd_norm_w, ssd_w_out,
           ple_w_proj, ple_w_gate, ple_norm_w, final_norm_w):
    bp, lp, d = x_prompt.shape
    bs, ls, _ = x_sample.shape
    depth = norm_w.shape[0]
    assert d == D_MODEL and ls == SEQ_PAD // 2 and lp % CHUNK_T == 0 and bs % SEQS_PER_CHUNK == 0
    mp, ms = bp * lp, bs * SEQ_PAD
    assert mp % ROW_TILE == 0 and ms % ROW_TILE == 0

    h = jnp.concatenate([x_prompt.reshape(mp, d), _pad_sequences(x_sample, ls)], axis=0)
    p = jnp.concatenate([p_prompt.reshape(depth, mp, PLE_DIM), _pad_sequences(p_sample, ls)], axis=1)

    def conv_caches(proj, lo, hi):
        c_p = jnp.stack([proj[(b + 1) * lp - (CONV_W - 1):(b + 1) * lp, lo:hi] for b in range(bp)])
        c_s = proj[mp:, lo:hi].reshape(bs, SEQ_PAD, hi - lo)[:, SEQ_PAD - (CONV_W - 1):, :]
        return c_p, c_s

    dims = _Dims(n_prompt_seqs=bp, chunks_per_seq=lp // CHUNK_T, n_sample_seqs=bs)
    gdn_states = state_gdn.reshape(state_gdn.shape[0], bs, GDN_V_HEADS * GDN_HEAD, GDN_HEAD)
    ssd_states = state_ssd.reshape(state_ssd.shape[0], bs, SSD_HEADS * SSD_HEAD, SSD_STATE)
    caches = {k: [] for k in ("cg_p", "cs_p", "cg_s", "cs_s")}
    gdn_prev, ssd_prev = (), ()
    hn = _norm_rows(h, norm_w[0], BF16)
    for i in range(depth):
        j = i // 2
        if i % 2 == 0:
            proj = _matmul(hn, gdn_w_in, j, GDN_MAIN, "plain")
            gates = _gdn_gates(hn, gdn_w_in[j, :, GDN_MAIN:], gdn_A_log[j], gdn_dt_bias[j], mp)
            y, *gdn_prev = _gdn_mixer(proj, gates, gdn_conv_w[j], gdn_norm_w[j],
                                      _history_rows(cache_gdn_conv[j]), gdn_states, j, gdn_prev, dims)
            c_p, c_s = conv_caches(proj, 0, GDN_CONV_DIM)
            w_out = gdn_w_out
            keys = ("cg_p", "cg_s")
        else:
            proj = _matmul(hn, ssd_w_in, j, SSD_MAIN, "plain")
            gates = _ssd_gates(hn, ssd_w_in[j, :, SSD_MAIN:], ssd_A_log[j], ssd_dt_bias[j], mp)
            y, *ssd_prev = _ssd_mixer(proj, gates, ssd_conv_w[j], ssd_conv_b[j], ssd_D[j], ssd_norm_w[j],
                                      _history_rows(cache_ssd_conv[j]), ssd_states, j, ssd_prev, dims)
            c_p, c_s = conv_caches(proj, SSD_D_INNER, SSD_MAIN)
            w_out = ssd_w_out
            keys = ("cs_p", "cs_s")
        for key, val in zip(keys, (c_p, c_s)):
            caches[key].append(val)
        h_mid, h_mid_bf = _matmul(y, w_out, j, D_MODEL, "residual", extras=(h,))
        e = _ple_embed(p, ple_w_proj, i, ple_norm_w[i])
        h = _matmul(h_mid_bf, ple_w_gate, i, D_MODEL, "ple_gate", extras=(h_mid, e))
        last = i + 1 == depth
        hn = _norm_rows(h, final_norm_w if last else norm_w[i + 1], F32 if last else BF16)

    y_prompt = hn[:mp].reshape(bp, lp, d)
    y_sample = hn[mp:].reshape(bs, SEQ_PAD, d)[:, SEQ_PAD - ls:, :]
    cc = {k: jnp.stack(v) for k, v in caches.items()}
    sg_p, sg_s = (s.reshape(s.shape[:2] + (GDN_V_HEADS, GDN_HEAD, GDN_HEAD)) for s in gdn_prev)
    ss_p, ss_s = (s.reshape(s.shape[:2] + (SSD_HEADS, SSD_HEAD, SSD_STATE)) for s in ssd_prev)
    return (y_prompt, y_sample, sg_p, cc["cg_p"], ss_p, cc["cs_p"], sg_s, cc["cg_s"], ss_s, cc["cs_s"])
```

```python
import functools
from typing import NamedTuple

import jax
import jax.numpy as jnp
from jax import lax
from jax.experimental import pallas as pl
from jax.experimental.pallas import tpu as pltpu

F32 = jnp.float32
BF16 = jnp.bfloat16

EPS = 1e-6
CONV_W = 4
D_MODEL = 2048
PLE_DIM = 256

GDN_HEAD = 128
GDN_QK_HEADS = 16
GDN_V_HEADS = 32
GDN_QK_W = GDN_QK_HEADS * GDN_HEAD
GDN_V_W = GDN_V_HEADS * GDN_HEAD
GDN_CONV_DIM = 2 * GDN_QK_W + GDN_V_W
GDN_MAIN = GDN_CONV_DIM + GDN_V_W

SSD_D_INNER = 4096
SSD_HEAD = 64
SSD_HEADS = 64
SSD_STATE = 128
SSD_GROUPS = 8
SSD_HPG = SSD_HEADS // SSD_GROUPS
SSD_GN = SSD_GROUPS * SSD_STATE
SSD_CONV_DIM = SSD_D_INNER + 2 * SSD_GN
SSD_MAIN = SSD_D_INNER + SSD_CONV_DIM
SSD_GW = SSD_HPG * SSD_HEAD

LANES = 128
SUBLANES = 8
CHUNK_T = 128
SEQ_PAD = 8
SEQS_PER_CHUNK = CHUNK_T // SEQ_PAD
INV_BLOCK = 16
GDN_HEADS_PER_STEP = 4
SSD_GROUPS_PER_STEP = 2
ROW_TILE = 512
MM_TILES = {"plain": (1024, 2048), "residual": (512, 1024), "ple_gate": (1024, 1024)}
VMEM_LIMIT = 56 * 1024 * 1024
NEG_BIG = -1e30


def _cparams(sem):
    return pltpu.CompilerParams(dimension_semantics=sem, vmem_limit_bytes=VMEM_LIMIT)


def _dot(a, b):
    return jnp.dot(a.astype(BF16), b.astype(BF16), preferred_element_type=F32)


def _dot_nt(a, b):
    return lax.dot_general(a.astype(BF16), b.astype(BF16), (((1,), (1,)), ((), ())),
                           preferred_element_type=F32)


def _split3(x):
    x1 = x.astype(BF16)
    r1 = x - x1.astype(F32)
    x2 = r1.astype(BF16)
    r2 = r1 - x2.astype(F32)
    return [x1, x2, r2.astype(BF16)]


def _softplus(x):
    return jnp.maximum(x, 0.0) + jnp.log1p(jnp.exp(-jnp.abs(x)))


def _iota2(shape, dim):
    return lax.broadcasted_iota(jnp.int32, shape, dim)


def _pick_lane(x, idx):
    lane = _iota2(x.shape, 1)
    return jnp.sum(jnp.where(lane == idx, x, 0.0), axis=1, keepdims=True)


def _chunk_masks(sample):
    ii = _iota2((CHUNK_T, CHUNK_T), 0)
    jj = _iota2((CHUNK_T, CHUNK_T), 1)
    incl = ii >= jj
    strict = ii > jj
    if sample:
        same = (ii // SEQ_PAD) == (jj // SEQ_PAD)
        incl = incl & same
        strict = strict & same
    return incl, strict


def _seq_last_rows(x):
    rows, cols = x.shape
    x3 = x.reshape(rows // SEQ_PAD, SEQ_PAD, cols)
    last = jnp.broadcast_to(x3[:, SEQ_PAD - 1:SEQ_PAD, :], x3.shape)
    return last.reshape(rows, cols)


def _unit_lower_inverses(mats, sample):
    ii = _iota2(mats[0].shape, 0)
    jj = _iota2(mats[0].shape, 1)
    eye = (ii == jj).astype(F32)

    def neumann(xs, squarings):
        ts = [eye - x for x in xs]
        ps = xs
        for _ in range(squarings):
            ps = [_dot(p, p) for p in ps]
            ts = [t + _dot(t, p) for t, p in zip(ts, ps)]
        return ts

    if sample:
        assert SEQ_PAD == 8
        return neumann(mats, 2)
    assert INV_BLOCK == 16 and CHUNK_T // INV_BLOCK == 8
    blk = (ii // INV_BLOCK) == (jj // INV_BLOCK)
    ds = [jnp.where(blk, a, 0.0) for a in mats]
    tds = neumann(ds, 3)
    ns = [_dot(td, a - d) for td, a, d in zip(tds, mats, ds)]
    return [_dot(p, td) for p, td in zip(neumann(ns, 2), tds)]


def _causal_conv(buf_ref, x, w_ref):
    t = x.shape[0]
    buf_ref[SUBLANES:SUBLANES + t, :] = x
    w = w_ref[...]
    y = buf_ref[pl.ds(SUBLANES - 3, t), :] * w[0:1, :]
    for j in range(1, CONV_W):
        y = y + buf_ref[pl.ds(SUBLANES - 3 + j, t), :] * w[j:j + 1, :]
    buf_ref[0:SUBLANES, :] = buf_ref[t:t + SUBLANES, :]
    return y


def _zero_conv_history(bufs):
    for b in bufs:
        b[0:SUBLANES, :] = jnp.zeros((SUBLANES, b.shape[1]), F32)


def _real_row_mask(shape, dim):
    return (_iota2(shape, dim) % SEQ_PAD) >= (SEQ_PAD // 2)


def _rmsnorm_rows(x, w):
    return x * lax.rsqrt(jnp.mean(x * x, axis=-1, keepdims=True) + EPS) * w


def _norm_kernel(x_ref, w_ref, o_ref):
    o_ref[...] = _rmsnorm_rows(x_ref[...], w_ref[...]).astype(o_ref.dtype)


def _norm_rows(x, w, out_dtype):
    m, d = x.shape
    return pl.pallas_call(
        _norm_kernel,
        grid=(m // ROW_TILE,),
        in_specs=[pl.BlockSpec((ROW_TILE, d), lambda i: (i, 0)),
                  pl.BlockSpec((1, d), lambda i: (0, 0))],
        out_specs=pl.BlockSpec((ROW_TILE, d), lambda i: (i, 0)),
        out_shape=jax.ShapeDtypeStruct((m, d), out_dtype),
        compiler_params=_cparams(("parallel",)),
        name="norm_rows",
    )(x, w.reshape(1, d))


def _ple_embed_kernel(p_ref, w_ref, nw_ref, o_ref):
    e = _dot(p_ref[...], w_ref[...])
    o_ref[...] = _rmsnorm_rows(e, nw_ref[...])


def _ple_embed(p, w_proj, layer, norm_w):
    m = p.shape[1]
    return pl.pallas_call(
        _ple_embed_kernel,
        grid=(m // ROW_TILE,),
        in_specs=[pl.BlockSpec((None, ROW_TILE, PLE_DIM), lambda i: (layer, i, 0)),
                  pl.BlockSpec((None, PLE_DIM, D_MODEL), lambda i: (layer, 0, 0)),
                  pl.BlockSpec((1, D_MODEL), lambda i: (0, 0))],
        out_specs=pl.BlockSpec((ROW_TILE, D_MODEL), lambda i: (i, 0)),
        out_shape=jax.ShapeDtypeStruct((m, D_MODEL), F32),
        compiler_params=_cparams(("parallel",)),
        name="ple_embed",
    )(p, w_proj, norm_w.reshape(1, D_MODEL))


def _mm_kernel(x_ref, w_ref, *rest, epilogue, w_nk):
    wbf_ref = rest[-1]
    refs = rest[:-1]

    @pl.when(pl.program_id(1) == 0)
    def _():
        wbf_ref[...] = w_ref[...].astype(BF16)

    x = x_ref[...]
    acc = _dot_nt(x, wbf_ref[...]) if w_nk else jnp.dot(x, wbf_ref[...], preferred_element_type=F32)
    if epilogue == "plain":
        (o_ref,) = refs
        o_ref[...] = acc
    elif epilogue == "residual":
        res_ref, o_ref, obf_ref = refs
        h = res_ref[...] + acc
        o_ref[...] = h
        obf_ref[...] = h.astype(BF16)
    elif epilogue == "ple_gate":
        res_ref, e_ref, o_ref = refs
        o_ref[...] = res_ref[...] + e_ref[...] * jax.nn.sigmoid(acc)
    else:
        raise ValueError(epilogue)


def _matmul(x, w, layer, n_cols, epilogue, extras=(), w_nk=False):
    m, k = x.shape
    tm, tn = MM_TILES[epilogue]
    while m % tm:
        tm //= 2
    assert n_cols % tn == 0
    tile = pl.BlockSpec((tm, tn), lambda j, i: (i, j))
    w_block = (None, tn, k) if w_nk else (None, k, tn)
    w_index = (lambda j, i: (layer, j, 0)) if w_nk else (lambda j, i: (layer, 0, j))
    w_spec = pl.BlockSpec(w_block, w_index, pipeline_mode=pl.Buffered(1))
    in_specs = [pl.BlockSpec((tm, k), lambda j, i: (i, 0)), w_spec] + [tile] * len(extras)
    out_shape = [jax.ShapeDtypeStruct((m, n_cols), F32)]
    if epilogue == "residual":
        out_shape.append(jax.ShapeDtypeStruct((m, n_cols), BF16))
    out = pl.pallas_call(
        functools.partial(_mm_kernel, epilogue=epilogue, w_nk=w_nk),
        grid=(n_cols // tn, m // tm),
        in_specs=in_specs,
        out_specs=[tile] * len(out_shape),
        out_shape=out_shape,
        scratch_shapes=[pltpu.VMEM(w_block[1:], BF16)],
        compiler_params=_cparams(("parallel", "arbitrary")),
        name="matmul_" + epilogue,
    )(x, w, *extras)
    return out if epilogue == "residual" else out[0]


def _cumsum_operands(n_prompt_tiles):
    is_prompt = pl.program_id(0) < n_prompt_tiles
    shift = jnp.where(is_prompt, CHUNK_T.bit_length() - 1, SEQ_PAD.bit_length() - 1)
    ii = _iota2((ROW_TILE, ROW_TILE), 0)
    jj = _iota2((ROW_TILE, ROW_TILE), 1)
    same = lax.shift_right_logical(ii, shift) == lax.shift_right_logical(jj, shift)
    lower = (same & (jj <= ii)).astype(F32)
    upper = (same & (ii <= jj)).astype(F32)
    return is_prompt, lower, upper


def _dot_f32(a, b):
    return jnp.dot(a, b, preferred_element_type=F32, precision=lax.Precision.HIGHEST)


def _gdn_gate_kernel(x_ref, wa_ref, wb_ref, alog_ref, dtb_ref, beta_ref, gc_ref, gct_ref, *,
                     n_prompt_tiles):
    x = x_ref[...]
    is_prompt, lower, upper = _cumsum_operands(n_prompt_tiles)
    live_rows = is_prompt | _real_row_mask((ROW_TILE, LANES), 0)
    a = _dot(x, wa_ref[...])
    g = jnp.where(live_rows, -jnp.exp(alog_ref[...]) * _softplus(a + dtb_ref[...]), 0.0)
    beta_ref[...] = jnp.where(live_rows, jax.nn.sigmoid(_dot(x, wb_ref[...])), 0.0)
    gc_ref[...] = _dot_f32(lower, g)
    gct_ref[...] = _dot_f32(g.T[:GDN_V_HEADS], upper)


def _pad_lanes(x):
    return jnp.pad(x, ((0, 0), (0, LANES - x.shape[1])))


def _gdn_gates(hn, w_ab, a_log, dt_bias, n_prompt_rows):
    m = hn.shape[0]
    wa = w_ab[:, :GDN_V_HEADS]
    wb = w_ab[:, GDN_V_HEADS:]
    row = pl.BlockSpec((ROW_TILE, LANES), lambda i: (i, 0))
    full = lambda shape: pl.BlockSpec(shape, lambda i: (0, 0))
    return pl.pallas_call(
        functools.partial(_gdn_gate_kernel, n_prompt_tiles=n_prompt_rows // ROW_TILE),
        grid=(m // ROW_TILE,),
        in_specs=[pl.BlockSpec((ROW_TILE, D_MODEL), lambda i: (i, 0)),
                  full((D_MODEL, LANES)), full((D_MODEL, LANES)), full((1, LANES)), full((1, LANES))],
        out_specs=[row, row, pl.BlockSpec((GDN_V_HEADS, ROW_TILE), lambda i: (0, i))],
        out_shape=[jax.ShapeDtypeStruct((m, LANES), F32)] * 2
        + [jax.ShapeDtypeStruct((GDN_V_HEADS, m), F32)],
        compiler_params=_cparams(("parallel",)),
        name="gdn_gates",
    )(hn, _pad_lanes(wa), _pad_lanes(wb),
      _pad_lanes(a_log.reshape(1, -1)), _pad_lanes(dt_bias.reshape(1, -1)))


def _ssd_gate_kernel(x_ref, w_ref, alog_ref, dtb_ref, dt_ref, acs_ref, acst_ref, *, n_prompt_tiles):
    x = x_ref[...]
    is_prompt, lower, upper = _cumsum_operands(n_prompt_tiles)
    live_rows = is_prompt | _real_row_mask((ROW_TILE, LANES), 0)
    dt = jnp.where(live_rows, _softplus(_dot(x, w_ref[...]) + dtb_ref[...]), 0.0)
    a = dt * -jnp.exp(alog_ref[...])
    dt_ref[...] = dt
    acs_ref[...] = _dot_f32(lower, a)
    acst_ref[...] = _dot_f32(a.T[:SSD_HEADS], upper)


def _ssd_gates(hn, w, a_log, dt_bias, n_prompt_rows):
    m = hn.shape[0]
    row = pl.BlockSpec((ROW_TILE, LANES), lambda i: (i, 0))
    full = lambda shape: pl.BlockSpec(shape, lambda i: (0, 0))
    return pl.pallas_call(
        functools.partial(_ssd_gate_kernel, n_prompt_tiles=n_prompt_rows // ROW_TILE),
        grid=(m // ROW_TILE,),
        in_specs=[pl.BlockSpec((ROW_TILE, D_MODEL), lambda i: (i, 0)),
                  full((D_MODEL, LANES)), full((1, LANES)), full((1, LANES))],
        out_specs=[row, row, pl.BlockSpec((SSD_HEADS, ROW_TILE), lambda i: (0, i))],
        out_shape=[jax.ShapeDtypeStruct((m, LANES), F32)] * 2
        + [jax.ShapeDtypeStruct((SSD_HEADS, m), F32)],
        compiler_params=_cparams(("parallel",)),
        name="ssd_gates",
    )(hn, _pad_lanes(w), _pad_lanes(a_log.reshape(1, -1)), _pad_lanes(dt_bias.reshape(1, -1)))


def _l2norm_rows(x):
    return x * lax.rsqrt(jnp.sum(x * x, axis=-1, keepdims=True) + EPS)


def _gdn_kernel(*refs, heads, n_prompt_chunks, chunks_per_seq, n_prev):
    (q_ref, k_ref, v_ref, z_ref, beta_ref, gc_ref, gct_ref, cwq_ref, cwk_ref, cwv_ref, nw_ref,
     cq_ref, ck_ref, cv_ref, s0_ref) = refs[:15]
    y_ref, sp_ref, ss_ref, bq, bk, bv, s_scr = refs[15 + n_prev:]
    hg = pl.program_id(0)
    step = pl.program_id(1)
    is_prompt = step < n_prompt_chunks
    gct_row0 = (2 * heads * hg) % gct_ref.shape[0]

    def body(sample, xq, xk, xv):
        qc = jax.nn.silu(_causal_conv(bq, xq, cwq_ref))
        kc = jax.nn.silu(_causal_conv(bk, xk, cwk_ref))
        vc = jax.nn.silu(_causal_conv(bv, xv, cwv_ref))
        incl, strict = _chunk_masks(sample)
        beta_tile, gc_tile = beta_ref[...], gc_ref[...]
        nw = nw_ref[...]
        nv = 2 * heads
        vcols = [slice(lv * GDN_HEAD, (lv + 1) * GDN_HEAD) for lv in range(nv)]
        qs_ = [_l2norm_rows(qc[:, vcols[hh]]) * (GDN_HEAD ** -0.5) for hh in range(heads)]
        ks_ = [_l2norm_rows(kc[:, vcols[hh]]) for hh in range(heads)]
        kks = [_dot_nt(k, k) for k in ks_]
        qks = [_dot_nt(q, k) for q, k in zip(qs_, ks_)]
        beta_c = [_pick_lane(beta_tile, nv * hg + lv) for lv in range(nv)]
        gc_c = [_pick_lane(gc_tile, nv * hg + lv) for lv in range(nv)]
        gc_r = [gct_ref[pl.ds(gct_row0 + lv, 1), :] for lv in range(nv)]
        decay = [jnp.exp(jnp.where(incl, c - r, NEG_BIG)) for c, r in zip(gc_c, gc_r)]
        a_mats = [jnp.where(strict, beta_c[lv] * kks[lv // 2] * decay[lv], 0.0) for lv in range(nv)]
        t_mats = _unit_lower_inverses(a_mats, sample)
        egc = [jnp.exp(c) for c in gc_c]
        uw = [_dot(t_mats[lv], jnp.concatenate([vc[:, vcols[lv]] * beta_c[lv],
                                                ks_[lv // 2] * (beta_c[lv] * egc[lv])], axis=1))
              for lv in range(nv)]
        u = [x[:, :GDN_HEAD] for x in uw]
        w = [x[:, GDN_HEAD:] for x in uw]
        qk_d = [qks[lv // 2] * decay[lv] for lv in range(nv)]
        q_dec = [qs_[lv // 2] * egc[lv] for lv in range(nv)]
        gl_c = [_seq_last_rows(c) if sample else c[CHUNK_T - 1:CHUNK_T, :] for c in gc_c]
        kd_t = [(ks_[lv // 2] * jnp.exp(gl_c[lv] - gc_c[lv])).T for lv in range(nv)]

        if sample:
            wq = [[_dot(jnp.concatenate([w[lv][s * SEQ_PAD:(s + 1) * SEQ_PAD],
                                         q_dec[lv][s * SEQ_PAD:(s + 1) * SEQ_PAD]], axis=0),
                        s0_ref[s, vcols[lv], :]) for s in range(SEQS_PER_CHUNK)] for lv in range(nv)]
            v_new = [u[lv] - jnp.concatenate([x[:SEQ_PAD] for x in wq[lv]], axis=0) for lv in range(nv)]
            o = [jnp.concatenate([x[SEQ_PAD:] for x in wq[lv]], axis=0) + _dot(qk_d[lv], v_new[lv])
                 for lv in range(nv)]
            seq_of_lane = _iota2((GDN_HEAD, CHUNK_T), 1) // SEQ_PAD
            for lv in range(nv):
                v_new_bf = v_new[lv].astype(BF16)
                for s in range(SEQS_PER_CHUNK):
                    dec = jnp.exp(gl_c[lv][s * SEQ_PAD:s * SEQ_PAD + 1, :])
                    upd = _dot(jnp.where(seq_of_lane == s, kd_t[lv], 0.0), v_new_bf)
                    ss_ref[s, vcols[lv], :] = s0_ref[s, vcols[lv], :] * dec + upd
        else:
            s_old = [s_scr[lv] for lv in range(nv)]
            wq = [_dot(jnp.concatenate([w[lv], q_dec[lv]], axis=0), s_old[lv]) for lv in range(nv)]
            v_new = [u[lv] - wq[lv][:CHUNK_T] for lv in range(nv)]
            o = [wq[lv][CHUNK_T:] + _dot(qk_d[lv], v_new[lv]) for lv in range(nv)]
            for lv in range(nv):
                s_new = s_old[lv] * jnp.exp(gl_c[lv]) + _dot(kd_t[lv], v_new[lv])
                s_scr[lv] = s_new
                sp_ref[vcols[lv], :] = s_new

        for lv in range(nv):
            y = _rmsnorm_rows(o[lv], nw) * jax.nn.silu(z_ref[:, vcols[lv]])
            y_ref[:, vcols[lv]] = y.astype(y_ref.dtype)

    @pl.when(is_prompt)
    def _():
        @pl.when(step % chunks_per_seq == 0)
        def _():
            _zero_conv_history((bq, bk, bv))
            s_scr[...] = jnp.zeros(s_scr.shape, F32)
        body(False, q_ref[...], k_ref[...], v_ref[...])

    @pl.when(jnp.logical_not(is_prompt))
    def _():
        real = _real_row_mask((CHUNK_T, 1), 0)
        _zero_conv_history((bq, bk, bv))
        body(True, jnp.where(real, q_ref[...], cq_ref[...]), jnp.where(real, k_ref[...], ck_ref[...]),
             jnp.where(real, v_ref[...], cv_ref[...]))


class _Dims(NamedTuple):
    n_prompt_seqs: int
    chunks_per_seq: int
    n_sample_seqs: int

    @property
    def n_prompt_chunks(self):
        return self.n_prompt_seqs * self.chunks_per_seq

    @property
    def n_steps(self):
        return self.n_prompt_chunks + self.n_sample_seqs // SEQS_PER_CHUNK

    @property
    def n_rows(self):
        return self.n_steps * CHUNK_T


def _call_mixer(body, name, dims, layer, n_groups, in_specs, args, prev, *, y_width, y_block,
                state_rows, state_block, state_cols, n_layers, scratch):
    n_pc, cps = dims.n_prompt_chunks, dims.chunks_per_seq
    in_specs = list(in_specs) + [pl.BlockSpec(memory_space=pl.ANY)] * len(prev)
    n_in = len(args)
    out_specs = [
        pl.BlockSpec((CHUNK_T, y_block), lambda g, t: (t, g)),
        pl.BlockSpec((None, None, state_block, state_cols),
                     lambda g, t: (layer, jnp.minimum(t // cps, dims.n_prompt_seqs - 1), g, 0)),
        pl.BlockSpec((None, SEQS_PER_CHUNK, state_block, state_cols),
                     lambda g, t: (layer, jnp.maximum(t - n_pc, 0), g, 0))]
    out_shape = [jax.ShapeDtypeStruct((dims.n_rows, y_width), BF16),
                 jax.ShapeDtypeStruct((n_layers, dims.n_prompt_seqs, state_rows, state_cols), F32),
                 jax.ShapeDtypeStruct((n_layers, dims.n_sample_seqs, state_rows, state_cols), F32)]
    return pl.pallas_call(
        functools.partial(body, n_prev=len(prev)),
        grid=(n_groups, dims.n_steps),
        in_specs=in_specs,
        out_specs=out_specs,
        out_shape=out_shape,
        input_output_aliases={n_in + i: 1 + i for i in range(len(prev))},
        scratch_shapes=scratch,
        compiler_params=_cparams(("parallel", "arbitrary")),
        name=name,
    )(*args, *prev)


def _gdn_mixer(proj, gates, conv_w, norm_w, cache8, states, layer, prev, dims):
    beta, gc, gct = gates
    hb, heads = GDN_HEAD, GDN_HEADS_PER_STEP
    qw, vw = heads * hb, 2 * heads * hb
    qkb = GDN_QK_W // qw
    vb = (2 * GDN_QK_W) // vw
    zb = GDN_CONV_DIM // vw
    gct_rows = max(SUBLANES, 2 * heads)
    n_pc = dims.n_prompt_chunks
    sample_blk = lambda t: jnp.maximum(t - n_pc, 0)

    def spec(width, col_off, row_fn=lambda t: t):
        return pl.BlockSpec((CHUNK_T, width), lambda g, t: (row_fn(t), col_off + g))

    row128 = pl.BlockSpec((CHUNK_T, LANES), lambda g, t: (t, 0))
    in_specs = [spec(qw, 0), spec(qw, qkb), spec(vw, vb), spec(vw, zb), row128, row128,
                pl.BlockSpec((gct_rows, CHUNK_T), lambda g, t: ((2 * heads * g) // gct_rows, t)),
                pl.BlockSpec((CONV_W, qw), lambda g, t: (0, g)),
                pl.BlockSpec((CONV_W, qw), lambda g, t: (0, qkb + g)),
                pl.BlockSpec((CONV_W, vw), lambda g, t: (0, vb + g)),
                pl.BlockSpec((1, hb), lambda g, t: (0, 0)),
                spec(qw, 0, sample_blk), spec(qw, qkb, sample_blk), spec(vw, vb, sample_blk),
                pl.BlockSpec((None, SEQS_PER_CHUNK, vw, hb), lambda g, t: (layer, sample_blk(t), g, 0))]
    args = [proj, proj, proj, proj, beta, gc, gct, conv_w, conv_w, conv_w, norm_w.reshape(1, hb),
            cache8, cache8, cache8, states]
    return _call_mixer(
        functools.partial(_gdn_kernel, heads=heads, n_prompt_chunks=dims.n_prompt_chunks,
                          chunks_per_seq=dims.chunks_per_seq),
        "gdn_mixer", dims, layer, GDN_QK_HEADS // heads, in_specs, args, prev,
        y_width=GDN_V_W, y_block=vw, state_rows=GDN_V_HEADS * hb, state_block=vw, state_cols=hb,
        n_layers=states.shape[0],
        scratch=[pltpu.VMEM((CHUNK_T + SUBLANES, qw), F32), pltpu.VMEM((CHUNK_T + SUBLANES, qw), F32),
                 pltpu.VMEM((CHUNK_T + SUBLANES, vw), F32), pltpu.VMEM((2 * heads, hb, hb), F32)])


def _ssd_kernel(*refs, groups, n_prompt_chunks, chunks_per_seq, n_prev):
    (z_ref, x_ref, b_ref, c_ref, dt_ref, acs_ref, acst_ref, cwx_ref, cwb_ref, cwc_ref,
     cbx_ref, cbb_ref, cbc_ref, de_ref, nw_ref, cx_ref, cb_ref, cc_ref, s0_ref) = refs[:19]
    y_ref, sp_ref, ss_ref, bx, bb, bc, s_scr = refs[19 + n_prev:]
    grp = pl.program_id(0)
    step = pl.program_id(1)
    is_prompt = step < n_prompt_chunks
    t = CHUNK_T
    gw, n = SSD_GW, SSD_STATE
    gcols = [slice(gi * gw, (gi + 1) * gw) for gi in range(groups)]
    ncols = [slice(gi * n, (gi + 1) * n) for gi in range(groups)]
    n_heads = groups * SSD_HPG

    def scaled_by_head(s_mat, col):
        eb = jnp.broadcast_to(jnp.exp(acst_ref[:, col:col + 1]), (n_heads, n))
        return jnp.concatenate([s_mat[r * SSD_HEAD:(r + 1) * SSD_HEAD] * eb[r:r + 1, :]
                                for r in range(s_mat.shape[0] // SSD_HEAD)], axis=0)

    def body(sample, xx, xb, xc):
        xg = jax.nn.silu(_causal_conv(bx, xx, cwx_ref) + cbx_ref[...])
        bg = jax.nn.silu(_causal_conv(bb, xb, cwb_ref) + cbb_ref[...])
        cg = jax.nn.silu(_causal_conv(bc, xc, cwc_ref) + cbc_ref[...])

        dt_tile, acs_tile = dt_ref[...], acs_ref[...]
        sel_shape = (LANES, groups * gw)
        sel = _iota2(sel_shape, 0) == grp * n_heads + _iota2(sel_shape, 1) // SSD_HEAD
        parts = jnp.concatenate(_split3(dt_tile) + _split3(acs_tile), axis=0)
        ex = jnp.dot(parts, sel.astype(BF16), preferred_element_type=F32)
        dt_e = ex[0:t] + ex[t:2 * t] + ex[2 * t:3 * t]
        acs_e = ex[3 * t:4 * t] + ex[4 * t:5 * t] + ex[5 * t:6 * t]
        acs_last_e = _seq_last_rows(acs_e) if sample else acs_e[t - 1:t, :]

        xdt = xg * dt_e
        incl, _ = _chunk_masks(sample)
        cbs = [_dot_nt(cg[:, ncols[gi]], bg[:, ncols[gi]]) for gi in range(groups)]
        acs_c = [_pick_lane(acs_tile, grp * n_heads + r) for r in range(n_heads)]
        ms_ = [cbs[r // SSD_HPG] * jnp.exp(jnp.where(incl, acs_c[r] - acst_ref[r:r + 1, :], NEG_BIG))
               for r in range(n_heads)]
        lane_lo = _iota2((t, 2 * SSD_HEAD), 1) < SSD_HEAD
        pairs = []
        for pr in range(n_heads // 2):
            xpair = xdt[:, pr * 2 * SSD_HEAD:(pr + 1) * 2 * SSD_HEAD]
            rhs = jnp.concatenate([jnp.where(lane_lo, xpair, 0.0), jnp.where(lane_lo, 0.0, xpair)],
                                  axis=0)
            pairs.append(_dot(jnp.concatenate([ms_[2 * pr], ms_[2 * pr + 1]], axis=1), rhs))
        y_diag = jnp.concatenate(pairs, axis=1)

        xd_t = (xdt * jnp.exp(acs_last_e - acs_e)).T
        if sample:
            offs = []
            zeros = jnp.zeros((SEQ_PAD, n), F32)
            seq_of_lane = _iota2((groups * gw, t), 1) // SEQ_PAD
            bg_bf = bg.astype(BF16)
            for s in range(SEQS_PER_CHUNK):
                rows = slice(s * SEQ_PAD, (s + 1) * SEQ_PAD)
                s_old = s0_ref[s]
                offs.append(jnp.concatenate(
                    [_dot_nt(jnp.concatenate([cg[rows, ncols[gi]], zeros], axis=0),
                             s_old[gcols[gi]])[:SEQ_PAD] for gi in range(groups)], axis=1))
                xd_s = jnp.where(seq_of_lane == s, xd_t, 0.0)
                upd = jnp.concatenate([_dot(xd_s[gcols[gi]], bg_bf[:, ncols[gi]])
                                       for gi in range(groups)], axis=0)
                ss_ref[s] = scaled_by_head(s_old, (s + 1) * SEQ_PAD - 1) + upd
            y_off = jnp.concatenate(offs, axis=0)
        else:
            s_old = s_scr[...]
            y_off = jnp.concatenate([_dot_nt(cg[:, ncols[gi]], s_old[gcols[gi]])
                                     for gi in range(groups)], axis=1)
            upd = jnp.concatenate([_dot(xd_t[gcols[gi]], bg[:, ncols[gi]]) for gi in range(groups)],
                                  axis=0)
            s_new = scaled_by_head(s_old, t - 1) + upd
            s_scr[...] = s_new
            sp_ref[...] = s_new

        y = y_diag + y_off * jnp.exp(acs_e) + de_ref[...] * xg
        y = y * jax.nn.silu(z_ref[...])
        nw = nw_ref[...]
        for gi in range(groups):
            yg = y[:, gcols[gi]]
            yg = yg * lax.rsqrt(jnp.mean(yg * yg, axis=-1, keepdims=True) + EPS) * nw[:, gcols[gi]]
            y_ref[:, gcols[gi]] = yg.astype(y_ref.dtype)

    @pl.when(is_prompt)
    def _():
        @pl.when(step % chunks_per_seq == 0)
        def _():
            _zero_conv_history((bx, bb, bc))
            s_scr[...] = jnp.zeros(s_scr.shape, F32)
        body(False, x_ref[...], b_ref[...], c_ref[...])

    @pl.when(jnp.logical_not(is_prompt))
    def _():
        real = _real_row_mask((CHUNK_T, 1), 0)
        _zero_conv_history((bx, bb, bc))
        body(True, jnp.where(real, x_ref[...], cx_ref[...]), jnp.where(real, b_ref[...], cb_ref[...]),
             jnp.where(real, c_ref[...], cc_ref[...]))


def _ssd_mixer(proj, gates, conv_w, conv_b, d_skip, norm_w, cache8, states, layer, prev, dims):
    dt, acs, acst = gates
    groups = SSD_GROUPS_PER_STEP
    gw, n = groups * SSD_GW, groups * SSD_STATE
    xb = SSD_D_INNER // gw
    bb = (2 * SSD_D_INNER) // n
    cb = bb + SSD_GN // n
    wbb = SSD_D_INNER // n
    wcb = wbb + SSD_GN // n
    n_pc = dims.n_prompt_chunks
    sample_blk = lambda t: jnp.maximum(t - n_pc, 0)

    def spec(width, col_off, row_fn=lambda t: t):
        return pl.BlockSpec((CHUNK_T, width), lambda g, t: (row_fn(t), col_off + g))

    def wspec(rows, width, col_off):
        return pl.BlockSpec((rows, width), lambda g, t: (0, col_off + g))

    row128 = pl.BlockSpec((CHUNK_T, LANES), lambda g, t: (t, 0))
    in_specs = [spec(gw, 0), spec(gw, xb), spec(n, bb), spec(n, cb), row128, row128,
                pl.BlockSpec((groups * SSD_HPG, CHUNK_T), lambda g, t: (g, t)),
                wspec(CONV_W, gw, 0), wspec(CONV_W, n, wbb), wspec(CONV_W, n, wcb),
                wspec(1, gw, 0), wspec(1, n, wbb), wspec(1, n, wcb),
                wspec(1, gw, 0), wspec(1, gw, 0),
                spec(gw, 0, sample_blk), spec(n, wbb, sample_blk), spec(n, wcb, sample_blk),
                pl.BlockSpec((None, SEQS_PER_CHUNK, gw, SSD_STATE),
                             lambda g, t: (layer, sample_blk(t), g, 0))]
    conv_b2 = conv_b.reshape(1, -1)
    d_e = jnp.repeat(d_skip, SSD_HEAD).reshape(1, SSD_D_INNER)
    args = [proj, proj, proj, proj, dt, acs, acst, conv_w, conv_w, conv_w,
            conv_b2, conv_b2, conv_b2, d_e, norm_w.reshape(1, SSD_D_INNER),
            cache8, cache8, cache8, states]
    return _call_mixer(
        functools.partial(_ssd_kernel, groups=groups, n_prompt_chunks=dims.n_prompt_chunks,
                          chunks_per_seq=dims.chunks_per_seq),
        "ssd_mixer", dims, layer, SSD_GROUPS // groups, in_specs, args, prev,
        y_width=SSD_D_INNER, y_block=gw, state_rows=SSD_HEADS * SSD_HEAD, state_block=gw,
        state_cols=SSD_STATE, n_layers=states.shape[0],
        scratch=[pltpu.VMEM((CHUNK_T + SUBLANES, gw), F32), pltpu.VMEM((CHUNK_T + SUBLANES, n), F32),
                 pltpu.VMEM((CHUNK_T + SUBLANES, n), F32), pltpu.VMEM((gw, SSD_STATE), F32)])


def _pad_sequences(x, n_real):
    pads = [(0, 0)] * (x.ndim - 2) + [(SEQ_PAD - n_real, 0), (0, 0)]
    xp = jnp.pad(x, pads)
    return xp.reshape(x.shape[:-3] + (x.shape[-3] * SEQ_PAD, x.shape[-1]))


def _history_rows(cache):
    n_real = SEQ_PAD // 2
    lo = SEQ_PAD - n_real - (CONV_W - 1)
    xp = jnp.pad(cache, ((0, 0), (lo, n_real), (0, 0)))
    return xp.reshape(cache.shape[0] * SEQ_PAD, cache.shape[2])


def kernel(x_prompt, x_sample, state_gdn, cache_gdn_conv, state_ssd, cache_ssd_conv, p_prompt, p_sample,
           norm_w, gdn_w_in, gdn_conv_w, gdn_A_log, gdn_dt_bias, gdn_norm_w, gdn_w_out,
           ssd_w_in, ssd_conv_w, ssd_conv_b, ssd_A_log, ssd_dt_bias, ssd_D, ssd_norm_w, ssd_w_out,
           ple_w_proj, ple_w_gate, ple_norm_w, final_norm_w):
    bp, lp, d = x_prompt.shape
    bs, ls, _ = x_sample.shape
    depth = norm_w.shape[0]
    assert d == D_MODEL and ls == SEQ_PAD // 2 and lp % CHUNK_T == 0 and bs % SEQS_PER_CHUNK == 0
    mp, ms = bp * lp, bs * SEQ_PAD
    assert mp % ROW_TILE == 0 and ms % ROW_TILE == 0

    h = jnp.concatenate([x_prompt.reshape(mp, d), _pad_sequences(x_sample, ls)], axis=0)
    p = jnp.concatenate([p_prompt.reshape(depth, mp, PLE_DIM), _pad_sequences(p_sample, ls)], axis=1)

    def conv_caches(proj, lo, hi):
        c_p = jnp.stack([proj[(b + 1) * lp - (CONV_W - 1):(b + 1) * lp, lo:hi] for b in range(bp)])
        c_s = proj[mp:, lo:hi].reshape(bs, SEQ_PAD, hi - lo)[:, SEQ_PAD - (CONV_W - 1):, :]
        return c_p, c_s

    dims = _Dims(n_prompt_seqs=bp, chunks_per_seq=lp // CHUNK_T, n_sample_seqs=bs)
    gdn_states = state_gdn.reshape(state_gdn.shape[0], bs, GDN_V_HEADS * GDN_HEAD, GDN_HEAD)
    ssd_states = state_ssd.reshape(state_ssd.shape[0], bs, SSD_HEADS * SSD_HEAD, SSD_STATE)
    gdn_w_in_t = jnp.swapaxes(gdn_w_in, 1, 2)
    ssd_w_in_t = jnp.swapaxes(ssd_w_in, 1, 2)
    caches = {k: [] for k in ("cg_p", "cs_p", "cg_s", "cs_s")}
    gdn_prev, ssd_prev = (), ()
    hn = _norm_rows(h, norm_w[0], BF16)
    for i in range(depth):
        j = i // 2
        if i % 2 == 0:
            proj = _matmul(hn, gdn_w_in_t, j, GDN_MAIN, "plain", w_nk=True)
            gates = _gdn_gates(hn, gdn_w_in[j, :, GDN_MAIN:], gdn_A_log[j], gdn_dt_bias[j], mp)
            y, *gdn_prev = _gdn_mixer(proj, gates, gdn_conv_w[j], gdn_norm_w[j],
                                      _history_rows(cache_gdn_conv[j]), gdn_states, j, gdn_prev, dims)
            c_p, c_s = conv_caches(proj, 0, GDN_CONV_DIM)
            w_out = gdn_w_out
            keys = ("cg_p", "cg_s")
        else:
            proj = _matmul(hn, ssd_w_in_t, j, SSD_MAIN, "plain", w_nk=True)
            gates = _ssd_gates(hn, ssd_w_in[j, :, SSD_MAIN:], ssd_A_log[j], ssd_dt_bias[j], mp)
            y, *ssd_prev = _ssd_mixer(proj, gates, ssd_conv_w[j], ssd_conv_b[j], ssd_D[j], ssd_norm_w[j],
                                      _history_rows(cache_ssd_conv[j]), ssd_states, j, ssd_prev, dims)
            c_p, c_s = conv_caches(proj, SSD_D_INNER, SSD_MAIN)
            w_out = ssd_w_out
            keys = ("cs_p", "cs_s")
        for key, val in zip(keys, (c_p, c_s)):
            caches[key].append(val)
        h_mid, h_mid_bf = _matmul(y, w_out, j, D_MODEL, "residual", extras=(h,))
        e = _ple_embed(p, ple_w_proj, i, ple_norm_w[i])
        h = _matmul(h_mid_bf, ple_w_gate, i, D_MODEL, "ple_gate", extras=(h_mid, e))
        last = i + 1 == depth
        hn = _norm_rows(h, final_norm_w if last else norm_w[i + 1], F32 if last else BF16)

    y_prompt = hn[:mp].reshape(bp, lp, d)
    y_sample = hn[mp:].reshape(bs, SEQ_PAD, d)[:, SEQ_PAD - ls:, :]
    cc = {k: jnp.stack(v) for k, v in caches.items()}
    sg_p, sg_s = (s.reshape(s.shape[:2] + (GDN_V_HEADS, GDN_HEAD, GDN_HEAD)) for s in gdn_prev)
    ss_p, ss_s = (s.reshape(s.shape[:2] + (SSD_HEADS, SSD_HEAD, SSD_STATE)) for s in ssd_prev)
    return (y_prompt, y_sample, sg_p, cc["cg_p"], ss_p, cc["cs_p"], sg_s, cc["cg_s"], ss_s, cc["cs_s"])
```

```python
import functools
from typing import NamedTuple

import jax
import jax.numpy as jnp
from jax import lax
from jax.experimental import pallas as pl
from jax.experimental.pallas import tpu as pltpu

F32 = jnp.float32
BF16 = jnp.bfloat16

EPS = 1e-6
CONV_W = 4
D_MODEL = 2048
PLE_DIM = 256

GDN_HEAD = 128
GDN_QK_HEADS = 16
GDN_V_HEADS = 32
GDN_QK_W = GDN_QK_HEADS * GDN_HEAD
GDN_V_W = GDN_V_HEADS * GDN_HEAD
GDN_CONV_DIM = 2 * GDN_QK_W + GDN_V_W
GDN_MAIN = GDN_CONV_DIM + GDN_V_W

SSD_D_INNER = 4096
SSD_HEAD = 64
SSD_HEADS = 64
SSD_STATE = 128
SSD_GROUPS = 8
SSD_HPG = SSD_HEADS // SSD_GROUPS
SSD_GN = SSD_GROUPS * SSD_STATE
SSD_CONV_DIM = SSD_D_INNER + 2 * SSD_GN
SSD_MAIN = SSD_D_INNER + SSD_CONV_DIM
SSD_GW = SSD_HPG * SSD_HEAD

LANES = 128
SUBLANES = 8
CHUNK_T = 128
SEQ_PAD = 8
SEQS_PER_CHUNK = CHUNK_T // SEQ_PAD
INV_BLOCK = 16
GDN_HEADS_PER_STEP = 4
SSD_GROUPS_PER_STEP = 2
ROW_TILE = 512
MM_TILES = {"plain": (1024, 2048), "residual": (512, 1024)}
VMEM_LIMIT = 56 * 1024 * 1024
NEG_BIG = -1e30


def _cparams(sem):
    return pltpu.CompilerParams(dimension_semantics=sem, vmem_limit_bytes=VMEM_LIMIT)


def _dot(a, b):
    return jnp.dot(a.astype(BF16), b.astype(BF16), preferred_element_type=F32)


def _dot_nt(a, b):
    return lax.dot_general(a.astype(BF16), b.astype(BF16), (((1,), (1,)), ((), ())),
                           preferred_element_type=F32)


def _split3(x):
    x1 = x.astype(BF16)
    r1 = x - x1.astype(F32)
    x2 = r1.astype(BF16)
    r2 = r1 - x2.astype(F32)
    return [x1, x2, r2.astype(BF16)]


def _softplus(x):
    return jnp.maximum(x, 0.0) + jnp.log1p(jnp.exp(-jnp.abs(x)))


def _iota2(shape, dim):
    return lax.broadcasted_iota(jnp.int32, shape, dim)


def _pick_lane(x, idx):
    lane = _iota2(x.shape, 1)
    return jnp.sum(jnp.where(lane == idx, x, 0.0), axis=1, keepdims=True)


def _chunk_masks(sample):
    ii = _iota2((CHUNK_T, CHUNK_T), 0)
    jj = _iota2((CHUNK_T, CHUNK_T), 1)
    incl = ii >= jj
    strict = ii > jj
    if sample:
        same = (ii // SEQ_PAD) == (jj // SEQ_PAD)
        incl = incl & same
        strict = strict & same
    return incl, strict


def _seq_last_rows(x):
    rows, cols = x.shape
    x3 = x.reshape(rows // SEQ_PAD, SEQ_PAD, cols)
    last = jnp.broadcast_to(x3[:, SEQ_PAD - 1:SEQ_PAD, :], x3.shape)
    return last.reshape(rows, cols)


def _unit_lower_inverses(mats, sample):
    ii = _iota2(mats[0].shape, 0)
    jj = _iota2(mats[0].shape, 1)
    eye = (ii == jj).astype(F32)

    def neumann(xs, squarings):
        ts = [eye - x for x in xs]
        ps = xs
        for _ in range(squarings):
            ps = [_dot(p, p) for p in ps]
            ts = [t + _dot(t, p) for t, p in zip(ts, ps)]
        return ts

    if sample:
        assert SEQ_PAD == 8
        return neumann(mats, 2)
    assert INV_BLOCK == 16 and CHUNK_T // INV_BLOCK == 8
    blk = (ii // INV_BLOCK) == (jj // INV_BLOCK)
    ds = [jnp.where(blk, a, 0.0) for a in mats]
    tds = neumann(ds, 3)
    ns = [_dot(td, a - d) for td, a, d in zip(tds, mats, ds)]
    return [_dot(p, td) for p, td in zip(neumann(ns, 2), tds)]


def _causal_conv(buf_ref, x, w_ref):
    t = x.shape[0]
    buf_ref[SUBLANES:SUBLANES + t, :] = x
    w = w_ref[...]
    y = buf_ref[pl.ds(SUBLANES - 3, t), :] * w[0:1, :]
    for j in range(1, CONV_W):
        y = y + buf_ref[pl.ds(SUBLANES - 3 + j, t), :] * w[j:j + 1, :]
    buf_ref[0:SUBLANES, :] = buf_ref[t:t + SUBLANES, :]
    return y


def _zero_conv_history(bufs):
    for b in bufs:
        b[0:SUBLANES, :] = jnp.zeros((SUBLANES, b.shape[1]), F32)


def _real_row_mask(shape, dim):
    return (_iota2(shape, dim) % SEQ_PAD) >= (SEQ_PAD // 2)


def _rmsnorm_rows(x, w):
    return x * lax.rsqrt(jnp.mean(x * x, axis=-1, keepdims=True) + EPS) * w


def _norm_kernel(x_ref, w_ref, o_ref):
    o_ref[...] = _rmsnorm_rows(x_ref[...], w_ref[...]).astype(o_ref.dtype)


def _norm_rows(x, w, out_dtype):
    m, d = x.shape
    return pl.pallas_call(
        _norm_kernel,
        grid=(m // ROW_TILE,),
        in_specs=[pl.BlockSpec((ROW_TILE, d), lambda i: (i, 0)),
                  pl.BlockSpec((1, d), lambda i: (0, 0))],
        out_specs=pl.BlockSpec((ROW_TILE, d), lambda i: (i, 0)),
        out_shape=jax.ShapeDtypeStruct((m, d), out_dtype),
        compiler_params=_cparams(("parallel",)),
        name="norm_rows",
    )(x, w.reshape(1, d))


def _ple_kernel(hbf_ref, wg_ref, h_ref, p_ref, wp_ref, pnw_ref, nnw_ref, oa_ref, ob_ref, *,
                n_prompt_tiles):
    gate = jax.nn.sigmoid(jnp.dot(hbf_ref[...], wg_ref[...], preferred_element_type=F32))
    e = _rmsnorm_rows(_dot(p_ref[...], wp_ref[...]), pnw_ref[...])
    h = h_ref[...] + e * gate
    hn = _rmsnorm_rows(h, nnw_ref[...])
    if n_prompt_tiles is None:
        oa_ref[...] = h
        ob_ref[...] = hn.astype(ob_ref.dtype)
    else:
        is_prompt = pl.program_id(0) < n_prompt_tiles

        @pl.when(is_prompt)
        def _():
            oa_ref[...] = hn

        @pl.when(jnp.logical_not(is_prompt))
        def _():
            ob_ref[...] = hn


def _ple_add(h_bf, h, p, w_gate_bf, w_proj, layer, ple_norm_w, next_norm_w, n_prompt_rows=None):
    m = h.shape[0]
    row = lambda width: pl.BlockSpec((ROW_TILE, width), lambda i: (i, 0))
    vec = pl.BlockSpec((1, D_MODEL), lambda i: (0, 0))
    once = dict(pipeline_mode=pl.Buffered(1))
    if n_prompt_rows is None:
        n_pt = None
        out_specs = [row(D_MODEL), row(D_MODEL)]
        out_shape = [jax.ShapeDtypeStruct((m, D_MODEL), F32), jax.ShapeDtypeStruct((m, D_MODEL), BF16)]
    else:
        n_pt = n_prompt_rows // ROW_TILE
        out_specs = [pl.BlockSpec((ROW_TILE, D_MODEL), lambda i: (jnp.minimum(i, n_pt - 1), 0)),
                     pl.BlockSpec((ROW_TILE, D_MODEL), lambda i: (jnp.maximum(i - n_pt, 0), 0))]
        out_shape = [jax.ShapeDtypeStruct((n_prompt_rows, D_MODEL), F32),
                     jax.ShapeDtypeStruct((m - n_prompt_rows, D_MODEL), F32)]
    return pl.pallas_call(
        functools.partial(_ple_kernel, n_prompt_tiles=n_pt),
        grid=(m // ROW_TILE,),
        in_specs=[row(D_MODEL),
                  pl.BlockSpec((None, D_MODEL, D_MODEL), lambda i: (layer, 0, 0), **once),
                  row(D_MODEL),
                  pl.BlockSpec((None, ROW_TILE, PLE_DIM), lambda i: (layer, i, 0)),
                  pl.BlockSpec((None, PLE_DIM, D_MODEL), lambda i: (layer, 0, 0), **once),
                  vec, vec],
        out_specs=out_specs,
        out_shape=out_shape,
        compiler_params=_cparams(("arbitrary",)),
        name="ple_add",
    )(h_bf, w_gate_bf, h, p, w_proj, ple_norm_w.reshape(1, D_MODEL), next_norm_w.reshape(1, D_MODEL))


def _mm_kernel(x_ref, w_ref, *rest, epilogue, w_nk):
    wbf_ref = rest[-1]
    refs = rest[:-1]

    @pl.when(pl.program_id(1) == 0)
    def _():
        wbf_ref[...] = w_ref[...].astype(BF16)

    x = x_ref[...]
    acc = _dot_nt(x, wbf_ref[...]) if w_nk else jnp.dot(x, wbf_ref[...], preferred_element_type=F32)
    if epilogue == "plain":
        (o_ref,) = refs
        o_ref[...] = acc
    elif epilogue == "residual":
        res_ref, o_ref, obf_ref = refs
        h = res_ref[...] + acc
        o_ref[...] = h
        obf_ref[...] = h.astype(BF16)
    else:
        raise ValueError(epilogue)


def _matmul(x, w, layer, n_cols, epilogue, extras=(), w_nk=False):
    m, k = x.shape
    tm, tn = MM_TILES[epilogue]
    while m % tm:
        tm //= 2
    assert n_cols % tn == 0
    tile = pl.BlockSpec((tm, tn), lambda j, i: (i, j))
    w_block = (None, tn, k) if w_nk else (None, k, tn)
    w_index = (lambda j, i: (layer, j, 0)) if w_nk else (lambda j, i: (layer, 0, j))
    w_spec = pl.BlockSpec(w_block, w_index, pipeline_mode=pl.Buffered(1))
    in_specs = [pl.BlockSpec((tm, k), lambda j, i: (i, 0)), w_spec] + [tile] * len(extras)
    out_shape = [jax.ShapeDtypeStruct((m, n_cols), F32)]
    if epilogue == "residual":
        out_shape.append(jax.ShapeDtypeStruct((m, n_cols), BF16))
    out = pl.pallas_call(
        functools.partial(_mm_kernel, epilogue=epilogue, w_nk=w_nk),
        grid=(n_cols // tn, m // tm),
        in_specs=in_specs,
        out_specs=[tile] * len(out_shape),
        out_shape=out_shape,
        scratch_shapes=[pltpu.VMEM(w_block[1:], BF16)],
        compiler_params=_cparams(("parallel", "arbitrary")),
        name="matmul_" + epilogue,
    )(x, w, *extras)
    return out if epilogue == "residual" else out[0]


def _dot_f32(a, b):
    return jnp.dot(a, b, preferred_element_type=F32, precision=lax.Precision.HIGHEST)


def _chunk_cumsums(a, n_heads, is_prompt):
    shift = jnp.where(is_prompt, CHUNK_T.bit_length() - 1, SEQ_PAD.bit_length() - 1)
    ii = _iota2((CHUNK_T, CHUNK_T), 0)
    jj = _iota2((CHUNK_T, CHUNK_T), 1)
    same = lax.shift_right_logical(ii, shift) == lax.shift_right_logical(jj, shift)
    lower = (same & (jj <= ii)).astype(F32)
    upper = (same & (ii <= jj)).astype(F32)
    a_t = a.T[:n_heads]
    blocks = [slice(b * CHUNK_T, (b + 1) * CHUNK_T) for b in range(ROW_TILE // CHUNK_T)]
    sums = jnp.concatenate([_dot_f32(lower, a[b]) for b in blocks], axis=0)
    sums_t = jnp.concatenate([_dot_f32(a_t[:, b], upper) for b in blocks], axis=1)
    return sums, sums_t


def _gdn_gate_kernel(x_ref, wa_ref, wb_ref, alog_ref, dtb_ref, beta_ref, gc_ref, gct_ref, *,
                     n_prompt_tiles):
    x = x_ref[...]
    is_prompt = pl.program_id(0) < n_prompt_tiles
    live_rows = is_prompt | _real_row_mask((ROW_TILE, LANES), 0)
    a = _dot(x, wa_ref[...])
    g = jnp.where(live_rows, -jnp.exp(alog_ref[...]) * _softplus(a + dtb_ref[...]), 0.0)
    beta_ref[...] = jnp.where(live_rows, jax.nn.sigmoid(_dot(x, wb_ref[...])), 0.0)
    gc_ref[...], gct_ref[...] = _chunk_cumsums(g, GDN_V_HEADS, is_prompt)


def _pad_lanes(x):
    return jnp.pad(x, ((0, 0), (0, LANES - x.shape[1])))


def _gdn_gates(hn, w_ab, a_log, dt_bias, n_prompt_rows):
    m = hn.shape[0]
    wa = w_ab[:, :GDN_V_HEADS]
    wb = w_ab[:, GDN_V_HEADS:]
    row = pl.BlockSpec((ROW_TILE, LANES), lambda i: (i, 0))
    full = lambda shape: pl.BlockSpec(shape, lambda i: (0, 0))
    return pl.pallas_call(
        functools.partial(_gdn_gate_kernel, n_prompt_tiles=n_prompt_rows // ROW_TILE),
        grid=(m // ROW_TILE,),
        in_specs=[pl.BlockSpec((ROW_TILE, D_MODEL), lambda i: (i, 0)),
                  full((D_MODEL, LANES)), full((D_MODEL, LANES)), full((1, LANES)), full((1, LANES))],
        out_specs=[row, row, pl.BlockSpec((GDN_V_HEADS, ROW_TILE), lambda i: (0, i))],
        out_shape=[jax.ShapeDtypeStruct((m, LANES), F32)] * 2
        + [jax.ShapeDtypeStruct((GDN_V_HEADS, m), F32)],
        compiler_params=_cparams(("parallel",)),
        name="gdn_gates",
    )(hn, _pad_lanes(wa), _pad_lanes(wb),
      _pad_lanes(a_log.reshape(1, -1)), _pad_lanes(dt_bias.reshape(1, -1)))


def _ssd_gate_kernel(x_ref, w_ref, alog_ref, dtb_ref, dt_ref, acs_ref, acst_ref, *, n_prompt_tiles):
    x = x_ref[...]
    is_prompt = pl.program_id(0) < n_prompt_tiles
    live_rows = is_prompt | _real_row_mask((ROW_TILE, LANES), 0)
    dt = jnp.where(live_rows, _softplus(_dot(x, w_ref[...]) + dtb_ref[...]), 0.0)
    dt_ref[...] = dt
    acs_ref[...], acst_ref[...] = _chunk_cumsums(dt * -jnp.exp(alog_ref[...]), SSD_HEADS, is_prompt)


def _ssd_gates(hn, w, a_log, dt_bias, n_prompt_rows):
    m = hn.shape[0]
    row = pl.BlockSpec((ROW_TILE, LANES), lambda i: (i, 0))
    full = lambda shape: pl.BlockSpec(shape, lambda i: (0, 0))
    return pl.pallas_call(
        functools.partial(_ssd_gate_kernel, n_prompt_tiles=n_prompt_rows // ROW_TILE),
        grid=(m // ROW_TILE,),
        in_specs=[pl.BlockSpec((ROW_TILE, D_MODEL), lambda i: (i, 0)),
                  full((D_MODEL, LANES)), full((1, LANES)), full((1, LANES))],
        out_specs=[row, row, pl.BlockSpec((SSD_HEADS, ROW_TILE), lambda i: (0, i))],
        out_shape=[jax.ShapeDtypeStruct((m, LANES), F32)] * 2
        + [jax.ShapeDtypeStruct((SSD_HEADS, m), F32)],
        compiler_params=_cparams(("parallel",)),
        name="ssd_gates",
    )(hn, _pad_lanes(w), _pad_lanes(a_log.reshape(1, -1)), _pad_lanes(dt_bias.reshape(1, -1)))


def _l2norm_rows(x):
    return x * lax.rsqrt(jnp.sum(x * x, axis=-1, keepdims=True) + EPS)


def _gdn_kernel(*refs, heads, n_prompt_chunks, chunks_per_seq, n_prev):
    (q_ref, k_ref, v_ref, z_ref, beta_ref, gc_ref, gct_ref, cwq_ref, cwk_ref, cwv_ref, nw_ref,
     cq_ref, ck_ref, cv_ref, s0_ref) = refs[:15]
    y_ref, sp_ref, ss_ref, bq, bk, bv, s_scr = refs[15 + n_prev:]
    hg = pl.program_id(0)
    step = pl.program_id(1)
    is_prompt = step < n_prompt_chunks
    gct_row0 = (2 * heads * hg) % gct_ref.shape[0]

    def body(sample, xq, xk, xv):
        qc = jax.nn.silu(_causal_conv(bq, xq, cwq_ref))
        kc = jax.nn.silu(_causal_conv(bk, xk, cwk_ref))
        vc = jax.nn.silu(_causal_conv(bv, xv, cwv_ref))
        incl, strict = _chunk_masks(sample)
        beta_tile, gc_tile = beta_ref[...], gc_ref[...]
        nw = nw_ref[...]
        nv = 2 * heads
        vcols = [slice(lv * GDN_HEAD, (lv + 1) * GDN_HEAD) for lv in range(nv)]
        qs_ = [_l2norm_rows(qc[:, vcols[hh]]) * (GDN_HEAD ** -0.5) for hh in range(heads)]
        ks_ = [_l2norm_rows(kc[:, vcols[hh]]) for hh in range(heads)]
        kks = [_dot_nt(k, k) for k in ks_]
        qks = [_dot_nt(q, k) for q, k in zip(qs_, ks_)]
        beta_c = [_pick_lane(beta_tile, nv * hg + lv) for lv in range(nv)]
        gc_c = [_pick_lane(gc_tile, nv * hg + lv) for lv in range(nv)]
        gc_r = [gct_ref[pl.ds(gct_row0 + lv, 1), :] for lv in range(nv)]
        decay = [jnp.exp(jnp.where(incl, c - r, NEG_BIG)) for c, r in zip(gc_c, gc_r)]
        a_mats = [jnp.where(strict, beta_c[lv] * kks[lv // 2] * decay[lv], 0.0) for lv in range(nv)]
        t_mats = _unit_lower_inverses(a_mats, sample)
        egc = [jnp.exp(c) for c in gc_c]
        uw = [_dot(t_mats[lv], jnp.concatenate([vc[:, vcols[lv]] * beta_c[lv],
                                                ks_[lv // 2] * (beta_c[lv] * egc[lv])], axis=1))
              for lv in range(nv)]
        u = [x[:, :GDN_HEAD] for x in uw]
        w = [x[:, GDN_HEAD:] for x in uw]
        qk_d = [qks[lv // 2] * decay[lv] for lv in range(nv)]
        q_dec = [qs_[lv // 2] * egc[lv] for lv in range(nv)]
        gl_c = [_seq_last_rows(c) if sample else c[CHUNK_T - 1:CHUNK_T, :] for c in gc_c]
        kd_t = [(ks_[lv // 2] * jnp.exp(gl_c[lv] - gc_c[lv])).T for lv in range(nv)]

        if sample:
            wq = [[_dot(jnp.concatenate([w[lv][s * SEQ_PAD:(s + 1) * SEQ_PAD],
                                         q_dec[lv][s * SEQ_PAD:(s + 1) * SEQ_PAD]], axis=0),
                        s0_ref[s, vcols[lv], :]) for s in range(SEQS_PER_CHUNK)] for lv in range(nv)]
            v_new = [u[lv] - jnp.concatenate([x[:SEQ_PAD] for x in wq[lv]], axis=0) for lv in range(nv)]
            o = [jnp.concatenate([x[SEQ_PAD:] for x in wq[lv]], axis=0) + _dot(qk_d[lv], v_new[lv])
                 for lv in range(nv)]
            seq_of_lane = _iota2((GDN_HEAD, CHUNK_T), 1) // SEQ_PAD
            for lv in range(nv):
                v_new_bf = v_new[lv].astype(BF16)
                for s in range(SEQS_PER_CHUNK):
                    dec = jnp.exp(gl_c[lv][s * SEQ_PAD:s * SEQ_PAD + 1, :])
                    upd = _dot(jnp.where(seq_of_lane == s, kd_t[lv], 0.0), v_new_bf)
                    ss_ref[s, vcols[lv], :] = s0_ref[s, vcols[lv], :] * dec + upd
        else:
            s_old = [s_scr[lv] for lv in range(nv)]
            wq = [_dot(jnp.concatenate([w[lv], q_dec[lv]], axis=0), s_old[lv]) for lv in range(nv)]
            v_new = [u[lv] - wq[lv][:CHUNK_T] for lv in range(nv)]
            o = [wq[lv][CHUNK_T:] + _dot(qk_d[lv], v_new[lv]) for lv in range(nv)]
            for lv in range(nv):
                s_new = s_old[lv] * jnp.exp(gl_c[lv]) + _dot(kd_t[lv], v_new[lv])
                s_scr[lv] = s_new
                sp_ref[vcols[lv], :] = s_new

        for lv in range(nv):
            y = _rmsnorm_rows(o[lv], nw) * jax.nn.silu(z_ref[:, vcols[lv]])
            y_ref[:, vcols[lv]] = y.astype(y_ref.dtype)

    @pl.when(is_prompt)
    def _():
        @pl.when(step % chunks_per_seq == 0)
        def _():
            _zero_conv_history((bq, bk, bv))
            s_scr[...] = jnp.zeros(s_scr.shape, F32)
        body(False, q_ref[...], k_ref[...], v_ref[...])

    @pl.when(jnp.logical_not(is_prompt))
    def _():
        real = _real_row_mask((CHUNK_T, 1), 0)
        _zero_conv_history((bq, bk, bv))
        body(True, jnp.where(real, q_ref[...], cq_ref[...]), jnp.where(real, k_ref[...], ck_ref[...]),
             jnp.where(real, v_ref[...], cv_ref[...]))


class _Dims(NamedTuple):
    n_prompt_seqs: int
    chunks_per_seq: int
    n_sample_seqs: int

    @property
    def n_prompt_chunks(self):
        return self.n_prompt_seqs * self.chunks_per_seq

    @property
    def n_steps(self):
        return self.n_prompt_chunks + self.n_sample_seqs // SEQS_PER_CHUNK

    @property
    def n_rows(self):
        return self.n_steps * CHUNK_T


def _call_mixer(body, name, dims, layer, n_groups, in_specs, args, prev, *, y_width, y_block,
                state_rows, state_block, state_cols, n_layers, scratch):
    n_pc, cps = dims.n_prompt_chunks, dims.chunks_per_seq
    in_specs = list(in_specs) + [pl.BlockSpec(memory_space=pl.ANY)] * len(prev)
    n_in = len(args)
    out_specs = [
        pl.BlockSpec((CHUNK_T, y_block), lambda g, t: (t, g)),
        pl.BlockSpec((None, None, state_block, state_cols),
                     lambda g, t: (layer, jnp.minimum(t // cps, dims.n_prompt_seqs - 1), g, 0)),
        pl.BlockSpec((None, SEQS_PER_CHUNK, state_block, state_cols),
                     lambda g, t: (layer, jnp.maximum(t - n_pc, 0), g, 0))]
    out_shape = [jax.ShapeDtypeStruct((dims.n_rows, y_width), BF16),
                 jax.ShapeDtypeStruct((n_layers, dims.n_prompt_seqs, state_rows, state_cols), F32),
                 jax.ShapeDtypeStruct((n_layers, dims.n_sample_seqs, state_rows, state_cols), F32)]
    return pl.pallas_call(
        functools.partial(body, n_prev=len(prev)),
        grid=(n_groups, dims.n_steps),
        in_specs=in_specs,
        out_specs=out_specs,
        out_shape=out_shape,
        input_output_aliases={n_in + i: 1 + i for i in range(len(prev))},
        scratch_shapes=scratch,
        compiler_params=_cparams(("parallel", "arbitrary")),
        name=name,
    )(*args, *prev)


def _gdn_mixer(proj, gates, conv_w, norm_w, cache8, states, layer, prev, dims):
    beta, gc, gct = gates
    hb, heads = GDN_HEAD, GDN_HEADS_PER_STEP
    qw, vw = heads * hb, 2 * heads * hb
    qkb = GDN_QK_W // qw
    vb = (2 * GDN_QK_W) // vw
    zb = GDN_CONV_DIM // vw
    gct_rows = max(SUBLANES, 2 * heads)
    n_pc = dims.n_prompt_chunks
    sample_blk = lambda t: jnp.maximum(t - n_pc, 0)

    def spec(width, col_off, row_fn=lambda t: t):
        return pl.BlockSpec((CHUNK_T, width), lambda g, t: (row_fn(t), col_off + g))

    row128 = pl.BlockSpec((CHUNK_T, LANES), lambda g, t: (t, 0))
    in_specs = [spec(qw, 0), spec(qw, qkb), spec(vw, vb), spec(vw, zb), row128, row128,
                pl.BlockSpec((gct_rows, CHUNK_T), lambda g, t: ((2 * heads * g) // gct_rows, t)),
                pl.BlockSpec((CONV_W, qw), lambda g, t: (0, g)),
                pl.BlockSpec((CONV_W, qw), lambda g, t: (0, qkb + g)),
                pl.BlockSpec((CONV_W, vw), lambda g, t: (0, vb + g)),
                pl.BlockSpec((1, hb), lambda g, t: (0, 0)),
                spec(qw, 0, sample_blk), spec(qw, qkb, sample_blk), spec(vw, vb, sample_blk),
                pl.BlockSpec((None, SEQS_PER_CHUNK, vw, hb), lambda g, t: (layer, sample_blk(t), g, 0))]
    args = [proj, proj, proj, proj, beta, gc, gct, conv_w, conv_w, conv_w, norm_w.reshape(1, hb),
            cache8, cache8, cache8, states]
    return _call_mixer(
        functools.partial(_gdn_kernel, heads=heads, n_prompt_chunks=dims.n_prompt_chunks,
                          chunks_per_seq=dims.chunks_per_seq),
        "gdn_mixer", dims, layer, GDN_QK_HEADS // heads, in_specs, args, prev,
        y_width=GDN_V_W, y_block=vw, state_rows=GDN_V_HEADS * hb, state_block=vw, state_cols=hb,
        n_layers=states.shape[0],
        scratch=[pltpu.VMEM((CHUNK_T + SUBLANES, qw), F32), pltpu.VMEM((CHUNK_T + SUBLANES, qw), F32),
                 pltpu.VMEM((CHUNK_T + SUBLANES, vw), F32), pltpu.VMEM((2 * heads, hb, hb), F32)])


def _ssd_kernel(*refs, groups, n_prompt_chunks, chunks_per_seq, n_prev):
    (z_ref, x_ref, b_ref, c_ref, dt_ref, acs_ref, acst_ref, cwx_ref, cwb_ref, cwc_ref,
     cbx_ref, cbb_ref, cbc_ref, de_ref, nw_ref, cx_ref, cb_ref, cc_ref, s0_ref) = refs[:19]
    y_ref, sp_ref, ss_ref, bx, bb, bc, s_scr = refs[19 + n_prev:]
    grp = pl.program_id(0)
    step = pl.program_id(1)
    is_prompt = step < n_prompt_chunks
    t = CHUNK_T
    gw, n = SSD_GW, SSD_STATE
    gcols = [slice(gi * gw, (gi + 1) * gw) for gi in range(groups)]
    ncols = [slice(gi * n, (gi + 1) * n) for gi in range(groups)]
    n_heads = groups * SSD_HPG

    def scaled_by_head(s_mat, col):
        eb = jnp.broadcast_to(jnp.exp(acst_ref[:, col:col + 1]), (n_heads, n))
        return jnp.concatenate([s_mat[r * SSD_HEAD:(r + 1) * SSD_HEAD] * eb[r:r + 1, :]
                                for r in range(s_mat.shape[0] // SSD_HEAD)], axis=0)

    def body(sample, xx, xb, xc):
        xg = jax.nn.silu(_causal_conv(bx, xx, cwx_ref) + cbx_ref[...])
        bg = jax.nn.silu(_causal_conv(bb, xb, cwb_ref) + cbb_ref[...])
        cg = jax.nn.silu(_causal_conv(bc, xc, cwc_ref) + cbc_ref[...])

        dt_tile, acs_tile = dt_ref[...], acs_ref[...]
        sel_shape = (LANES, groups * gw)
        sel = _iota2(sel_shape, 0) == grp * n_heads + _iota2(sel_shape, 1) // SSD_HEAD
        parts = jnp.concatenate(_split3(dt_tile) + _split3(acs_tile), axis=0)
        ex = jnp.dot(parts, sel.astype(BF16), preferred_element_type=F32)
        dt_e = ex[0:t] + ex[t:2 * t] + ex[2 * t:3 * t]
        acs_e = ex[3 * t:4 * t] + ex[4 * t:5 * t] + ex[5 * t:6 * t]
        acs_last_e = _seq_last_rows(acs_e) if sample else acs_e[t - 1:t, :]

        xdt = xg * dt_e
        incl, _ = _chunk_masks(sample)
        cbs = [_dot_nt(cg[:, ncols[gi]], bg[:, ncols[gi]]) for gi in range(groups)]
        acs_c = [_pick_lane(acs_tile, grp * n_heads + r) for r in range(n_heads)]
        ms_ = [cbs[r // SSD_HPG] * jnp.exp(jnp.where(incl, acs_c[r] - acst_ref[r:r + 1, :], NEG_BIG))
               for r in range(n_heads)]
        lane_lo = _iota2((t, 2 * SSD_HEAD), 1) < SSD_HEAD
        pairs = []
        for pr in range(n_heads // 2):
            xpair = xdt[:, pr * 2 * SSD_HEAD:(pr + 1) * 2 * SSD_HEAD]
            rhs = jnp.concatenate([jnp.where(lane_lo, xpair, 0.0), jnp.where(lane_lo, 0.0, xpair)],
                                  axis=0)
            pairs.append(_dot(jnp.concatenate([ms_[2 * pr], ms_[2 * pr + 1]], axis=1), rhs))
        y_diag = jnp.concatenate(pairs, axis=1)

        xd_t = (xdt * jnp.exp(acs_last_e - acs_e)).T
        if sample:
            offs = []
            zeros = jnp.zeros((SEQ_PAD, n), F32)
            seq_of_lane = _iota2((groups * gw, t), 1) // SEQ_PAD
            bg_bf = bg.astype(BF16)
            for s in range(SEQS_PER_CHUNK):
                rows = slice(s * SEQ_PAD, (s + 1) * SEQ_PAD)
                s_old = s0_ref[s]
                offs.append(jnp.concatenate(
                    [_dot_nt(jnp.concatenate([cg[rows, ncols[gi]], zeros], axis=0),
                             s_old[gcols[gi]])[:SEQ_PAD] for gi in range(groups)], axis=1))
                xd_s = jnp.where(seq_of_lane == s, xd_t, 0.0)
                upd = jnp.concatenate([_dot(xd_s[gcols[gi]], bg_bf[:, ncols[gi]])
                                       for gi in range(groups)], axis=0)
                ss_ref[s] = scaled_by_head(s_old, (s + 1) * SEQ_PAD - 1) + upd
            y_off = jnp.concatenate(offs, axis=0)
        else:
            s_old = s_scr[...]
            y_off = jnp.concatenate([_dot_nt(cg[:, ncols[gi]], s_old[gcols[gi]])
                                     for gi in range(groups)], axis=1)
            upd = jnp.concatenate([_dot(xd_t[gcols[gi]], bg[:, ncols[gi]]) for gi in range(groups)],
                                  axis=0)
            s_new = scaled_by_head(s_old, t - 1) + upd
            s_scr[...] = s_new
            sp_ref[...] = s_new

        y = y_diag + y_off * jnp.exp(acs_e) + de_ref[...] * xg
        y = y * jax.nn.silu(z_ref[...])
        nw = nw_ref[...]
        for gi in range(groups):
            yg = y[:, gcols[gi]]
            yg = yg * lax.rsqrt(jnp.mean(yg * yg, axis=-1, keepdims=True) + EPS) * nw[:, gcols[gi]]
            y_ref[:, gcols[gi]] = yg.astype(y_ref.dtype)

    @pl.when(is_prompt)
    def _():
        @pl.when(step % chunks_per_seq == 0)
        def _():
            _zero_conv_history((bx, bb, bc))
            s_scr[...] = jnp.zeros(s_scr.shape, F32)
        body(False, x_ref[...], b_ref[...], c_ref[...])

    @pl.when(jnp.logical_not(is_prompt))
    def _():
        real = _real_row_mask((CHUNK_T, 1), 0)
        _zero_conv_history((bx, bb, bc))
        body(True, jnp.where(real, x_ref[...], cx_ref[...]), jnp.where(real, b_ref[...], cb_ref[...]),
             jnp.where(real, c_ref[...], cc_ref[...]))


def _ssd_mixer(proj, gates, conv_w, conv_b, d_skip, norm_w, cache8, states, layer, prev, dims):
    dt, acs, acst = gates
    groups = SSD_GROUPS_PER_STEP
    gw, n = groups * SSD_GW, groups * SSD_STATE
    xb = SSD_D_INNER // gw
    bb = (2 * SSD_D_INNER) // n
    cb = bb + SSD_GN // n
    wbb = SSD_D_INNER // n
    wcb = wbb + SSD_GN // n
    n_pc = dims.n_prompt_chunks
    sample_blk = lambda t: jnp.maximum(t - n_pc, 0)

    def spec(width, col_off, row_fn=lambda t: t):
        return pl.BlockSpec((CHUNK_T, width), lambda g, t: (row_fn(t), col_off + g))

    def wspec(rows, width, col_off):
        return pl.BlockSpec((rows, width), lambda g, t: (0, col_off + g))

    row128 = pl.BlockSpec((CHUNK_T, LANES), lambda g, t: (t, 0))
    in_specs = [spec(gw, 0), spec(gw, xb), spec(n, bb), spec(n, cb), row128, row128,
                pl.BlockSpec((groups * SSD_HPG, CHUNK_T), lambda g, t: (g, t)),
                wspec(CONV_W, gw, 0), wspec(CONV_W, n, wbb), wspec(CONV_W, n, wcb),
                wspec(1, gw, 0), wspec(1, n, wbb), wspec(1, n, wcb),
                wspec(1, gw, 0), wspec(1, gw, 0),
                spec(gw, 0, sample_blk), spec(n, wbb, sample_blk), spec(n, wcb, sample_blk),
                pl.BlockSpec((None, SEQS_PER_CHUNK, gw, SSD_STATE),
                             lambda g, t: (layer, sample_blk(t), g, 0))]
    conv_b2 = conv_b.reshape(1, -1)
    d_e = jnp.repeat(d_skip, SSD_HEAD).reshape(1, SSD_D_INNER)
    args = [proj, proj, proj, proj, dt, acs, acst, conv_w, conv_w, conv_w,
            conv_b2, conv_b2, conv_b2, d_e, norm_w.reshape(1, SSD_D_INNER),
            cache8, cache8, cache8, states]
    return _call_mixer(
        functools.partial(_ssd_kernel, groups=groups, n_prompt_chunks=dims.n_prompt_chunks,
                          chunks_per_seq=dims.chunks_per_seq),
        "ssd_mixer", dims, layer, SSD_GROUPS // groups, in_specs, args, prev,
        y_width=SSD_D_INNER, y_block=gw, state_rows=SSD_HEADS * SSD_HEAD, state_block=gw,
        state_cols=SSD_STATE, n_layers=states.shape[0],
        scratch=[pltpu.VMEM((CHUNK_T + SUBLANES, gw), F32), pltpu.VMEM((CHUNK_T + SUBLANES, n), F32),
                 pltpu.VMEM((CHUNK_T + SUBLANES, n), F32), pltpu.VMEM((gw, SSD_STATE), F32)])


def _pad_sequences(x, n_real):
    pads = [(0, 0)] * (x.ndim - 2) + [(SEQ_PAD - n_real, 0), (0, 0)]
    xp = jnp.pad(x, pads)
    return xp.reshape(x.shape[:-3] + (x.shape[-3] * SEQ_PAD, x.shape[-1]))


def _history_rows(cache):
    n_real = SEQ_PAD // 2
    lo = SEQ_PAD - n_real - (CONV_W - 1)
    xp = jnp.pad(cache, ((0, 0), (lo, n_real), (0, 0)))
    return xp.reshape(cache.shape[0] * SEQ_PAD, cache.shape[2])


def kernel(x_prompt, x_sample, state_gdn, cache_gdn_conv, state_ssd, cache_ssd_conv, p_prompt, p_sample,
           norm_w, gdn_w_in, gdn_conv_w, gdn_A_log, gdn_dt_bias, gdn_norm_w, gdn_w_out,
           ssd_w_in, ssd_conv_w, ssd_conv_b, ssd_A_log, ssd_dt_bias, ssd_D, ssd_norm_w, ssd_w_out,
           ple_w_proj, ple_w_gate, ple_norm_w, final_norm_w):
    bp, lp, d = x_prompt.shape
    bs, ls, _ = x_sample.shape
    depth = norm_w.shape[0]
    assert d == D_MODEL and ls == SEQ_PAD // 2 and lp % CHUNK_T == 0 and bs % SEQS_PER_CHUNK == 0
    mp, ms = bp * lp, bs * SEQ_PAD
    assert mp % ROW_TILE == 0 and ms % ROW_TILE == 0

    h = jnp.concatenate([x_prompt.reshape(mp, d), _pad_sequences(x_sample, ls)], axis=0)
    p = jnp.concatenate([p_prompt.reshape(depth, mp, PLE_DIM), _pad_sequences(p_sample, ls)], axis=1)

    def conv_caches(proj, lo, hi):
        c_p = jnp.stack([proj[(b + 1) * lp - (CONV_W - 1):(b + 1) * lp, lo:hi] for b in range(bp)])
        c_s = proj[mp:, lo:hi].reshape(bs, SEQ_PAD, hi - lo)[:, SEQ_PAD - (CONV_W - 1):, :]
        return c_p, c_s

    dims = _Dims(n_prompt_seqs=bp, chunks_per_seq=lp // CHUNK_T, n_sample_seqs=bs)
    gdn_states = state_gdn.reshape(state_gdn.shape[0], bs, GDN_V_HEADS * GDN_HEAD, GDN_HEAD)
    ssd_states = state_ssd.reshape(state_ssd.shape[0], bs, SSD_HEADS * SSD_HEAD, SSD_STATE)
    gdn_w_in_t = jnp.swapaxes(gdn_w_in, 1, 2)
    ssd_w_in_t = jnp.swapaxes(ssd_w_in, 1, 2)
    w_gate_bf = ple_w_gate.astype(BF16)
    caches = {k: [] for k in ("cg_p", "cs_p", "cg_s", "cs_s")}
    gdn_prev, ssd_prev = (), ()
    hn = _norm_rows(h, norm_w[0], BF16)
    for i in range(depth):
        j = i // 2
        if i % 2 == 0:
            proj = _matmul(hn, gdn_w_in_t, j, GDN_MAIN, "plain", w_nk=True)
            gates = _gdn_gates(hn, gdn_w_in[j, :, GDN_MAIN:], gdn_A_log[j], gdn_dt_bias[j], mp)
            y, *gdn_prev = _gdn_mixer(proj, gates, gdn_conv_w[j], gdn_norm_w[j],
                                      _history_rows(cache_gdn_conv[j]), gdn_states, j, gdn_prev, dims)
            c_p, c_s = conv_caches(proj, 0, GDN_CONV_DIM)
            w_out = gdn_w_out
            keys = ("cg_p", "cg_s")
        else:
            proj = _matmul(hn, ssd_w_in_t, j, SSD_MAIN, "plain", w_nk=True)
            gates = _ssd_gates(hn, ssd_w_in[j, :, SSD_MAIN:], ssd_A_log[j], ssd_dt_bias[j], mp)
            y, *ssd_prev = _ssd_mixer(proj, gates, ssd_conv_w[j], ssd_conv_b[j], ssd_D[j], ssd_norm_w[j],
                                      _history_rows(cache_ssd_conv[j]), ssd_states, j, ssd_prev, dims)
            c_p, c_s = conv_caches(proj, SSD_D_INNER, SSD_MAIN)
            w_out = ssd_w_out
            keys = ("cs_p", "cs_s")
        for key, val in zip(keys, (c_p, c_s)):
            caches[key].append(val)
        h_mid, h_mid_bf = _matmul(y, w_out, j, D_MODEL, "residual", extras=(h,))
        if i + 1 < depth:
            h, hn = _ple_add(h_mid_bf, h_mid, p, w_gate_bf, ple_w_proj, i, ple_norm_w[i], norm_w[i + 1])
        else:
            y_p, y_s = _ple_add(h_mid_bf, h_mid, p, w_gate_bf, ple_w_proj, i, ple_norm_w[i],
                                final_norm_w, mp)

    y_prompt = y_p.reshape(bp, lp, d)
    y_sample = y_s.reshape(bs, SEQ_PAD, d)[:, SEQ_PAD - ls:, :]
    cc = {k: jnp.stack(v) for k, v in caches.items()}
    sg_p, sg_s = (s.reshape(s.shape[:2] + (GDN_V_HEADS, GDN_HEAD, GDN_HEAD)) for s in gdn_prev)
    ss_p, ss_s = (s.reshape(s.shape[:2] + (SSD_HEADS, SSD_HEAD, SSD_STATE)) for s in ssd_prev)
    return (y_prompt, y_sample, sg_p, cc["cg_p"], ss_p, cc["cs_p"], sg_s, cc["cg_s"], ss_s, cc["cs_s"])
```

```python
import functools
from typing import NamedTuple

import jax
import jax.numpy as jnp
import numpy as np
from jax import lax
from jax.experimental import pallas as pl
from jax.experimental.pallas import tpu as pltpu

F32 = jnp.float32
BF16 = jnp.bfloat16

EPS = 1e-6
CONV_W = 4
D_MODEL = 2048
PLE_DIM = 256

GDN_HEAD = 128
GDN_QK_HEADS = 16
GDN_V_HEADS = 32
GDN_QK_W = GDN_QK_HEADS * GDN_HEAD
GDN_V_W = GDN_V_HEADS * GDN_HEAD
GDN_CONV_DIM = 2 * GDN_QK_W + GDN_V_W
GDN_MAIN = GDN_CONV_DIM + GDN_V_W

SSD_D_INNER = 4096
SSD_HEAD = 64
SSD_HEADS = 64
SSD_STATE = 128
SSD_GROUPS = 8
SSD_HPG = SSD_HEADS // SSD_GROUPS
SSD_GN = SSD_GROUPS * SSD_STATE
SSD_CONV_DIM = SSD_D_INNER + 2 * SSD_GN
SSD_MAIN = SSD_D_INNER + SSD_CONV_DIM
SSD_GW = SSD_HPG * SSD_HEAD

LANES = 128
SUBLANES = 8
CHUNK_T = 128
SEQ_PAD = 8
CONV_HIST = 16
SEQS_PER_CHUNK = CHUNK_T // SEQ_PAD
INV_BLOCK = 16
GDN_HEADS_PER_STEP = 4
SSD_GROUPS_PER_STEP = 2
ROW_TILE = 512
MM_TILES = {"plain": (1024, 2048), "residual": (512, 1024)}
VMEM_LIMIT = 56 * 1024 * 1024
NEG_BIG = -1e30


def _cparams(sem):
    return pltpu.CompilerParams(dimension_semantics=sem, vmem_limit_bytes=VMEM_LIMIT)


def _dot(a, b):
    return jnp.dot(a.astype(BF16), b.astype(BF16), preferred_element_type=F32)


def _dot_nt(a, b):
    return lax.dot_general(a.astype(BF16), b.astype(BF16), (((1,), (1,)), ((), ())),
                           preferred_element_type=F32)


def _split3(x):
    x1 = x.astype(BF16)
    r1 = x - x1.astype(F32)
    x2 = r1.astype(BF16)
    r2 = r1 - x2.astype(F32)
    return [x1, x2, r2.astype(BF16)]


def _softplus(x):
    return jnp.maximum(x, 0.0) + jnp.log1p(jnp.exp(-jnp.abs(x)))


def _iota2(shape, dim):
    return lax.broadcasted_iota(jnp.int32, shape, dim)


def _pick_lane(x, idx):
    lane = _iota2(x.shape, 1)
    return jnp.sum(jnp.where(lane == idx, x, 0.0), axis=1, keepdims=True)


def _chunk_masks(sample):
    ii = _iota2((CHUNK_T, CHUNK_T), 0)
    jj = _iota2((CHUNK_T, CHUNK_T), 1)
    incl = ii >= jj
    strict = ii > jj
    if sample:
        same = (ii // SEQ_PAD) == (jj // SEQ_PAD)
        incl = incl & same
        strict = strict & same
    return incl, strict


def _seq_last_rows(x):
    rows, cols = x.shape
    x3 = x.reshape(rows // SEQ_PAD, SEQ_PAD, cols)
    last = jnp.broadcast_to(x3[:, SEQ_PAD - 1:SEQ_PAD, :], x3.shape)
    return last.reshape(rows, cols)


def _unit_lower_inverses(mats, sample):
    ii = _iota2(mats[0].shape, 0)
    jj = _iota2(mats[0].shape, 1)
    eye = (ii == jj).astype(F32)

    def neumann(xs, squarings):
        ts = [eye - x for x in xs]
        ps = xs
        for _ in range(squarings):
            ps = [_dot(p, p) for p in ps]
            ts = [t + _dot(t, p) for t, p in zip(ts, ps)]
        return ts

    if sample:
        assert SEQ_PAD == 8
        return neumann(mats, 2)
    assert INV_BLOCK == 16 and CHUNK_T // INV_BLOCK == 8
    blk = (ii // INV_BLOCK) == (jj // INV_BLOCK)
    ds = [jnp.where(blk, a, 0.0) for a in mats]
    tds = neumann(ds, 3)
    ns = [_dot(td, a - d) for td, a, d in zip(tds, mats, ds)]
    return [_dot(p, td) for p, td in zip(neumann(ns, 2), tds)]


def _conv_shift_matrix():
    r = np.arange((CONV_W - 1) * CHUNK_T)[:, None]
    c = np.arange(CONV_HIST + CHUNK_T)[None, :]
    return jnp.asarray(c == CONV_HIST - (CONV_W - 1) + r % CHUNK_T + r // CHUNK_T, dtype=BF16)


def _causal_conv(hist_ref, x, w_ref, shift_ref):
    t = x.shape[0]
    xb = jnp.concatenate([hist_ref[...], x], axis=0).astype(BF16)
    taps = jnp.dot(shift_ref[...], xb, preferred_element_type=F32)
    w = w_ref[...]
    y = taps[0:t] * w[0:1, :]
    for j in range(1, CONV_W - 1):
        y = y + taps[j * t:(j + 1) * t] * w[j:j + 1, :]
    y = y + x * w[CONV_W - 1:CONV_W, :]
    hist_ref[...] = x[t - CONV_HIST:, :]
    return y


def _zero_conv_history(bufs):
    for b in bufs:
        b[...] = jnp.zeros(b.shape, F32)


def _real_row_mask(shape, dim):
    return (_iota2(shape, dim) % SEQ_PAD) >= (SEQ_PAD // 2)


def _rmsnorm_rows(x, w):
    return x * lax.rsqrt(jnp.mean(x * x, axis=-1, keepdims=True) + EPS) * w


def _norm_kernel(x_ref, w_ref, o_ref):
    o_ref[...] = _rmsnorm_rows(x_ref[...], w_ref[...]).astype(o_ref.dtype)


def _norm_rows(x, w, out_dtype):
    m, d = x.shape
    return pl.pallas_call(
        _norm_kernel,
        grid=(m // ROW_TILE,),
        in_specs=[pl.BlockSpec((ROW_TILE, d), lambda i: (i, 0)),
                  pl.BlockSpec((1, d), lambda i: (0, 0))],
        out_specs=pl.BlockSpec((ROW_TILE, d), lambda i: (i, 0)),
        out_shape=jax.ShapeDtypeStruct((m, d), out_dtype),
        compiler_params=_cparams(("parallel",)),
        name="norm_rows",
    )(x, w.reshape(1, d))


def _ple_kernel(hbf_ref, wg_ref, h_ref, p_ref, wp_ref, pnw_ref, nnw_ref, oa_ref, ob_ref, *,
                n_prompt_tiles):
    gate = jax.nn.sigmoid(jnp.dot(hbf_ref[...], wg_ref[...], preferred_element_type=F32))
    e = _rmsnorm_rows(_dot(p_ref[...], wp_ref[...]), pnw_ref[...])
    h = h_ref[...] + e * gate
    hn = _rmsnorm_rows(h, nnw_ref[...])
    if n_prompt_tiles is None:
        oa_ref[...] = h
        ob_ref[...] = hn.astype(ob_ref.dtype)
    else:
        is_prompt = pl.program_id(0) < n_prompt_tiles

        @pl.when(is_prompt)
        def _():
            oa_ref[...] = hn

        @pl.when(jnp.logical_not(is_prompt))
        def _():
            ob_ref[...] = hn


def _ple_add(h_bf, h, p, w_gate_bf, w_proj, layer, ple_norm_w, next_norm_w, n_prompt_rows=None):
    m = h.shape[0]
    row = lambda width: pl.BlockSpec((ROW_TILE, width), lambda i: (i, 0))
    vec = pl.BlockSpec((1, D_MODEL), lambda i: (0, 0))
    once = dict(pipeline_mode=pl.Buffered(1))
    if n_prompt_rows is None:
        n_pt = None
        out_specs = [row(D_MODEL), row(D_MODEL)]
        out_shape = [jax.ShapeDtypeStruct((m, D_MODEL), F32), jax.ShapeDtypeStruct((m, D_MODEL), BF16)]
    else:
        n_pt = n_prompt_rows // ROW_TILE
        out_specs = [pl.BlockSpec((ROW_TILE, D_MODEL), lambda i: (jnp.minimum(i, n_pt - 1), 0)),
                     pl.BlockSpec((ROW_TILE, D_MODEL), lambda i: (jnp.maximum(i - n_pt, 0), 0))]
        out_shape = [jax.ShapeDtypeStruct((n_prompt_rows, D_MODEL), F32),
                     jax.ShapeDtypeStruct((m - n_prompt_rows, D_MODEL), F32)]
    return pl.pallas_call(
        functools.partial(_ple_kernel, n_prompt_tiles=n_pt),
        grid=(m // ROW_TILE,),
        in_specs=[row(D_MODEL),
                  pl.BlockSpec((None, D_MODEL, D_MODEL), lambda i: (layer, 0, 0), **once),
                  row(D_MODEL),
                  pl.BlockSpec((None, ROW_TILE, PLE_DIM), lambda i: (layer, i, 0)),
                  pl.BlockSpec((None, PLE_DIM, D_MODEL), lambda i: (layer, 0, 0), **once),
                  vec, vec],
        out_specs=out_specs,
        out_shape=out_shape,
        compiler_params=_cparams(("arbitrary",)),
        name="ple_add",
    )(h_bf, w_gate_bf, h, p, w_proj, ple_norm_w.reshape(1, D_MODEL), next_norm_w.reshape(1, D_MODEL))


def _mm_kernel(x_ref, w_ref, *rest, epilogue, w_nk):
    wbf_ref = rest[-1]
    refs = rest[:-1]

    @pl.when(pl.program_id(1) == 0)
    def _():
        wbf_ref[...] = w_ref[...].astype(BF16)

    x = x_ref[...]
    acc = _dot_nt(x, wbf_ref[...]) if w_nk else jnp.dot(x, wbf_ref[...], preferred_element_type=F32)
    if epilogue == "plain":
        (o_ref,) = refs
        o_ref[...] = acc
    elif epilogue == "residual":
        res_ref, o_ref, obf_ref = refs
        h = res_ref[...] + acc
        o_ref[...] = h
        obf_ref[...] = h.astype(BF16)
    else:
        raise ValueError(epilogue)


def _matmul(x, w, layer, n_cols, epilogue, extras=(), w_nk=False):
    m, k = x.shape
    tm, tn = MM_TILES[epilogue]
    while m % tm:
        tm //= 2
    assert n_cols % tn == 0
    tile = pl.BlockSpec((tm, tn), lambda j, i: (i, j))
    w_block = (None, tn, k) if w_nk else (None, k, tn)
    w_index = (lambda j, i: (layer, j, 0)) if w_nk else (lambda j, i: (layer, 0, j))
    w_spec = pl.BlockSpec(w_block, w_index, pipeline_mode=pl.Buffered(1))
    in_specs = [pl.BlockSpec((tm, k), lambda j, i: (i, 0)), w_spec] + [tile] * len(extras)
    out_shape = [jax.ShapeDtypeStruct((m, n_cols), F32)]
    if epilogue == "residual":
        out_shape.append(jax.ShapeDtypeStruct((m, n_cols), BF16))
    out = pl.pallas_call(
        functools.partial(_mm_kernel, epilogue=epilogue, w_nk=w_nk),
        grid=(n_cols // tn, m // tm),
        in_specs=in_specs,
        out_specs=[tile] * len(out_shape),
        out_shape=out_shape,
        scratch_shapes=[pltpu.VMEM(w_block[1:], BF16)],
        compiler_params=_cparams(("parallel", "arbitrary")),
        name="matmul_" + epilogue,
    )(x, w, *extras)
    return out if epilogue == "residual" else out[0]


def _dot_f32(a, b):
    return jnp.dot(a, b, preferred_element_type=F32, precision=lax.Precision.HIGHEST)


def _chunk_cumsums(a, n_heads, is_prompt):
    shift = jnp.where(is_prompt, CHUNK_T.bit_length() - 1, SEQ_PAD.bit_length() - 1)
    ii = _iota2((CHUNK_T, CHUNK_T), 0)
    jj = _iota2((CHUNK_T, CHUNK_T), 1)
    same = lax.shift_right_logical(ii, shift) == lax.shift_right_logical(jj, shift)
    lower = (same & (jj <= ii)).astype(F32)
    upper = (same & (ii <= jj)).astype(F32)
    a_t = a.T[:n_heads]
    blocks = [slice(b * CHUNK_T, (b + 1) * CHUNK_T) for b in range(ROW_TILE // CHUNK_T)]
    sums = jnp.concatenate([_dot_f32(lower, a[b]) for b in blocks], axis=0)
    sums_t = jnp.concatenate([_dot_f32(a_t[:, b], upper) for b in blocks], axis=1)
    return sums, sums_t


def _gdn_gate_kernel(x_ref, wa_ref, wb_ref, alog_ref, dtb_ref, beta_ref, gc_ref, gct_ref, *,
                     n_prompt_tiles):
    x = x_ref[...]
    is_prompt = pl.program_id(0) < n_prompt_tiles
    live_rows = is_prompt | _real_row_mask((ROW_TILE, LANES), 0)
    a = _dot(x, wa_ref[...])
    g = jnp.where(live_rows, -jnp.exp(alog_ref[...]) * _softplus(a + dtb_ref[...]), 0.0)
    beta_ref[...] = jnp.where(live_rows, jax.nn.sigmoid(_dot(x, wb_ref[...])), 0.0)
    gc_ref[...], gct_ref[...] = _chunk_cumsums(g, GDN_V_HEADS, is_prompt)


def _pad_lanes(x):
    return jnp.pad(x, ((0, 0), (0, LANES - x.shape[1])))


def _gdn_gates(hn, w_ab, a_log, dt_bias, n_prompt_rows):
    m = hn.shape[0]
    wa = w_ab[:, :GDN_V_HEADS]
    wb = w_ab[:, GDN_V_HEADS:]
    row = pl.BlockSpec((ROW_TILE, LANES), lambda i: (i, 0))
    full = lambda shape: pl.BlockSpec(shape, lambda i: (0, 0))
    return pl.pallas_call(
        functools.partial(_gdn_gate_kernel, n_prompt_tiles=n_prompt_rows // ROW_TILE),
        grid=(m // ROW_TILE,),
        in_specs=[pl.BlockSpec((ROW_TILE, D_MODEL), lambda i: (i, 0)),
                  full((D_MODEL, LANES)), full((D_MODEL, LANES)), full((1, LANES)), full((1, LANES))],
        out_specs=[row, row, pl.BlockSpec((GDN_V_HEADS, ROW_TILE), lambda i: (0, i))],
        out_shape=[jax.ShapeDtypeStruct((m, LANES), F32)] * 2
        + [jax.ShapeDtypeStruct((GDN_V_HEADS, m), F32)],
        compiler_params=_cparams(("parallel",)),
        name="gdn_gates",
    )(hn, _pad_lanes(wa), _pad_lanes(wb),
      _pad_lanes(a_log.reshape(1, -1)), _pad_lanes(dt_bias.reshape(1, -1)))


def _ssd_gate_kernel(x_ref, w_ref, alog_ref, dtb_ref, dt_ref, acs_ref, acst_ref, *, n_prompt_tiles):
    x = x_ref[...]
    is_prompt = pl.program_id(0) < n_prompt_tiles
    live_rows = is_prompt | _real_row_mask((ROW_TILE, LANES), 0)
    dt = jnp.where(live_rows, _softplus(_dot(x, w_ref[...]) + dtb_ref[...]), 0.0)
    dt_ref[...] = dt
    acs_ref[...], acst_ref[...] = _chunk_cumsums(dt * -jnp.exp(alog_ref[...]), SSD_HEADS, is_prompt)


def _ssd_gates(hn, w, a_log, dt_bias, n_prompt_rows):
    m = hn.shape[0]
    row = pl.BlockSpec((ROW_TILE, LANES), lambda i: (i, 0))
    full = lambda shape: pl.BlockSpec(shape, lambda i: (0, 0))
    return pl.pallas_call(
        functools.partial(_ssd_gate_kernel, n_prompt_tiles=n_prompt_rows // ROW_TILE),
        grid=(m // ROW_TILE,),
        in_specs=[pl.BlockSpec((ROW_TILE, D_MODEL), lambda i: (i, 0)),
                  full((D_MODEL, LANES)), full((1, LANES)), full((1, LANES))],
        out_specs=[row, row, pl.BlockSpec((SSD_HEADS, ROW_TILE), lambda i: (0, i))],
        out_shape=[jax.ShapeDtypeStruct((m, LANES), F32)] * 2
        + [jax.ShapeDtypeStruct((SSD_HEADS, m), F32)],
        compiler_params=_cparams(("parallel",)),
        name="ssd_gates",
    )(hn, _pad_lanes(w), _pad_lanes(a_log.reshape(1, -1)), _pad_lanes(dt_bias.reshape(1, -1)))


def _l2norm_rows(x):
    return x * lax.rsqrt(jnp.sum(x * x, axis=-1, keepdims=True) + EPS)


def _gdn_kernel(*refs, heads, n_prompt_chunks, chunks_per_seq, n_prev):
    (shift_ref, q_ref, k_ref, v_ref, z_ref, beta_ref, gc_ref, gct_ref, cwq_ref, cwk_ref, cwv_ref, nw_ref,
     cq_ref, ck_ref, cv_ref, s0_ref) = refs[:16]
    y_ref, sp_ref, ss_ref, bq, bk, bv, s_scr = refs[16 + n_prev:]
    hg = pl.program_id(0)
    step = pl.program_id(1)
    is_prompt = step < n_prompt_chunks
    gct_row0 = (2 * heads * hg) % gct_ref.shape[0]

    def body(sample, xq, xk, xv):
        qc = jax.nn.silu(_causal_conv(bq, xq, cwq_ref, shift_ref))
        kc = jax.nn.silu(_causal_conv(bk, xk, cwk_ref, shift_ref))
        vc = jax.nn.silu(_causal_conv(bv, xv, cwv_ref, shift_ref))
        incl, strict = _chunk_masks(sample)
        beta_tile, gc_tile = beta_ref[...], gc_ref[...]
        nw = nw_ref[...]
        nv = 2 * heads
        vcols = [slice(lv * GDN_HEAD, (lv + 1) * GDN_HEAD) for lv in range(nv)]
        qs_ = [_l2norm_rows(qc[:, vcols[hh]]) * (GDN_HEAD ** -0.5) for hh in range(heads)]
        ks_ = [_l2norm_rows(kc[:, vcols[hh]]) for hh in range(heads)]
        kks = [_dot_nt(k, k) for k in ks_]
        qks = [_dot_nt(q, k) for q, k in zip(qs_, ks_)]
        beta_c = [_pick_lane(beta_tile, nv * hg + lv) for lv in range(nv)]
        gc_c = [_pick_lane(gc_tile, nv * hg + lv) for lv in range(nv)]
        gc_r = [gct_ref[pl.ds(gct_row0 + lv, 1), :] for lv in range(nv)]
        decay = [jnp.exp(jnp.where(incl, c - r, NEG_BIG)) for c, r in zip(gc_c, gc_r)]
        a_mats = [jnp.where(strict, beta_c[lv] * kks[lv // 2] * decay[lv], 0.0) for lv in range(nv)]
        t_mats = _unit_lower_inverses(a_mats, sample)
        egc = [jnp.exp(c) for c in gc_c]
        uw = [_dot(t_mats[lv], jnp.concatenate([vc[:, vcols[lv]] * beta_c[lv],
                                                ks_[lv // 2] * (beta_c[lv] * egc[lv])], axis=1))
              for lv in range(nv)]
        u = [x[:, :GDN_HEAD] for x in uw]
        w = [x[:, GDN_HEAD:] for x in uw]
        qk_d = [qks[lv // 2] * decay[lv] for lv in range(nv)]
        q_dec = [qs_[lv // 2] * egc[lv] for lv in range(nv)]
        gl_c = [_seq_last_rows(c) if sample else c[CHUNK_T - 1:CHUNK_T, :] for c in gc_c]
        kd_t = [(ks_[lv // 2] * jnp.exp(gl_c[lv] - gc_c[lv])).T for lv in range(nv)]

        if sample:
            wq = [[_dot(jnp.concatenate([w[lv][s * SEQ_PAD:(s + 1) * SEQ_PAD],
                                         q_dec[lv][s * SEQ_PAD:(s + 1) * SEQ_PAD]], axis=0),
                        s0_ref[s, vcols[lv], :]) for s in range(SEQS_PER_CHUNK)] for lv in range(nv)]
            v_new = [u[lv] - jnp.concatenate([x[:SEQ_PAD] for x in wq[lv]], axis=0) for lv in range(nv)]
            o = [jnp.concatenate([x[SEQ_PAD:] for x in wq[lv]], axis=0) + _dot(qk_d[lv], v_new[lv])
                 for lv in range(nv)]
            seq_of_lane = _iota2((GDN_HEAD, CHUNK_T), 1) // SEQ_PAD
            for lv in range(nv):
                v_new_bf = v_new[lv].astype(BF16)
                for s in range(SEQS_PER_CHUNK):
                    dec = jnp.exp(gl_c[lv][s * SEQ_PAD:s * SEQ_PAD + 1, :])
                    upd = _dot(jnp.where(seq_of_lane == s, kd_t[lv], 0.0), v_new_bf)
                    ss_ref[s, vcols[lv], :] = s0_ref[s, vcols[lv], :] * dec + upd
        else:
            s_old = [s_scr[lv] for lv in range(nv)]
            wq = [_dot(jnp.concatenate([w[lv], q_dec[lv]], axis=0), s_old[lv]) for lv in range(nv)]
            v_new = [u[lv] - wq[lv][:CHUNK_T] for lv in range(nv)]
            o = [wq[lv][CHUNK_T:] + _dot(qk_d[lv], v_new[lv]) for lv in range(nv)]
            for lv in range(nv):
                s_new = s_old[lv] * jnp.exp(gl_c[lv]) + _dot(kd_t[lv], v_new[lv])
                s_scr[lv] = s_new
                sp_ref[vcols[lv], :] = s_new

        for lv in range(nv):
            y = _rmsnorm_rows(o[lv], nw) * jax.nn.silu(z_ref[:, vcols[lv]])
            y_ref[:, vcols[lv]] = y.astype(y_ref.dtype)

    @pl.when(is_prompt)
    def _():
        @pl.when(step % chunks_per_seq == 0)
        def _():
            _zero_conv_history((bq, bk, bv))
            s_scr[...] = jnp.zeros(s_scr.shape, F32)
        body(False, q_ref[...], k_ref[...], v_ref[...])

    @pl.when(jnp.logical_not(is_prompt))
    def _():
        real = _real_row_mask((CHUNK_T, 1), 0)
        _zero_conv_history((bq, bk, bv))
        body(True, jnp.where(real, q_ref[...], cq_ref[...]), jnp.where(real, k_ref[...], ck_ref[...]),
             jnp.where(real, v_ref[...], cv_ref[...]))


class _Dims(NamedTuple):
    n_prompt_seqs: int
    chunks_per_seq: int
    n_sample_seqs: int

    @property
    def n_prompt_chunks(self):
        return self.n_prompt_seqs * self.chunks_per_seq

    @property
    def n_steps(self):
        return self.n_prompt_chunks + self.n_sample_seqs // SEQS_PER_CHUNK

    @property
    def n_rows(self):
        return self.n_steps * CHUNK_T


def _call_mixer(body, name, dims, layer, n_groups, in_specs, args, prev, *, y_width, y_block,
                state_rows, state_block, state_cols, n_layers, scratch):
    n_pc, cps = dims.n_prompt_chunks, dims.chunks_per_seq
    shift = _conv_shift_matrix()
    in_specs = ([pl.BlockSpec(shift.shape, lambda g, t: (0, 0))] + list(in_specs)
                + [pl.BlockSpec(memory_space=pl.ANY)] * len(prev))
    args = [shift] + list(args)
    n_in = len(args)
    out_specs = [
        pl.BlockSpec((CHUNK_T, y_block), lambda g, t: (t, g)),
        pl.BlockSpec((None, None, state_block, state_cols),
                     lambda g, t: (layer, jnp.minimum(t // cps, dims.n_prompt_seqs - 1), g, 0)),
        pl.BlockSpec((None, SEQS_PER_CHUNK, state_block, state_cols),
                     lambda g, t: (layer, jnp.maximum(t - n_pc, 0), g, 0))]
    out_shape = [jax.ShapeDtypeStruct((dims.n_rows, y_width), BF16),
                 jax.ShapeDtypeStruct((n_layers, dims.n_prompt_seqs, state_rows, state_cols), F32),
                 jax.ShapeDtypeStruct((n_layers, dims.n_sample_seqs, state_rows, state_cols), F32)]
    return pl.pallas_call(
        functools.partial(body, n_prev=len(prev)),
        grid=(n_groups, dims.n_steps),
        in_specs=in_specs,
        out_specs=out_specs,
        out_shape=out_shape,
        input_output_aliases={n_in + i: 1 + i for i in range(len(prev))},
        scratch_shapes=scratch,
        compiler_params=_cparams(("parallel", "arbitrary")),
        name=name,
    )(*args, *prev)


def _gdn_mixer(proj, gates, conv_w, norm_w, cache8, states, layer, prev, dims):
    beta, gc, gct = gates
    hb, heads = GDN_HEAD, GDN_HEADS_PER_STEP
    qw, vw = heads * hb, 2 * heads * hb
    qkb = GDN_QK_W // qw
    vb = (2 * GDN_QK_W) // vw
    zb = GDN_CONV_DIM // vw
    gct_rows = max(SUBLANES, 2 * heads)
    n_pc = dims.n_prompt_chunks
    sample_blk = lambda t: jnp.maximum(t - n_pc, 0)

    def spec(width, col_off, row_fn=lambda t: t):
        return pl.BlockSpec((CHUNK_T, width), lambda g, t: (row_fn(t), col_off + g))

    row128 = pl.BlockSpec((CHUNK_T, LANES), lambda g, t: (t, 0))
    in_specs = [spec(qw, 0), spec(qw, qkb), spec(vw, vb), spec(vw, zb), row128, row128,
                pl.BlockSpec((gct_rows, CHUNK_T), lambda g, t: ((2 * heads * g) // gct_rows, t)),
                pl.BlockSpec((CONV_W, qw), lambda g, t: (0, g)),
                pl.BlockSpec((CONV_W, qw), lambda g, t: (0, qkb + g)),
                pl.BlockSpec((CONV_W, vw), lambda g, t: (0, vb + g)),
                pl.BlockSpec((1, hb), lambda g, t: (0, 0)),
                spec(qw, 0, sample_blk), spec(qw, qkb, sample_blk), spec(vw, vb, sample_blk),
                pl.BlockSpec((None, SEQS_PER_CHUNK, vw, hb), lambda g, t: (layer, sample_blk(t), g, 0))]
    args = [proj, proj, proj, proj, beta, gc, gct, conv_w, conv_w, conv_w, norm_w.reshape(1, hb),
            cache8, cache8, cache8, states]
    return _call_mixer(
        functools.partial(_gdn_kernel, heads=heads, n_prompt_chunks=dims.n_prompt_chunks,
                          chunks_per_seq=dims.chunks_per_seq),
        "gdn_mixer", dims, layer, GDN_QK_HEADS // heads, in_specs, args, prev,
        y_width=GDN_V_W, y_block=vw, state_rows=GDN_V_HEADS * hb, state_block=vw, state_cols=hb,
        n_layers=states.shape[0],
        scratch=[pltpu.VMEM((CONV_HIST,qw), F32), pltpu.VMEM((CONV_HIST,qw), F32),
                 pltpu.VMEM((CONV_HIST,vw), F32), pltpu.VMEM((2 * heads, hb, hb), F32)])


def _ssd_kernel(*refs, groups, n_prompt_chunks, chunks_per_seq, n_prev):
    (shift_ref, z_ref, x_ref, b_ref, c_ref, dt_ref, acs_ref, acst_ref, cwx_ref, cwb_ref, cwc_ref,
     cbx_ref, cbb_ref, cbc_ref, de_ref, nw_ref, cx_ref, cb_ref, cc_ref, s0_ref) = refs[:20]
    y_ref, sp_ref, ss_ref, bx, bb, bc, s_scr = refs[20 + n_prev:]
    grp = pl.program_id(0)
    step = pl.program_id(1)
    is_prompt = step < n_prompt_chunks
    t = CHUNK_T
    gw, n = SSD_GW, SSD_STATE
    gcols = [slice(gi * gw, (gi + 1) * gw) for gi in range(groups)]
    ncols = [slice(gi * n, (gi + 1) * n) for gi in range(groups)]
    n_heads = groups * SSD_HPG

    def scaled_by_head(s_mat, col):
        eb = jnp.broadcast_to(jnp.exp(acst_ref[:, col:col + 1]), (n_heads, n))
        return jnp.concatenate([s_mat[r * SSD_HEAD:(r + 1) * SSD_HEAD] * eb[r:r + 1, :]
                                for r in range(s_mat.shape[0] // SSD_HEAD)], axis=0)

    def body(sample, xx, xb, xc):
        xg = jax.nn.silu(_causal_conv(bx, xx, cwx_ref, shift_ref) + cbx_ref[...])
        bg = jax.nn.silu(_causal_conv(bb, xb, cwb_ref, shift_ref) + cbb_ref[...])
        cg = jax.nn.silu(_causal_conv(bc, xc, cwc_ref, shift_ref) + cbc_ref[...])

        dt_tile, acs_tile = dt_ref[...], acs_ref[...]
        sel_shape = (LANES, groups * gw)
        sel = _iota2(sel_shape, 0) == grp * n_heads + _iota2(sel_shape, 1) // SSD_HEAD
        parts = jnp.concatenate(_split3(dt_tile) + _split3(acs_tile), axis=0)
        ex = jnp.dot(parts, sel.astype(BF16), preferred_element_type=F32)
        dt_e = ex[0:t] + ex[t:2 * t] + ex[2 * t:3 * t]
        acs_e = ex[3 * t:4 * t] + ex[4 * t:5 * t] + ex[5 * t:6 * t]
        acs_last_e = _seq_last_rows(acs_e) if sample else acs_e[t - 1:t, :]

        xdt = xg * dt_e
        incl, _ = _chunk_masks(sample)
        cbs = [_dot_nt(cg[:, ncols[gi]], bg[:, ncols[gi]]) for gi in range(groups)]
        acs_c = [_pick_lane(acs_tile, grp * n_heads + r) for r in range(n_heads)]
        ms_ = [cbs[r // SSD_HPG] * jnp.exp(jnp.where(incl, acs_c[r] - acst_ref[r:r + 1, :], NEG_BIG))
               for r in range(n_heads)]
        lane_lo = _iota2((t, 2 * SSD_HEAD), 1) < SSD_HEAD
        pairs = []
        for pr in range(n_heads // 2):
            xpair = xdt[:, pr * 2 * SSD_HEAD:(pr + 1) * 2 * SSD_HEAD]
            rhs = jnp.concatenate([jnp.where(lane_lo, xpair, 0.0), jnp.where(lane_lo, 0.0, xpair)],
                                  axis=0)
            pairs.append(_dot(jnp.concatenate([ms_[2 * pr], ms_[2 * pr + 1]], axis=1), rhs))
        y_diag = jnp.concatenate(pairs, axis=1)

        xd_t = (xdt * jnp.exp(acs_last_e - acs_e)).T
        if sample:
            offs = []
            zeros = jnp.zeros((SEQ_PAD, n), F32)
            seq_of_lane = _iota2((groups * gw, t), 1) // SEQ_PAD
            bg_bf = bg.astype(BF16)
            for s in range(SEQS_PER_CHUNK):
                rows = slice(s * SEQ_PAD, (s + 1) * SEQ_PAD)
                s_old = s0_ref[s]
                offs.append(jnp.concatenate(
                    [_dot_nt(jnp.concatenate([cg[rows, ncols[gi]], zeros], axis=0),
                             s_old[gcols[gi]])[:SEQ_PAD] for gi in range(groups)], axis=1))
                xd_s = jnp.where(seq_of_lane == s, xd_t, 0.0)
                upd = jnp.concatenate([_dot(xd_s[gcols[gi]], bg_bf[:, ncols[gi]])
                                       for gi in range(groups)], axis=0)
                ss_ref[s] = scaled_by_head(s_old, (s + 1) * SEQ_PAD - 1) + upd
            y_off = jnp.concatenate(offs, axis=0)
        else:
            s_old = s_scr[...]
            y_off = jnp.concatenate([_dot_nt(cg[:, ncols[gi]], s_old[gcols[gi]])
                                     for gi in range(groups)], axis=1)
            upd = jnp.concatenate([_dot(xd_t[gcols[gi]], bg[:, ncols[gi]]) for gi in range(groups)],
                                  axis=0)
            s_new = scaled_by_head(s_old, t - 1) + upd
            s_scr[...] = s_new
            sp_ref[...] = s_new

        y = y_diag + y_off * jnp.exp(acs_e) + de_ref[...] * xg
        y = y * jax.nn.silu(z_ref[...])
        nw = nw_ref[...]
        for gi in range(groups):
            yg = y[:, gcols[gi]]
            yg = yg * lax.rsqrt(jnp.mean(yg * yg, axis=-1, keepdims=True) + EPS) * nw[:, gcols[gi]]
            y_ref[:, gcols[gi]] = yg.astype(y_ref.dtype)

    @pl.when(is_prompt)
    def _():
        @pl.when(step % chunks_per_seq == 0)
        def _():
            _zero_conv_history((bx, bb, bc))
            s_scr[...] = jnp.zeros(s_scr.shape, F32)
        body(False, x_ref[...], b_ref[...], c_ref[...])

    @pl.when(jnp.logical_not(is_prompt))
    def _():
        real = _real_row_mask((CHUNK_T, 1), 0)
        _zero_conv_history((bx, bb, bc))
        body(True, jnp.where(real, x_ref[...], cx_ref[...]), jnp.where(real, b_ref[...], cb_ref[...]),
             jnp.where(real, c_ref[...], cc_ref[...]))


def _ssd_mixer(proj, gates, conv_w, conv_b, d_skip, norm_w, cache8, states, layer, prev, dims):
    dt, acs, acst = gates
    groups = SSD_GROUPS_PER_STEP
    gw, n = groups * SSD_GW, groups * SSD_STATE
    xb = SSD_D_INNER // gw
    bb = (2 * SSD_D_INNER) // n
    cb = bb + SSD_GN // n
    wbb = SSD_D_INNER // n
    wcb = wbb + SSD_GN // n
    n_pc = dims.n_prompt_chunks
    sample_blk = lambda t: jnp.maximum(t - n_pc, 0)

    def spec(width, col_off, row_fn=lambda t: t):
        return pl.BlockSpec((CHUNK_T, width), lambda g, t: (row_fn(t), col_off + g))

    def wspec(rows, width, col_off):
        return pl.BlockSpec((rows, width), lambda g, t: (0, col_off + g))

    row128 = pl.BlockSpec((CHUNK_T, LANES), lambda g, t: (t, 0))
    in_specs = [spec(gw, 0), spec(gw, xb), spec(n, bb), spec(n, cb), row128, row128,
                pl.BlockSpec((groups * SSD_HPG, CHUNK_T), lambda g, t: (g, t)),
                wspec(CONV_W, gw, 0), wspec(CONV_W, n, wbb), wspec(CONV_W, n, wcb),
                wspec(1, gw, 0), wspec(1, n, wbb), wspec(1, n, wcb),
                wspec(1, gw, 0), wspec(1, gw, 0),
                spec(gw, 0, sample_blk), spec(n, wbb, sample_blk), spec(n, wcb, sample_blk),
                pl.BlockSpec((None, SEQS_PER_CHUNK, gw, SSD_STATE),
                             lambda g, t: (layer, sample_blk(t), g, 0))]
    conv_b2 = conv_b.reshape(1, -1)
    d_e = jnp.repeat(d_skip, SSD_HEAD).reshape(1, SSD_D_INNER)
    args = [proj, proj, proj, proj, dt, acs, acst, conv_w, conv_w, conv_w,
            conv_b2, conv_b2, conv_b2, d_e, norm_w.reshape(1, SSD_D_INNER),
            cache8, cache8, cache8, states]
    return _call_mixer(
        functools.partial(_ssd_kernel, groups=groups, n_prompt_chunks=dims.n_prompt_chunks,
                          chunks_per_seq=dims.chunks_per_seq),
        "ssd_mixer", dims, layer, SSD_GROUPS // groups, in_specs, args, prev,
        y_width=SSD_D_INNER, y_block=gw, state_rows=SSD_HEADS * SSD_HEAD, state_block=gw,
        state_cols=SSD_STATE, n_layers=states.shape[0],
        scratch=[pltpu.VMEM((CONV_HIST,gw), F32), pltpu.VMEM((CONV_HIST,n), F32),
                 pltpu.VMEM((CONV_HIST,n), F32), pltpu.VMEM((gw, SSD_STATE), F32)])


def _pad_sequences(x, n_real):
    pads = [(0, 0)] * (x.ndim - 2) + [(SEQ_PAD - n_real, 0), (0, 0)]
    xp = jnp.pad(x, pads)
    return xp.reshape(x.shape[:-3] + (x.shape[-3] * SEQ_PAD, x.shape[-1]))


def _history_rows(cache):
    n_real = SEQ_PAD // 2
    lo = SEQ_PAD - n_real - (CONV_W - 1)
    xp = jnp.pad(cache, ((0, 0), (lo, n_real), (0, 0)))
    return xp.reshape(cache.shape[0] * SEQ_PAD, cache.shape[2])


def kernel(x_prompt, x_sample, state_gdn, cache_gdn_conv, state_ssd, cache_ssd_conv, p_prompt, p_sample,
           norm_w, gdn_w_in, gdn_conv_w, gdn_A_log, gdn_dt_bias, gdn_norm_w, gdn_w_out,
           ssd_w_in, ssd_conv_w, ssd_conv_b, ssd_A_log, ssd_dt_bias, ssd_D, ssd_norm_w, ssd_w_out,
           ple_w_proj, ple_w_gate, ple_norm_w, final_norm_w):
    bp, lp, d = x_prompt.shape
    bs, ls, _ = x_sample.shape
    depth = norm_w.shape[0]
    assert d == D_MODEL and ls == SEQ_PAD // 2 and lp % CHUNK_T == 0 and bs % SEQS_PER_CHUNK == 0
    mp, ms = bp * lp, bs * SEQ_PAD
    assert mp % ROW_TILE == 0 and ms % ROW_TILE == 0

    h = jnp.concatenate([x_prompt.reshape(mp, d), _pad_sequences(x_sample, ls)], axis=0)
    p = jnp.concatenate([p_prompt.reshape(depth, mp, PLE_DIM), _pad_sequences(p_sample, ls)], axis=1)

    def conv_caches(proj, lo, hi):
        c_p = jnp.stack([proj[(b + 1) * lp - (CONV_W - 1):(b + 1) * lp, lo:hi] for b in range(bp)])
        c_s = proj[mp:, lo:hi].reshape(bs, SEQ_PAD, hi - lo)[:, SEQ_PAD - (CONV_W - 1):, :]
        return c_p, c_s

    dims = _Dims(n_prompt_seqs=bp, chunks_per_seq=lp // CHUNK_T, n_sample_seqs=bs)
    gdn_states = state_gdn.reshape(state_gdn.shape[0], bs, GDN_V_HEADS * GDN_HEAD, GDN_HEAD)
    ssd_states = state_ssd.reshape(state_ssd.shape[0], bs, SSD_HEADS * SSD_HEAD, SSD_STATE)
    gdn_w_in_t = jnp.swapaxes(gdn_w_in, 1, 2)
    ssd_w_in_t = jnp.swapaxes(ssd_w_in, 1, 2)
    w_gate_bf = ple_w_gate.astype(BF16)
    caches = {k: [] for k in ("cg_p", "cs_p", "cg_s", "cs_s")}
    gdn_prev, ssd_prev = (), ()
    hn = _norm_rows(h, norm_w[0], BF16)
    for i in range(depth):
        j = i // 2
        if i % 2 == 0:
            proj = _matmul(hn, gdn_w_in_t, j, GDN_MAIN, "plain", w_nk=True)
            gates = _gdn_gates(hn, gdn_w_in[j, :, GDN_MAIN:], gdn_A_log[j], gdn_dt_bias[j], mp)
            y, *gdn_prev = _gdn_mixer(proj, gates, gdn_conv_w[j], gdn_norm_w[j],
                                      _history_rows(cache_gdn_conv[j]), gdn_states, j, gdn_prev, dims)
            c_p, c_s = conv_caches(proj, 0, GDN_CONV_DIM)
            w_out = gdn_w_out
            keys = ("cg_p", "cg_s")
        else:
            proj = _matmul(hn, ssd_w_in_t, j, SSD_MAIN, "plain", w_nk=True)
            gates = _ssd_gates(hn, ssd_w_in[j, :, SSD_MAIN:], ssd_A_log[j], ssd_dt_bias[j], mp)
            y, *ssd_prev = _ssd_mixer(proj, gates, ssd_conv_w[j], ssd_conv_b[j], ssd_D[j], ssd_norm_w[j],
                                      _history_rows(cache_ssd_conv[j]), ssd_states, j, ssd_prev, dims)
            c_p, c_s = conv_caches(proj, SSD_D_INNER, SSD_MAIN)
            w_out = ssd_w_out
            keys = ("cs_p", "cs_s")
        for key, val in zip(keys, (c_p, c_s)):
            caches[key].append(val)
        h_mid, h_mid_bf = _matmul(y, w_out, j, D_MODEL, "residual", extras=(h,))
        if i + 1 < depth:
            h, hn = _ple_add(h_mid_bf, h_mid, p, w_gate_bf, ple_w_proj, i, ple_norm_w[i], norm_w[i + 1])
        else:
            y_p, y_s = _ple_add(h_mid_bf, h_mid, p, w_gate_bf, ple_w_proj, i, ple_norm_w[i],
                                final_norm_w, mp)

    y_prompt = y_p.reshape(bp, lp, d)
    y_sample = y_s.reshape(bs, SEQ_PAD, d)[:, SEQ_PAD - ls:, :]
    cc = {k: jnp.stack(v) for k, v in caches.items()}
    sg_p, sg_s = (s.reshape(s.shape[:2] + (GDN_V_HEADS, GDN_HEAD, GDN_HEAD)) for s in gdn_prev)
    ss_p, ss_s = (s.reshape(s.shape[:2] + (SSD_HEADS, SSD_HEAD, SSD_STATE)) for s in ssd_prev)
    return (y_prompt, y_sample, sg_p, cc["cg_p"], ss_p, cc["cs_p"], sg_s, cc["cg_s"], ss_s, cc["cs_s"])
```

```python
import functools
from typing import NamedTuple

import jax
import jax.numpy as jnp
import numpy as np
from jax import lax
from jax.experimental import pallas as pl
from jax.experimental.pallas import tpu as pltpu

F32 = jnp.float32
BF16 = jnp.bfloat16

EPS = 1e-6
CONV_W = 4
D_MODEL = 2048
PLE_DIM = 256

GDN_HEAD = 128
GDN_QK_HEADS = 16
GDN_V_HEADS = 32
GDN_QK_W = GDN_QK_HEADS * GDN_HEAD
GDN_V_W = GDN_V_HEADS * GDN_HEAD
GDN_CONV_DIM = 2 * GDN_QK_W + GDN_V_W
GDN_MAIN = GDN_CONV_DIM + GDN_V_W

SSD_D_INNER = 4096
SSD_HEAD = 64
SSD_HEADS = 64
SSD_STATE = 128
SSD_GROUPS = 8
SSD_HPG = SSD_HEADS // SSD_GROUPS
SSD_GN = SSD_GROUPS * SSD_STATE
SSD_CONV_DIM = SSD_D_INNER + 2 * SSD_GN
SSD_MAIN = SSD_D_INNER + SSD_CONV_DIM
SSD_GW = SSD_HPG * SSD_HEAD

LANES = 128
SUBLANES = 8
CHUNK_T = 128
SEQ_PAD = 8
CONV_HIST = 16
SEQS_PER_CHUNK = CHUNK_T // SEQ_PAD
INV_BLOCK = 16
GDN_HEADS_PER_STEP = 4
GDN_HEADS_PER_PROMPT_STEP = 8
SSD_GROUPS_PER_STEP = 2
ROW_TILE = 512
MM_TILES = {"plain": (1024, 2048), "residual": (512, 1024)}
VMEM_LIMIT = 56 * 1024 * 1024
NEG_BIG = -1e30


def _cparams(sem):
    return pltpu.CompilerParams(dimension_semantics=sem, vmem_limit_bytes=VMEM_LIMIT)


def _dot(a, b):
    return jnp.dot(a.astype(BF16), b.astype(BF16), preferred_element_type=F32)


def _dot_nt(a, b):
    return lax.dot_general(a.astype(BF16), b.astype(BF16), (((1,), (1,)), ((), ())),
                           preferred_element_type=F32)


def _split3(x):
    x1 = x.astype(BF16)
    r1 = x - x1.astype(F32)
    x2 = r1.astype(BF16)
    r2 = r1 - x2.astype(F32)
    return [x1, x2, r2.astype(BF16)]


def _softplus(x):
    return jnp.maximum(x, 0.0) + jnp.log1p(jnp.exp(-jnp.abs(x)))


def _iota2(shape, dim):
    return lax.broadcasted_iota(jnp.int32, shape, dim)


def _pick_lane(x, idx):
    lane = _iota2(x.shape, 1)
    return jnp.sum(jnp.where(lane == idx, x, 0.0), axis=1, keepdims=True)


def _chunk_masks(sample):
    ii = _iota2((CHUNK_T, CHUNK_T), 0)
    jj = _iota2((CHUNK_T, CHUNK_T), 1)
    incl = ii >= jj
    strict = ii > jj
    if sample:
        same = (ii // SEQ_PAD) == (jj // SEQ_PAD)
        incl = incl & same
        strict = strict & same
    return incl, strict


def _seq_last_rows(x):
    rows, cols = x.shape
    x3 = x.reshape(rows // SEQ_PAD, SEQ_PAD, cols)
    last = jnp.broadcast_to(x3[:, SEQ_PAD - 1:SEQ_PAD, :], x3.shape)
    return last.reshape(rows, cols)


def _unit_lower_inverses(mats, sample):
    ii = _iota2(mats[0].shape, 0)
    jj = _iota2(mats[0].shape, 1)
    eye = (ii == jj).astype(F32)

    def neumann(xs, squarings):
        ts = [eye - x for x in xs]
        ps = xs
        for _ in range(squarings):
            ps = [_dot(p, p) for p in ps]
            ts = [t + _dot(t, p) for t, p in zip(ts, ps)]
        return ts

    if sample:
        assert SEQ_PAD == 8
        return neumann(mats, 2)
    assert INV_BLOCK == 16 and CHUNK_T // INV_BLOCK == 8
    blk = (ii // INV_BLOCK) == (jj // INV_BLOCK)
    ds = [jnp.where(blk, a, 0.0) for a in mats]
    tds = neumann(ds, 3)
    ns = [_dot(td, a - d) for td, a, d in zip(tds, mats, ds)]
    return [_dot(p, td) for p, td in zip(neumann(ns, 2), tds)]


def _conv_shift_matrix():
    r = np.arange((CONV_W - 1) * CHUNK_T)[:, None]
    c = np.arange(CONV_HIST + CHUNK_T)[None, :]
    return jnp.asarray(c == CONV_HIST - (CONV_W - 1) + r % CHUNK_T + r // CHUNK_T, dtype=BF16)


def _causal_conv(hist_ref, x, w_ref, shift_ref):
    t = x.shape[0]
    xb = jnp.concatenate([hist_ref[...], x], axis=0).astype(BF16)
    taps = jnp.dot(shift_ref[...], xb, preferred_element_type=F32)
    w = w_ref[...]
    y = taps[0:t] * w[0:1, :]
    for j in range(1, CONV_W - 1):
        y = y + taps[j * t:(j + 1) * t] * w[j:j + 1, :]
    y = y + x * w[CONV_W - 1:CONV_W, :]
    hist_ref[...] = x[t - CONV_HIST:, :]
    return y


def _zero_conv_history(bufs):
    for b in bufs:
        b[...] = jnp.zeros(b.shape, F32)


def _real_row_mask(shape, dim):
    return (_iota2(shape, dim) % SEQ_PAD) >= (SEQ_PAD // 2)


def _rmsnorm_rows(x, w):
    return x * lax.rsqrt(jnp.mean(x * x, axis=-1, keepdims=True) + EPS) * w


def _norm_kernel(x_ref, w_ref, o_ref):
    o_ref[...] = _rmsnorm_rows(x_ref[...], w_ref[...]).astype(o_ref.dtype)


def _norm_rows(x, w, out_dtype):
    m, d = x.shape
    return pl.pallas_call(
        _norm_kernel,
        grid=(m // ROW_TILE,),
        in_specs=[pl.BlockSpec((ROW_TILE, d), lambda i: (i, 0)),
                  pl.BlockSpec((1, d), lambda i: (0, 0))],
        out_specs=pl.BlockSpec((ROW_TILE, d), lambda i: (i, 0)),
        out_shape=jax.ShapeDtypeStruct((m, d), out_dtype),
        compiler_params=_cparams(("parallel",)),
        name="norm_rows",
    )(x, w.reshape(1, d))


def _ple_kernel(hbf_ref, wg_ref, h_ref, p_ref, wp_ref, pnw_ref, nnw_ref, oa_ref, ob_ref, *,
                n_prompt_tiles):
    gate = jax.nn.sigmoid(jnp.dot(hbf_ref[...], wg_ref[...], preferred_element_type=F32))
    e = _rmsnorm_rows(_dot(p_ref[...], wp_ref[...]), pnw_ref[...])
    h = h_ref[...] + e * gate
    hn = _rmsnorm_rows(h, nnw_ref[...])
    if n_prompt_tiles is None:
        oa_ref[...] = h
        ob_ref[...] = hn.astype(ob_ref.dtype)
    else:
        is_prompt = pl.program_id(0) < n_prompt_tiles

        @pl.when(is_prompt)
        def _():
            oa_ref[...] = hn

        @pl.when(jnp.logical_not(is_prompt))
        def _():
            ob_ref[...] = hn


def _ple_add(h_bf, h, p, w_gate_bf, w_proj, layer, ple_norm_w, next_norm_w, n_prompt_rows=None):
    m = h.shape[0]
    row = lambda width: pl.BlockSpec((ROW_TILE, width), lambda i: (i, 0))
    vec = pl.BlockSpec((1, D_MODEL), lambda i: (0, 0))
    once = dict(pipeline_mode=pl.Buffered(1))
    if n_prompt_rows is None:
        n_pt = None
        out_specs = [row(D_MODEL), row(D_MODEL)]
        out_shape = [jax.ShapeDtypeStruct((m, D_MODEL), F32), jax.ShapeDtypeStruct((m, D_MODEL), BF16)]
    else:
        n_pt = n_prompt_rows // ROW_TILE
        out_specs = [pl.BlockSpec((ROW_TILE, D_MODEL), lambda i: (jnp.minimum(i, n_pt - 1), 0)),
                     pl.BlockSpec((ROW_TILE, D_MODEL), lambda i: (jnp.maximum(i - n_pt, 0), 0))]
        out_shape = [jax.ShapeDtypeStruct((n_prompt_rows, D_MODEL), F32),
                     jax.ShapeDtypeStruct((m - n_prompt_rows, D_MODEL), F32)]
    return pl.pallas_call(
        functools.partial(_ple_kernel, n_prompt_tiles=n_pt),
        grid=(m // ROW_TILE,),
        in_specs=[row(D_MODEL),
                  pl.BlockSpec((None, D_MODEL, D_MODEL), lambda i: (layer, 0, 0), **once),
                  row(D_MODEL),
                  pl.BlockSpec((None, ROW_TILE, PLE_DIM), lambda i: (layer, i, 0)),
                  pl.BlockSpec((None, PLE_DIM, D_MODEL), lambda i: (layer, 0, 0), **once),
                  vec, vec],
        out_specs=out_specs,
        out_shape=out_shape,
        compiler_params=_cparams(("arbitrary",)),
        name="ple_add",
    )(h_bf, w_gate_bf, h, p, w_proj, ple_norm_w.reshape(1, D_MODEL), next_norm_w.reshape(1, D_MODEL))


def _mm_kernel(x_ref, w_ref, *rest, epilogue, w_nk):
    wbf_ref = rest[-1]
    refs = rest[:-1]

    @pl.when(pl.program_id(1) == 0)
    def _():
        wbf_ref[...] = w_ref[...].astype(BF16)

    x = x_ref[...]
    acc = _dot_nt(x, wbf_ref[...]) if w_nk else jnp.dot(x, wbf_ref[...], preferred_element_type=F32)
    if epilogue == "plain":
        (o_ref,) = refs
        o_ref[...] = acc
    elif epilogue == "residual":
        res_ref, o_ref, obf_ref = refs
        h = res_ref[...] + acc
        o_ref[...] = h
        obf_ref[...] = h.astype(BF16)
    else:
        raise ValueError(epilogue)


def _matmul(x, w, layer, n_cols, epilogue, extras=(), w_nk=False):
    m, k = x.shape
    tm, tn = MM_TILES[epilogue]
    while m % tm:
        tm //= 2
    assert n_cols % tn == 0
    tile = pl.BlockSpec((tm, tn), lambda j, i: (i, j))
    w_block = (None, tn, k) if w_nk else (None, k, tn)
    w_index = (lambda j, i: (layer, j, 0)) if w_nk else (lambda j, i: (layer, 0, j))
    w_spec = pl.BlockSpec(w_block, w_index, pipeline_mode=pl.Buffered(1))
    in_specs = [pl.BlockSpec((tm, k), lambda j, i: (i, 0)), w_spec] + [tile] * len(extras)
    out_shape = [jax.ShapeDtypeStruct((m, n_cols), F32)]
    if epilogue == "residual":
        out_shape.append(jax.ShapeDtypeStruct((m, n_cols), BF16))
    out = pl.pallas_call(
        functools.partial(_mm_kernel, epilogue=epilogue, w_nk=w_nk),
        grid=(n_cols // tn, m // tm),
        in_specs=in_specs,
        out_specs=[tile] * len(out_shape),
        out_shape=out_shape,
        scratch_shapes=[pltpu.VMEM(w_block[1:], BF16)],
        compiler_params=_cparams(("parallel", "arbitrary")),
        name="matmul_" + epilogue,
    )(x, w, *extras)
    return out if epilogue == "residual" else out[0]


def _dot_f32(a, b):
    return jnp.dot(a, b, preferred_element_type=F32, precision=lax.Precision.HIGHEST)


def _chunk_cumsums(a, n_heads, is_prompt):
    shift = jnp.where(is_prompt, CHUNK_T.bit_length() - 1, SEQ_PAD.bit_length() - 1)
    ii = _iota2((CHUNK_T, CHUNK_T), 0)
    jj = _iota2((CHUNK_T, CHUNK_T), 1)
    same = lax.shift_right_logical(ii, shift) == lax.shift_right_logical(jj, shift)
    lower = (same & (jj <= ii)).astype(F32)
    upper = (same & (ii <= jj)).astype(F32)
    a_t = a.T[:n_heads]
    blocks = [slice(b * CHUNK_T, (b + 1) * CHUNK_T) for b in range(ROW_TILE // CHUNK_T)]
    sums = jnp.concatenate([_dot_f32(lower, a[b]) for b in blocks], axis=0)
    sums_t = jnp.concatenate([_dot_f32(a_t[:, b], upper) for b in blocks], axis=1)
    return sums, sums_t


def _gdn_gate_kernel(x_ref, wa_ref, wb_ref, alog_ref, dtb_ref, beta_ref, gc_ref, gct_ref, *,
                     n_prompt_tiles):
    x = x_ref[...]
    is_prompt = pl.program_id(0) < n_prompt_tiles
    live_rows = is_prompt | _real_row_mask((ROW_TILE, LANES), 0)
    a = _dot(x, wa_ref[...])
    g = jnp.where(live_rows, -jnp.exp(alog_ref[...]) * _softplus(a + dtb_ref[...]), 0.0)
    beta_ref[...] = jnp.where(live_rows, jax.nn.sigmoid(_dot(x, wb_ref[...])), 0.0)
    gc_ref[...], gct_ref[...] = _chunk_cumsums(g, GDN_V_HEADS, is_prompt)


def _pad_lanes(x):
    return jnp.pad(x, ((0, 0), (0, LANES - x.shape[1])))


def _gdn_gates(hn, w_ab, a_log, dt_bias, n_prompt_rows):
    m = hn.shape[0]
    wa = w_ab[:, :GDN_V_HEADS]
    wb = w_ab[:, GDN_V_HEADS:]
    row = pl.BlockSpec((ROW_TILE, LANES), lambda i: (i, 0))
    full = lambda shape: pl.BlockSpec(shape, lambda i: (0, 0))
    return pl.pallas_call(
        functools.partial(_gdn_gate_kernel, n_prompt_tiles=n_prompt_rows // ROW_TILE),
        grid=(m // ROW_TILE,),
        in_specs=[pl.BlockSpec((ROW_TILE, D_MODEL), lambda i: (i, 0)),
                  full((D_MODEL, LANES)), full((D_MODEL, LANES)), full((1, LANES)), full((1, LANES))],
        out_specs=[row, row, pl.BlockSpec((GDN_V_HEADS, ROW_TILE), lambda i: (0, i))],
        out_shape=[jax.ShapeDtypeStruct((m, LANES), F32)] * 2
        + [jax.ShapeDtypeStruct((GDN_V_HEADS, m), F32)],
        compiler_params=_cparams(("parallel",)),
        name="gdn_gates",
    )(hn, _pad_lanes(wa), _pad_lanes(wb),
      _pad_lanes(a_log.reshape(1, -1)), _pad_lanes(dt_bias.reshape(1, -1)))


def _ssd_gate_kernel(x_ref, w_ref, alog_ref, dtb_ref, dt_ref, acs_ref, acst_ref, *, n_prompt_tiles):
    x = x_ref[...]
    is_prompt = pl.program_id(0) < n_prompt_tiles
    live_rows = is_prompt | _real_row_mask((ROW_TILE, LANES), 0)
    dt = jnp.where(live_rows, _softplus(_dot(x, w_ref[...]) + dtb_ref[...]), 0.0)
    dt_ref[...] = dt
    acs_ref[...], acst_ref[...] = _chunk_cumsums(dt * -jnp.exp(alog_ref[...]), SSD_HEADS, is_prompt)


def _ssd_gates(hn, w, a_log, dt_bias, n_prompt_rows):
    m = hn.shape[0]
    row = pl.BlockSpec((ROW_TILE, LANES), lambda i: (i, 0))
    full = lambda shape: pl.BlockSpec(shape, lambda i: (0, 0))
    return pl.pallas_call(
        functools.partial(_ssd_gate_kernel, n_prompt_tiles=n_prompt_rows // ROW_TILE),
        grid=(m // ROW_TILE,),
        in_specs=[pl.BlockSpec((ROW_TILE, D_MODEL), lambda i: (i, 0)),
                  full((D_MODEL, LANES)), full((1, LANES)), full((1, LANES))],
        out_specs=[row, row, pl.BlockSpec((SSD_HEADS, ROW_TILE), lambda i: (0, i))],
        out_shape=[jax.ShapeDtypeStruct((m, LANES), F32)] * 2
        + [jax.ShapeDtypeStruct((SSD_HEADS, m), F32)],
        compiler_params=_cparams(("parallel",)),
        name="ssd_gates",
    )(hn, _pad_lanes(w), _pad_lanes(a_log.reshape(1, -1)), _pad_lanes(dt_bias.reshape(1, -1)))


def _l2norm_rows(x):
    return x * lax.rsqrt(jnp.sum(x * x, axis=-1, keepdims=True) + EPS)


def _gdn_kernel(*refs, heads, sample, chunks_per_seq, n_prev):
    (shift_ref, q_ref, k_ref, v_ref, z_ref, beta_ref, gc_ref, gct_ref, cwq_ref, cwk_ref, cwv_ref,
     nw_ref) = refs[:12]
    n_in = 12
    if sample:
        cq_ref, ck_ref, cv_ref, s0_ref = refs[12:16]
        n_in = 16
    y_ref, st_ref, bq, bk, bv, s_scr = refs[n_in + n_prev:]
    ss_ref = sp_ref = st_ref
    hg = pl.program_id(0)
    gct_row0 = (2 * heads * hg) % gct_ref.shape[0]

    def body(xq, xk, xv):
        qc = jax.nn.silu(_causal_conv(bq, xq, cwq_ref, shift_ref))
        kc = jax.nn.silu(_causal_conv(bk, xk, cwk_ref, shift_ref))
        vc = jax.nn.silu(_causal_conv(bv, xv, cwv_ref, shift_ref))
        incl, strict = _chunk_masks(sample)
        beta_tile, gc_tile = beta_ref[...], gc_ref[...]
        nw = nw_ref[...]
        nv = 2 * heads
        vcols = [slice(lv * GDN_HEAD, (lv + 1) * GDN_HEAD) for lv in range(nv)]
        qs_ = [_l2norm_rows(qc[:, vcols[hh]]) * (GDN_HEAD ** -0.5) for hh in range(heads)]
        ks_ = [_l2norm_rows(kc[:, vcols[hh]]) for hh in range(heads)]
        kks = [_dot_nt(k, k) for k in ks_]
        qks = [_dot_nt(q, k) for q, k in zip(qs_, ks_)]
        beta_c = [_pick_lane(beta_tile, nv * hg + lv) for lv in range(nv)]
        gc_c = [_pick_lane(gc_tile, nv * hg + lv) for lv in range(nv)]
        gc_r = [gct_ref[pl.ds(gct_row0 + lv, 1), :] for lv in range(nv)]
        decay = [jnp.exp(jnp.where(incl, c - r, NEG_BIG)) for c, r in zip(gc_c, gc_r)]
        a_mats = [jnp.where(strict, beta_c[lv] * kks[lv // 2] * decay[lv], 0.0) for lv in range(nv)]
        t_mats = _unit_lower_inverses(a_mats, sample)
        egc = [jnp.exp(c) for c in gc_c]
        uw = [_dot(t_mats[lv], jnp.concatenate([vc[:, vcols[lv]] * beta_c[lv],
                                                ks_[lv // 2] * (beta_c[lv] * egc[lv])], axis=1))
              for lv in range(nv)]
        u = [x[:, :GDN_HEAD] for x in uw]
        w = [x[:, GDN_HEAD:] for x in uw]
        qk_d = [qks[lv // 2] * decay[lv] for lv in range(nv)]
        q_dec = [qs_[lv // 2] * egc[lv] for lv in range(nv)]
        gl_c = [_seq_last_rows(c) if sample else c[CHUNK_T - 1:CHUNK_T, :] for c in gc_c]
        kd_t = [(ks_[lv // 2] * jnp.exp(gl_c[lv] - gc_c[lv])).T for lv in range(nv)]

        if sample:
            wq = [[_dot(jnp.concatenate([w[lv][s * SEQ_PAD:(s + 1) * SEQ_PAD],
                                         q_dec[lv][s * SEQ_PAD:(s + 1) * SEQ_PAD]], axis=0),
                        s0_ref[s, vcols[lv], :]) for s in range(SEQS_PER_CHUNK)] for lv in range(nv)]
            v_new = [u[lv] - jnp.concatenate([x[:SEQ_PAD] for x in wq[lv]], axis=0) for lv in range(nv)]
            o = [jnp.concatenate([x[SEQ_PAD:] for x in wq[lv]], axis=0) + _dot(qk_d[lv], v_new[lv])
                 for lv in range(nv)]
            seq_of_lane = _iota2((GDN_HEAD, CHUNK_T), 1) // SEQ_PAD
            for lv in range(nv):
                v_new_bf = v_new[lv].astype(BF16)
                for s in range(SEQS_PER_CHUNK):
                    dec = jnp.exp(gl_c[lv][s * SEQ_PAD:s * SEQ_PAD + 1, :])
                    upd = _dot(jnp.where(seq_of_lane == s, kd_t[lv], 0.0), v_new_bf)
                    ss_ref[s, vcols[lv], :] = s0_ref[s, vcols[lv], :] * dec + upd
        else:
            s_old = [s_scr[lv] for lv in range(nv)]
            wq = [_dot(jnp.concatenate([w[lv], q_dec[lv]], axis=0), s_old[lv]) for lv in range(nv)]
            v_new = [u[lv] - wq[lv][:CHUNK_T] for lv in range(nv)]
            o = [wq[lv][CHUNK_T:] + _dot(qk_d[lv], v_new[lv]) for lv in range(nv)]
            for lv in range(nv):
                s_new = s_old[lv] * jnp.exp(gl_c[lv]) + _dot(kd_t[lv], v_new[lv])
                s_scr[lv] = s_new
                sp_ref[vcols[lv], :] = s_new

        for lv in range(nv):
            y = _rmsnorm_rows(o[lv], nw) * jax.nn.silu(z_ref[:, vcols[lv]])
            y_ref[:, vcols[lv]] = y.astype(y_ref.dtype)

    if sample:
        real = _real_row_mask((CHUNK_T, 1), 0)
        _zero_conv_history((bq, bk, bv))
        body(jnp.where(real, q_ref[...], cq_ref[...]), jnp.where(real, k_ref[...], ck_ref[...]),
             jnp.where(real, v_ref[...], cv_ref[...]))
    else:
        @pl.when(pl.program_id(1) % chunks_per_seq == 0)
        def _():
            _zero_conv_history((bq, bk, bv))
            s_scr[...] = jnp.zeros(s_scr.shape, F32)
        body(q_ref[...], k_ref[...], v_ref[...])


class _Dims(NamedTuple):
    n_prompt_seqs: int
    chunks_per_seq: int
    n_sample_seqs: int

    @property
    def n_prompt_chunks(self):
        return self.n_prompt_seqs * self.chunks_per_seq

    @property
    def n_steps(self):
        return self.n_prompt_chunks + self.n_sample_seqs // SEQS_PER_CHUNK

    @property
    def n_rows(self):
        return self.n_steps * CHUNK_T


def _call_mixer(body, name, dims, layer, n_groups, in_specs, args, prev, *, y_width, y_block,
                state_rows, state_block, state_cols, n_layers, scratch):
    n_pc, cps = dims.n_prompt_chunks, dims.chunks_per_seq
    shift = _conv_shift_matrix()
    in_specs = ([pl.BlockSpec(shift.shape, lambda g, t: (0, 0))] + list(in_specs)
                + [pl.BlockSpec(memory_space=pl.ANY)] * len(prev))
    args = [shift] + list(args)
    n_in = len(args)
    out_specs = [
        pl.BlockSpec((CHUNK_T, y_block), lambda g, t: (t, g)),
        pl.BlockSpec((None, None, state_block, state_cols),
                     lambda g, t: (layer, jnp.minimum(t // cps, dims.n_prompt_seqs - 1), g, 0)),
        pl.BlockSpec((None, SEQS_PER_CHUNK, state_block, state_cols),
                     lambda g, t: (layer, jnp.maximum(t - n_pc, 0), g, 0))]
    out_shape = [jax.ShapeDtypeStruct((dims.n_rows, y_width), BF16),
                 jax.ShapeDtypeStruct((n_layers, dims.n_prompt_seqs, state_rows, state_cols), F32),
                 jax.ShapeDtypeStruct((n_layers, dims.n_sample_seqs, state_rows, state_cols), F32)]
    return pl.pallas_call(
        functools.partial(body, n_prev=len(prev)),
        grid=(n_groups, dims.n_steps),
        in_specs=in_specs,
        out_specs=out_specs,
        out_shape=out_shape,
        input_output_aliases={n_in + i: 1 + i for i in range(len(prev))},
        scratch_shapes=scratch,
        compiler_params=_cparams(("parallel", "arbitrary")),
        name=name,
    )(*args, *prev)


def _gdn_mixer(proj, gates, conv_w, norm_w, cache8, states, layer, prev, dims):
    beta, gc, gct = gates
    hb = GDN_HEAD
    n_pc = dims.n_prompt_chunks
    n_layers, n_seq = states.shape[:2]
    state_rows = GDN_V_HEADS * hb
    shift = _conv_shift_matrix()
    y_shape = jax.ShapeDtypeStruct((dims.n_rows, GDN_V_W), BF16)

    def call(sample, heads, first_step, n_steps, extra_specs, extra_args, state_spec, state_shape,
             aliased):
        qw, vw = heads * hb, 2 * heads * hb
        qkb = GDN_QK_W // qw
        vb = (2 * GDN_QK_W) // vw
        zb = GDN_CONV_DIM // vw
        gct_rows = max(SUBLANES, 2 * heads)

        def spec(width, col_off, row_off=first_step):
            return pl.BlockSpec((CHUNK_T, width), lambda g, t: (row_off + t, col_off + g))

        row128 = pl.BlockSpec((CHUNK_T, LANES), lambda g, t: (first_step + t, 0))
        in_specs = [pl.BlockSpec(shift.shape, lambda g, t: (0, 0)),
                    spec(qw, 0), spec(qw, qkb), spec(vw, vb), spec(vw, zb), row128, row128,
                    pl.BlockSpec((gct_rows, CHUNK_T),
                                 lambda g, t: ((2 * heads * g) // gct_rows, first_step + t)),
                    pl.BlockSpec((CONV_W, qw), lambda g, t: (0, g)),
                    pl.BlockSpec((CONV_W, qw), lambda g, t: (0, qkb + g)),
                    pl.BlockSpec((CONV_W, vw), lambda g, t: (0, vb + g)),
                    pl.BlockSpec((1, hb), lambda g, t: (0, 0))]
        args = [shift, proj, proj, proj, proj, beta, gc, gct, conv_w, conv_w, conv_w,
                norm_w.reshape(1, hb)]
        in_specs += extra_specs(spec, vw) + [pl.BlockSpec(memory_space=pl.ANY)] * len(aliased)
        args += extra_args
        n_in = len(args)
        return pl.pallas_call(
            functools.partial(_gdn_kernel, heads=heads, sample=sample,
                              chunks_per_seq=dims.chunks_per_seq, n_prev=len(aliased)),
            grid=(GDN_QK_HEADS // heads, n_steps),
            in_specs=in_specs,
            out_specs=[pl.BlockSpec((CHUNK_T, vw), lambda g, t: (first_step + t, g)), state_spec(vw)],
            out_shape=[y_shape, state_shape],
            input_output_aliases={n_in + i: out for i, out in enumerate(sorted(aliased))},
            scratch_shapes=[pltpu.VMEM((CONV_HIST, qw), F32), pltpu.VMEM((CONV_HIST, qw), F32),
                            pltpu.VMEM((CONV_HIST, vw), F32), pltpu.VMEM((2 * heads, hb, hb), F32)],
            compiler_params=_cparams(("parallel", "arbitrary")),
            name="gdn_sample" if sample else "gdn_prompt",
        )(*args, *[aliased[k] for k in sorted(aliased)])

    prev_p, prev_s = prev if prev else (None, None)
    cps = dims.chunks_per_seq
    y, sp = call(
        False, GDN_HEADS_PER_PROMPT_STEP, 0, n_pc, lambda spec, vw: [], [],
        lambda vw: pl.BlockSpec((None, None, vw, hb), lambda g, t: (layer, t // cps, g, 0)),
        jax.ShapeDtypeStruct((n_layers, dims.n_prompt_seqs, state_rows, hb), F32),
        {} if prev_p is None else {1: prev_p})
    s_block = lambda vw: pl.BlockSpec((None, SEQS_PER_CHUNK, vw, hb), lambda g, t: (layer, t, g, 0))
    y, ss = call(
        True, GDN_HEADS_PER_STEP, n_pc, n_seq // SEQS_PER_CHUNK,
        lambda spec, vw: [spec(vw // 2, 0, 0), spec(vw // 2, GDN_QK_W // (vw // 2), 0),
                          spec(vw, (2 * GDN_QK_W) // vw, 0), s_block(vw)],
        [cache8, cache8, cache8, states], s_block,
        jax.ShapeDtypeStruct((n_layers, n_seq, state_rows, hb), F32),
        {0: y} if prev_s is None else {0: y, 1: prev_s})
    return y, sp, ss


def _ssd_kernel(*refs, groups, n_prompt_chunks, chunks_per_seq, n_prev):
    (shift_ref, z_ref, x_ref, b_ref, c_ref, dt_ref, acs_ref, acst_ref, cwx_ref, cwb_ref, cwc_ref,
     cbx_ref, cbb_ref, cbc_ref, de_ref, nw_ref, cx_ref, cb_ref, cc_ref, s0_ref) = refs[:20]
    y_ref, sp_ref, ss_ref, bx, bb, bc, s_scr = refs[20 + n_prev:]
    grp = pl.program_id(0)
    step = pl.program_id(1)
    is_prompt = step < n_prompt_chunks
    t = CHUNK_T
    gw, n = SSD_GW, SSD_STATE
    gcols = [slice(gi * gw, (gi + 1) * gw) for gi in range(groups)]
    ncols = [slice(gi * n, (gi + 1) * n) for gi in range(groups)]
    n_heads = groups * SSD_HPG

    def scaled_by_head(s_mat, col):
        eb = jnp.broadcast_to(jnp.exp(acst_ref[:, col:col + 1]), (n_heads, n))
        return jnp.concatenate([s_mat[r * SSD_HEAD:(r + 1) * SSD_HEAD] * eb[r:r + 1, :]
                                for r in range(s_mat.shape[0] // SSD_HEAD)], axis=0)

    def body(sample, xx, xb, xc):
        xg = jax.nn.silu(_causal_conv(bx, xx, cwx_ref, shift_ref) + cbx_ref[...])
        bg = jax.nn.silu(_causal_conv(bb, xb, cwb_ref, shift_ref) + cbb_ref[...])
        cg = jax.nn.silu(_causal_conv(bc, xc, cwc_ref, shift_ref) + cbc_ref[...])

        dt_tile, acs_tile = dt_ref[...], acs_ref[...]
        sel_shape = (LANES, groups * gw)
        sel = _iota2(sel_shape, 0) == grp * n_heads + _iota2(sel_shape, 1) // SSD_HEAD
        parts = jnp.concatenate(_split3(dt_tile) + _split3(acs_tile), axis=0)
        ex = jnp.dot(parts, sel.astype(BF16), preferred_element_type=F32)
        dt_e = ex[0:t] + ex[t:2 * t] + ex[2 * t:3 * t]
        acs_e = ex[3 * t:4 * t] + ex[4 * t:5 * t] + ex[5 * t:6 * t]
        acs_last_e = _seq_last_rows(acs_e) if sample else acs_e[t - 1:t, :]

        xdt = xg * dt_e
        incl, _ = _chunk_masks(sample)
        cbs = [_dot_nt(cg[:, ncols[gi]], bg[:, ncols[gi]]) for gi in range(groups)]
        acs_c = [_pick_lane(acs_tile, grp * n_heads + r) for r in range(n_heads)]
        ms_ = [cbs[r // SSD_HPG] * jnp.exp(jnp.where(incl, acs_c[r] - acst_ref[r:r + 1, :], NEG_BIG))
               for r in range(n_heads)]
        lane_lo = _iota2((t, 2 * SSD_HEAD), 1) < SSD_HEAD
        pairs = []
        for pr in range(n_heads // 2):
            xpair = xdt[:, pr * 2 * SSD_HEAD:(pr + 1) * 2 * SSD_HEAD]
            rhs = jnp.concatenate([jnp.where(lane_lo, xpair, 0.0), jnp.where(lane_lo, 0.0, xpair)],
                                  axis=0)
            pairs.append(_dot(jnp.concatenate([ms_[2 * pr], ms_[2 * pr + 1]], axis=1), rhs))
        y_diag = jnp.concatenate(pairs, axis=1)

        xd_t = (xdt * jnp.exp(acs_last_e - acs_e)).T
        if sample:
            offs = []
            zeros = jnp.zeros((SEQ_PAD, n), F32)
            seq_of_lane = _iota2((groups * gw, t), 1) // SEQ_PAD
            bg_bf = bg.astype(BF16)
            for s in range(SEQS_PER_CHUNK):
                rows = slice(s * SEQ_PAD, (s + 1) * SEQ_PAD)
                s_old = s0_ref[s]
                offs.append(jnp.concatenate(
                    [_dot_nt(jnp.concatenate([cg[rows, ncols[gi]], zeros], axis=0),
                             s_old[gcols[gi]])[:SEQ_PAD] for gi in range(groups)], axis=1))
                xd_s = jnp.where(seq_of_lane == s, xd_t, 0.0)
                upd = jnp.concatenate([_dot(xd_s[gcols[gi]], bg_bf[:, ncols[gi]])
                                       for gi in range(groups)], axis=0)
                ss_ref[s] = scaled_by_head(s_old, (s + 1) * SEQ_PAD - 1) + upd
            y_off = jnp.concatenate(offs, axis=0)
        else:
            s_old = s_scr[...]
            y_off = jnp.concatenate([_dot_nt(cg[:, ncols[gi]], s_old[gcols[gi]])
                                     for gi in range(groups)], axis=1)
            upd = jnp.concatenate([_dot(xd_t[gcols[gi]], bg[:, ncols[gi]]) for gi in range(groups)],
                                  axis=0)
            s_new = scaled_by_head(s_old, t - 1) + upd
            s_scr[...] = s_new
            sp_ref[...] = s_new

        y = y_diag + y_off * jnp.exp(acs_e) + de_ref[...] * xg
        y = y * jax.nn.silu(z_ref[...])
        nw = nw_ref[...]
        for gi in range(groups):
            yg = y[:, gcols[gi]]
            yg = yg * lax.rsqrt(jnp.mean(yg * yg, axis=-1, keepdims=True) + EPS) * nw[:, gcols[gi]]
            y_ref[:, gcols[gi]] = yg.astype(y_ref.dtype)

    @pl.when(is_prompt)
    def _():
        @pl.when(step % chunks_per_seq == 0)
        def _():
            _zero_conv_history((bx, bb, bc))
            s_scr[...] = jnp.zeros(s_scr.shape, F32)
        body(False, x_ref[...], b_ref[...], c_ref[...])

    @pl.when(jnp.logical_not(is_prompt))
    def _():
        real = _real_row_mask((CHUNK_T, 1), 0)
        _zero_conv_history((bx, bb, bc))
        body(True, jnp.where(real, x_ref[...], cx_ref[...]), jnp.where(real, b_ref[...], cb_ref[...]),
             jnp.where(real, c_ref[...], cc_ref[...]))


def _ssd_mixer(proj, gates, conv_w, conv_b, d_skip, norm_w, cache8, states, layer, prev, dims):
    dt, acs, acst = gates
    groups = SSD_GROUPS_PER_STEP
    gw, n = groups * SSD_GW, groups * SSD_STATE
    xb = SSD_D_INNER // gw
    bb = (2 * SSD_D_INNER) // n
    cb = bb + SSD_GN // n
    wbb = SSD_D_INNER // n
    wcb = wbb + SSD_GN // n
    n_pc = dims.n_prompt_chunks
    sample_blk = lambda t: jnp.maximum(t - n_pc, 0)

    def spec(width, col_off, row_fn=lambda t: t):
        return pl.BlockSpec((CHUNK_T, width), lambda g, t: (row_fn(t), col_off + g))

    def wspec(rows, width, col_off):
        return pl.BlockSpec((rows, width), lambda g, t: (0, col_off + g))

    row128 = pl.BlockSpec((CHUNK_T, LANES), lambda g, t: (t, 0))
    in_specs = [spec(gw, 0), spec(gw, xb), spec(n, bb), spec(n, cb), row128, row128,
                pl.BlockSpec((groups * SSD_HPG, CHUNK_T), lambda g, t: (g, t)),
                wspec(CONV_W, gw, 0), wspec(CONV_W, n, wbb), wspec(CONV_W, n, wcb),
                wspec(1, gw, 0), wspec(1, n, wbb), wspec(1, n, wcb),
                wspec(1, gw, 0), wspec(1, gw, 0),
                spec(gw, 0, sample_blk), spec(n, wbb, sample_blk), spec(n, wcb, sample_blk),
                pl.BlockSpec((None, SEQS_PER_CHUNK, gw, SSD_STATE),
                             lambda g, t: (layer, sample_blk(t), g, 0))]
    conv_b2 = conv_b.reshape(1, -1)
    d_e = jnp.repeat(d_skip, SSD_HEAD).reshape(1, SSD_D_INNER)
    args = [proj, proj, proj, proj, dt, acs, acst, conv_w, conv_w, conv_w,
            conv_b2, conv_b2, conv_b2, d_e, norm_w.reshape(1, SSD_D_INNER),
            cache8, cache8, cache8, states]
    return _call_mixer(
        functools.partial(_ssd_kernel, groups=groups, n_prompt_chunks=dims.n_prompt_chunks,
                          chunks_per_seq=dims.chunks_per_seq),
        "ssd_mixer", dims, layer, SSD_GROUPS // groups, in_specs, args, prev,
        y_width=SSD_D_INNER, y_block=gw, state_rows=SSD_HEADS * SSD_HEAD, state_block=gw,
        state_cols=SSD_STATE, n_layers=states.shape[0],
        scratch=[pltpu.VMEM((CONV_HIST,gw), F32), pltpu.VMEM((CONV_HIST,n), F32),
                 pltpu.VMEM((CONV_HIST,n), F32), pltpu.VMEM((gw, SSD_STATE), F32)])


def _pad_sequences(x, n_real):
    pads = [(0, 0)] * (x.ndim - 2) + [(SEQ_PAD - n_real, 0), (0, 0)]
    xp = jnp.pad(x, pads)
    return xp.reshape(x.shape[:-3] + (x.shape[-3] * SEQ_PAD, x.shape[-1]))


def _history_rows(cache):
    n_real = SEQ_PAD // 2
    lo = SEQ_PAD - n_real - (CONV_W - 1)
    xp = jnp.pad(cache, ((0, 0), (lo, n_real), (0, 0)))
    return xp.reshape(cache.shape[0] * SEQ_PAD, cache.shape[2])


def kernel(x_prompt, x_sample, state_gdn, cache_gdn_conv, state_ssd, cache_ssd_conv, p_prompt, p_sample,
           norm_w, gdn_w_in, gdn_conv_w, gdn_A_log, gdn_dt_bias, gdn_norm_w, gdn_w_out,
           ssd_w_in, ssd_conv_w, ssd_conv_b, ssd_A_log, ssd_dt_bias, ssd_D, ssd_norm_w, ssd_w_out,
           ple_w_proj, ple_w_gate, ple_norm_w, final_norm_w):
    bp, lp, d = x_prompt.shape
    bs, ls, _ = x_sample.shape
    depth = norm_w.shape[0]
    assert d == D_MODEL and ls == SEQ_PAD // 2 and lp % CHUNK_T == 0 and bs % SEQS_PER_CHUNK == 0
    mp, ms = bp * lp, bs * SEQ_PAD
    assert mp % ROW_TILE == 0 and ms % ROW_TILE == 0

    h = jnp.concatenate([x_prompt.reshape(mp, d), _pad_sequences(x_sample, ls)], axis=0)
    p = jnp.concatenate([p_prompt.reshape(depth, mp, PLE_DIM), _pad_sequences(p_sample, ls)], axis=1)

    def conv_caches(proj, lo, hi):
        c_p = jnp.stack([proj[(b + 1) * lp - (CONV_W - 1):(b + 1) * lp, lo:hi] for b in range(bp)])
        c_s = proj[mp:, lo:hi].reshape(bs, SEQ_PAD, hi - lo)[:, SEQ_PAD - (CONV_W - 1):, :]
        return c_p, c_s

    dims = _Dims(n_prompt_seqs=bp, chunks_per_seq=lp // CHUNK_T, n_sample_seqs=bs)
    gdn_states = state_gdn.reshape(state_gdn.shape[0], bs, GDN_V_HEADS * GDN_HEAD, GDN_HEAD)
    ssd_states = state_ssd.reshape(state_ssd.shape[0], bs, SSD_HEADS * SSD_HEAD, SSD_STATE)
    gdn_w_in_t = jnp.swapaxes(gdn_w_in, 1, 2)
    ssd_w_in_t = jnp.swapaxes(ssd_w_in, 1, 2)
    w_gate_bf = ple_w_gate.astype(BF16)
    caches = {k: [] for k in ("cg_p", "cs_p", "cg_s", "cs_s")}
    gdn_prev, ssd_prev = (), ()
    hn = _norm_rows(h, norm_w[0], BF16)
    for i in range(depth):
        j = i // 2
        if i % 2 == 0:
            proj = _matmul(hn, gdn_w_in_t, j, GDN_MAIN, "plain", w_nk=True)
            gates = _gdn_gates(hn, gdn_w_in[j, :, GDN_MAIN:], gdn_A_log[j], gdn_dt_bias[j], mp)
            y, *gdn_prev = _gdn_mixer(proj, gates, gdn_conv_w[j], gdn_norm_w[j],
                                      _history_rows(cache_gdn_conv[j]), gdn_states, j, gdn_prev, dims)
            c_p, c_s = conv_caches(proj, 0, GDN_CONV_DIM)
            w_out = gdn_w_out
            keys = ("cg_p", "cg_s")
        else:
            proj = _matmul(hn, ssd_w_in_t, j, SSD_MAIN, "plain", w_nk=True)
            gates = _ssd_gates(hn, ssd_w_in[j, :, SSD_MAIN:], ssd_A_log[j], ssd_dt_bias[j], mp)
            y, *ssd_prev = _ssd_mixer(proj, gates, ssd_conv_w[j], ssd_conv_b[j], ssd_D[j], ssd_norm_w[j],
                                      _history_rows(cache_ssd_conv[j]), ssd_states, j, ssd_prev, dims)
            c_p, c_s = conv_caches(proj, SSD_D_INNER, SSD_MAIN)
            w_out = ssd_w_out
            keys = ("cs_p", "cs_s")
        for key, val in zip(keys, (c_p, c_s)):
            caches[key].append(val)
        h_mid, h_mid_bf = _matmul(y, w_out, j, D_MODEL, "residual", extras=(h,))
        if i + 1 < depth:
            h, hn = _ple_add(h_mid_bf, h_mid, p, w_gate_bf, ple_w_proj, i, ple_norm_w[i], norm_w[i + 1])
        else:
            y_p, y_s = _ple_add(h_mid_bf, h_mid, p, w_gate_bf, ple_w_proj, i, ple_norm_w[i],
                                final_norm_w, mp)

    y_prompt = y_p.reshape(bp, lp, d)
    y_sample = y_s.reshape(bs, SEQ_PAD, d)[:, SEQ_PAD - ls:, :]
    cc = {k: jnp.stack(v) for k, v in caches.items()}
    sg_p, sg_s = (s.reshape(s.shape[:2] + (GDN_V_HEADS, GDN_HEAD, GDN_HEAD)) for s in gdn_prev)
    ss_p, ss_s = (s.reshape(s.shape[:2] + (SSD_HEADS, SSD_HEAD, SSD_STATE)) for s in ssd_prev)
    return (y_prompt, y_sample, sg_p, cc["cg_p"], ss_p, cc["cs_p"], sg_s, cc["cg_s"], ss_s, cc["cs_s"])
```

```python
import functools
from typing import NamedTuple

import jax
import jax.numpy as jnp
import numpy as np
from jax import lax
from jax.experimental import pallas as pl
from jax.experimental.pallas import tpu as pltpu

F32 = jnp.float32
BF16 = jnp.bfloat16

EPS = 1e-6
CONV_W = 4
D_MODEL = 2048
PLE_DIM = 256

GDN_HEAD = 128
GDN_QK_HEADS = 16
GDN_V_HEADS = 32
GDN_QK_W = GDN_QK_HEADS * GDN_HEAD
GDN_V_W = GDN_V_HEADS * GDN_HEAD
GDN_CONV_DIM = 2 * GDN_QK_W + GDN_V_W
GDN_MAIN = GDN_CONV_DIM + GDN_V_W

SSD_D_INNER = 4096
SSD_HEAD = 64
SSD_HEADS = 64
SSD_STATE = 128
SSD_GROUPS = 8
SSD_HPG = SSD_HEADS // SSD_GROUPS
SSD_GN = SSD_GROUPS * SSD_STATE
SSD_CONV_DIM = SSD_D_INNER + 2 * SSD_GN
SSD_MAIN = SSD_D_INNER + SSD_CONV_DIM
SSD_GW = SSD_HPG * SSD_HEAD

LANES = 128
SUBLANES = 8
CHUNK_T = 128
SEQ_PAD = 8
CONV_HIST = 16
SEQS_PER_CHUNK = CHUNK_T // SEQ_PAD
INV_BLOCK = 16
GDN_HEADS_PER_STEP = 4
GDN_HEADS_PER_PROMPT_STEP = 8
SSD_GROUPS_PER_STEP = 2
SSD_GROUPS_PER_PROMPT_STEP = 4
ROW_TILE = 512
MM_TILES = {"plain": (1024, 2048), "residual": (512, 1024)}
VMEM_LIMIT = 56 * 1024 * 1024
NEG_BIG = -1e30


def _cparams(sem):
    return pltpu.CompilerParams(dimension_semantics=sem, vmem_limit_bytes=VMEM_LIMIT)


def _dot(a, b):
    return jnp.dot(a.astype(BF16), b.astype(BF16), preferred_element_type=F32)


def _dot_nt(a, b):
    return lax.dot_general(a.astype(BF16), b.astype(BF16), (((1,), (1,)), ((), ())),
                           preferred_element_type=F32)


def _split3(x):
    x1 = x.astype(BF16)
    r1 = x - x1.astype(F32)
    x2 = r1.astype(BF16)
    r2 = r1 - x2.astype(F32)
    return [x1, x2, r2.astype(BF16)]


def _softplus(x):
    return jnp.maximum(x, 0.0) + jnp.log1p(jnp.exp(-jnp.abs(x)))


def _iota2(shape, dim):
    return lax.broadcasted_iota(jnp.int32, shape, dim)


def _pick_lane(x, idx):
    lane = _iota2(x.shape, 1)
    return jnp.sum(jnp.where(lane == idx, x, 0.0), axis=1, keepdims=True)


def _chunk_masks(sample):
    ii = _iota2((CHUNK_T, CHUNK_T), 0)
    jj = _iota2((CHUNK_T, CHUNK_T), 1)
    incl = ii >= jj
    strict = ii > jj
    if sample:
        same = (ii // SEQ_PAD) == (jj // SEQ_PAD)
        incl = incl & same
        strict = strict & same
    return incl, strict


def _seq_last_rows(x):
    rows, cols = x.shape
    x3 = x.reshape(rows // SEQ_PAD, SEQ_PAD, cols)
    last = jnp.broadcast_to(x3[:, SEQ_PAD - 1:SEQ_PAD, :], x3.shape)
    return last.reshape(rows, cols)


def _unit_lower_inverses(mats, sample):
    ii = _iota2(mats[0].shape, 0)
    jj = _iota2(mats[0].shape, 1)
    eye = (ii == jj).astype(F32)

    def neumann(xs, squarings):
        ts = [eye - x for x in xs]
        ps = xs
        for _ in range(squarings):
            ps = [_dot(p, p) for p in ps]
            ts = [t + _dot(t, p) for t, p in zip(ts, ps)]
        return ts

    if sample:
        assert SEQ_PAD == 8
        return neumann(mats, 2)
    assert INV_BLOCK == 16 and CHUNK_T // INV_BLOCK == 8
    blk = (ii // INV_BLOCK) == (jj // INV_BLOCK)
    ds = [jnp.where(blk, a, 0.0) for a in mats]
    tds = neumann(ds, 3)
    ns = [_dot(td, a - d) for td, a, d in zip(tds, mats, ds)]
    return [_dot(p, td) for p, td in zip(neumann(ns, 2), tds)]


def _conv_shift_matrix():
    r = np.arange((CONV_W - 1) * CHUNK_T)[:, None]
    c = np.arange(CONV_HIST + CHUNK_T)[None, :]
    return jnp.asarray(c == CONV_HIST - (CONV_W - 1) + r % CHUNK_T + r // CHUNK_T, dtype=BF16)


def _causal_conv(hist_ref, x, w_ref, shift_ref):
    t = x.shape[0]
    xb = jnp.concatenate([hist_ref[...], x], axis=0).astype(BF16)
    taps = jnp.dot(shift_ref[...], xb, preferred_element_type=F32)
    w = w_ref[...]
    y = taps[0:t] * w[0:1, :]
    for j in range(1, CONV_W - 1):
        y = y + taps[j * t:(j + 1) * t] * w[j:j + 1, :]
    y = y + x * w[CONV_W - 1:CONV_W, :]
    hist_ref[...] = x[t - CONV_HIST:, :]
    return y


def _zero_conv_history(bufs):
    for b in bufs:
        b[...] = jnp.zeros(b.shape, F32)


def _real_row_mask(shape, dim):
    return (_iota2(shape, dim) % SEQ_PAD) >= (SEQ_PAD // 2)


def _rmsnorm_rows(x, w):
    return x * lax.rsqrt(jnp.mean(x * x, axis=-1, keepdims=True) + EPS) * w


def _norm_kernel(x_ref, w_ref, o_ref):
    o_ref[...] = _rmsnorm_rows(x_ref[...], w_ref[...]).astype(o_ref.dtype)


def _norm_rows(x, w, out_dtype):
    m, d = x.shape
    return pl.pallas_call(
        _norm_kernel,
        grid=(m // ROW_TILE,),
        in_specs=[pl.BlockSpec((ROW_TILE, d), lambda i: (i, 0)),
                  pl.BlockSpec((1, d), lambda i: (0, 0))],
        out_specs=pl.BlockSpec((ROW_TILE, d), lambda i: (i, 0)),
        out_shape=jax.ShapeDtypeStruct((m, d), out_dtype),
        compiler_params=_cparams(("parallel",)),
        name="norm_rows",
    )(x, w.reshape(1, d))


def _ple_kernel(hbf_ref, wg_ref, h_ref, p_ref, wp_ref, pnw_ref, nnw_ref, oa_ref, ob_ref, *,
                n_prompt_tiles):
    gate = jax.nn.sigmoid(jnp.dot(hbf_ref[...], wg_ref[...], preferred_element_type=F32))
    e = _rmsnorm_rows(_dot(p_ref[...], wp_ref[...]), pnw_ref[...])
    h = h_ref[...] + e * gate
    hn = _rmsnorm_rows(h, nnw_ref[...])
    if n_prompt_tiles is None:
        oa_ref[...] = h
        ob_ref[...] = hn.astype(ob_ref.dtype)
    else:
        is_prompt = pl.program_id(0) < n_prompt_tiles

        @pl.when(is_prompt)
        def _():
            oa_ref[...] = hn

        @pl.when(jnp.logical_not(is_prompt))
        def _():
            ob_ref[...] = hn


def _ple_add(h_bf, h, p, w_gate_bf, w_proj, layer, ple_norm_w, next_norm_w, n_prompt_rows=None):
    m = h.shape[0]
    row = lambda width: pl.BlockSpec((ROW_TILE, width), lambda i: (i, 0))
    vec = pl.BlockSpec((1, D_MODEL), lambda i: (0, 0))
    once = dict(pipeline_mode=pl.Buffered(1))
    if n_prompt_rows is None:
        n_pt = None
        out_specs = [row(D_MODEL), row(D_MODEL)]
        out_shape = [jax.ShapeDtypeStruct((m, D_MODEL), F32), jax.ShapeDtypeStruct((m, D_MODEL), BF16)]
    else:
        n_pt = n_prompt_rows // ROW_TILE
        out_specs = [pl.BlockSpec((ROW_TILE, D_MODEL), lambda i: (jnp.minimum(i, n_pt - 1), 0)),
                     pl.BlockSpec((ROW_TILE, D_MODEL), lambda i: (jnp.maximum(i - n_pt, 0), 0))]
        out_shape = [jax.ShapeDtypeStruct((n_prompt_rows, D_MODEL), F32),
                     jax.ShapeDtypeStruct((m - n_prompt_rows, D_MODEL), F32)]
    return pl.pallas_call(
        functools.partial(_ple_kernel, n_prompt_tiles=n_pt),
        grid=(m // ROW_TILE,),
        in_specs=[row(D_MODEL),
                  pl.BlockSpec((None, D_MODEL, D_MODEL), lambda i: (layer, 0, 0), **once),
                  row(D_MODEL),
                  pl.BlockSpec((None, ROW_TILE, PLE_DIM), lambda i: (layer, i, 0)),
                  pl.BlockSpec((None, PLE_DIM, D_MODEL), lambda i: (layer, 0, 0), **once),
                  vec, vec],
        out_specs=out_specs,
        out_shape=out_shape,
        compiler_params=_cparams(("arbitrary",)),
        name="ple_add",
    )(h_bf, w_gate_bf, h, p, w_proj, ple_norm_w.reshape(1, D_MODEL), next_norm_w.reshape(1, D_MODEL))


def _mm_kernel(x_ref, w_ref, *rest, epilogue, w_nk):
    wbf_ref = rest[-1]
    refs = rest[:-1]

    @pl.when(pl.program_id(1) == 0)
    def _():
        wbf_ref[...] = w_ref[...].astype(BF16)

    x = x_ref[...]
    acc = _dot_nt(x, wbf_ref[...]) if w_nk else jnp.dot(x, wbf_ref[...], preferred_element_type=F32)
    if epilogue == "plain":
        (o_ref,) = refs
        o_ref[...] = acc
    elif epilogue == "residual":
        res_ref, o_ref, obf_ref = refs
        h = res_ref[...] + acc
        o_ref[...] = h
        obf_ref[...] = h.astype(BF16)
    else:
        raise ValueError(epilogue)


def _matmul(x, w, layer, n_cols, epilogue, extras=(), w_nk=False):
    m, k = x.shape
    tm, tn = MM_TILES[epilogue]
    while m % tm:
        tm //= 2
    assert n_cols % tn == 0
    tile = pl.BlockSpec((tm, tn), lambda j, i: (i, j))
    w_block = (None, tn, k) if w_nk else (None, k, tn)
    w_index = (lambda j, i: (layer, j, 0)) if w_nk else (lambda j, i: (layer, 0, j))
    w_spec = pl.BlockSpec(w_block, w_index, pipeline_mode=pl.Buffered(1))
    in_specs = [pl.BlockSpec((tm, k), lambda j, i: (i, 0)), w_spec] + [tile] * len(extras)
    out_shape = [jax.ShapeDtypeStruct((m, n_cols), F32)]
    if epilogue == "residual":
        out_shape.append(jax.ShapeDtypeStruct((m, n_cols), BF16))
    out = pl.pallas_call(
        functools.partial(_mm_kernel, epilogue=epilogue, w_nk=w_nk),
        grid=(n_cols // tn, m // tm),
        in_specs=in_specs,
        out_specs=[tile] * len(out_shape),
        out_shape=out_shape,
        scratch_shapes=[pltpu.VMEM(w_block[1:], BF16)],
        compiler_params=_cparams(("parallel", "arbitrary")),
        name="matmul_" + epilogue,
    )(x, w, *extras)
    return out if epilogue == "residual" else out[0]


def _dot_f32(a, b):
    return jnp.dot(a, b, preferred_element_type=F32, precision=lax.Precision.HIGHEST)


def _chunk_cumsums(a, n_heads, is_prompt):
    shift = jnp.where(is_prompt, CHUNK_T.bit_length() - 1, SEQ_PAD.bit_length() - 1)
    ii = _iota2((CHUNK_T, CHUNK_T), 0)
    jj = _iota2((CHUNK_T, CHUNK_T), 1)
    same = lax.shift_right_logical(ii, shift) == lax.shift_right_logical(jj, shift)
    lower = (same & (jj <= ii)).astype(F32)
    upper = (same & (ii <= jj)).astype(F32)
    a_t = a.T[:n_heads]
    blocks = [slice(b * CHUNK_T, (b + 1) * CHUNK_T) for b in range(ROW_TILE // CHUNK_T)]
    sums = jnp.concatenate([_dot_f32(lower, a[b]) for b in blocks], axis=0)
    sums_t = jnp.concatenate([_dot_f32(a_t[:, b], upper) for b in blocks], axis=1)
    return sums, sums_t


def _gdn_gate_kernel(x_ref, wa_ref, wb_ref, alog_ref, dtb_ref, beta_ref, gc_ref, gct_ref, *,
                     n_prompt_tiles):
    x = x_ref[...]
    is_prompt = pl.program_id(0) < n_prompt_tiles
    live_rows = is_prompt | _real_row_mask((ROW_TILE, LANES), 0)
    a = _dot(x, wa_ref[...])
    g = jnp.where(live_rows, -jnp.exp(alog_ref[...]) * _softplus(a + dtb_ref[...]), 0.0)
    beta_ref[...] = jnp.where(live_rows, jax.nn.sigmoid(_dot(x, wb_ref[...])), 0.0)
    gc_ref[...], gct_ref[...] = _chunk_cumsums(g, GDN_V_HEADS, is_prompt)


def _pad_lanes(x):
    return jnp.pad(x, ((0, 0), (0, LANES - x.shape[1])))


def _gdn_gates(hn, w_ab, a_log, dt_bias, n_prompt_rows):
    m = hn.shape[0]
    wa = w_ab[:, :GDN_V_HEADS]
    wb = w_ab[:, GDN_V_HEADS:]
    row = pl.BlockSpec((ROW_TILE, LANES), lambda i: (i, 0))
    full = lambda shape: pl.BlockSpec(shape, lambda i: (0, 0))
    return pl.pallas_call(
        functools.partial(_gdn_gate_kernel, n_prompt_tiles=n_prompt_rows // ROW_TILE),
        grid=(m // ROW_TILE,),
        in_specs=[pl.BlockSpec((ROW_TILE, D_MODEL), lambda i: (i, 0)),
                  full((D_MODEL, LANES)), full((D_MODEL, LANES)), full((1, LANES)), full((1, LANES))],
        out_specs=[row, row, pl.BlockSpec((GDN_V_HEADS, ROW_TILE), lambda i: (0, i))],
        out_shape=[jax.ShapeDtypeStruct((m, LANES), F32)] * 2
        + [jax.ShapeDtypeStruct((GDN_V_HEADS, m), F32)],
        compiler_params=_cparams(("parallel",)),
        name="gdn_gates",
    )(hn, _pad_lanes(wa), _pad_lanes(wb),
      _pad_lanes(a_log.reshape(1, -1)), _pad_lanes(dt_bias.reshape(1, -1)))


def _ssd_gate_kernel(x_ref, w_ref, alog_ref, dtb_ref, dt_ref, acs_ref, acst_ref, *, n_prompt_tiles):
    x = x_ref[...]
    is_prompt = pl.program_id(0) < n_prompt_tiles
    live_rows = is_prompt | _real_row_mask((ROW_TILE, LANES), 0)
    dt = jnp.where(live_rows, _softplus(_dot(x, w_ref[...]) + dtb_ref[...]), 0.0)
    dt_ref[...] = dt
    acs_ref[...], acst_ref[...] = _chunk_cumsums(dt * -jnp.exp(alog_ref[...]), SSD_HEADS, is_prompt)


def _ssd_gates(hn, w, a_log, dt_bias, n_prompt_rows):
    m = hn.shape[0]
    row = pl.BlockSpec((ROW_TILE, LANES), lambda i: (i, 0))
    full = lambda shape: pl.BlockSpec(shape, lambda i: (0, 0))
    return pl.pallas_call(
        functools.partial(_ssd_gate_kernel, n_prompt_tiles=n_prompt_rows // ROW_TILE),
        grid=(m // ROW_TILE,),
        in_specs=[pl.BlockSpec((ROW_TILE, D_MODEL), lambda i: (i, 0)),
                  full((D_MODEL, LANES)), full((1, LANES)), full((1, LANES))],
        out_specs=[row, row, pl.BlockSpec((SSD_HEADS, ROW_TILE), lambda i: (0, i))],
        out_shape=[jax.ShapeDtypeStruct((m, LANES), F32)] * 2
        + [jax.ShapeDtypeStruct((SSD_HEADS, m), F32)],
        compiler_params=_cparams(("parallel",)),
        name="ssd_gates",
    )(hn, _pad_lanes(w), _pad_lanes(a_log.reshape(1, -1)), _pad_lanes(dt_bias.reshape(1, -1)))


def _l2norm_rows(x):
    return x * lax.rsqrt(jnp.sum(x * x, axis=-1, keepdims=True) + EPS)


def _gdn_kernel(*refs, heads, sample, chunks_per_seq, n_prev):
    (shift_ref, q_ref, k_ref, v_ref, z_ref, beta_ref, gc_ref, gct_ref, cwq_ref, cwk_ref, cwv_ref,
     nw_ref) = refs[:12]
    n_in = 12
    if sample:
        cq_ref, ck_ref, cv_ref, s0_ref = refs[12:16]
        n_in = 16
    y_ref, st_ref, bq, bk, bv, s_scr = refs[n_in + n_prev:]
    ss_ref = sp_ref = st_ref
    hg = pl.program_id(0)
    gct_row0 = (2 * heads * hg) % gct_ref.shape[0]

    def body(xq, xk, xv):
        qc = jax.nn.silu(_causal_conv(bq, xq, cwq_ref, shift_ref))
        kc = jax.nn.silu(_causal_conv(bk, xk, cwk_ref, shift_ref))
        vc = jax.nn.silu(_causal_conv(bv, xv, cwv_ref, shift_ref))
        incl, strict = _chunk_masks(sample)
        beta_tile, gc_tile = beta_ref[...], gc_ref[...]
        nw = nw_ref[...]
        nv = 2 * heads
        vcols = [slice(lv * GDN_HEAD, (lv + 1) * GDN_HEAD) for lv in range(nv)]
        qs_ = [_l2norm_rows(qc[:, vcols[hh]]) * (GDN_HEAD ** -0.5) for hh in range(heads)]
        ks_ = [_l2norm_rows(kc[:, vcols[hh]]) for hh in range(heads)]
        kks = [_dot_nt(k, k) for k in ks_]
        qks = [_dot_nt(q, k) for q, k in zip(qs_, ks_)]
        beta_c = [_pick_lane(beta_tile, nv * hg + lv) for lv in range(nv)]
        gc_c = [_pick_lane(gc_tile, nv * hg + lv) for lv in range(nv)]
        gc_r = [gct_ref[pl.ds(gct_row0 + lv, 1), :] for lv in range(nv)]
        decay = [jnp.exp(jnp.where(incl, c - r, NEG_BIG)) for c, r in zip(gc_c, gc_r)]
        a_mats = [jnp.where(strict, beta_c[lv] * kks[lv // 2] * decay[lv], 0.0) for lv in range(nv)]
        t_mats = _unit_lower_inverses(a_mats, sample)
        egc = [jnp.exp(c) for c in gc_c]
        uw = [_dot(t_mats[lv], jnp.concatenate([vc[:, vcols[lv]] * beta_c[lv],
                                                ks_[lv // 2] * (beta_c[lv] * egc[lv])], axis=1))
              for lv in range(nv)]
        u = [x[:, :GDN_HEAD] for x in uw]
        w = [x[:, GDN_HEAD:] for x in uw]
        qk_d = [qks[lv // 2] * decay[lv] for lv in range(nv)]
        q_dec = [qs_[lv // 2] * egc[lv] for lv in range(nv)]
        gl_c = [_seq_last_rows(c) if sample else c[CHUNK_T - 1:CHUNK_T, :] for c in gc_c]
        kd_t = [(ks_[lv // 2] * jnp.exp(gl_c[lv] - gc_c[lv])).T for lv in range(nv)]

        if sample:
            wq = [[_dot(jnp.concatenate([w[lv][s * SEQ_PAD:(s + 1) * SEQ_PAD],
                                         q_dec[lv][s * SEQ_PAD:(s + 1) * SEQ_PAD]], axis=0),
                        s0_ref[s, vcols[lv], :]) for s in range(SEQS_PER_CHUNK)] for lv in range(nv)]
            v_new = [u[lv] - jnp.concatenate([x[:SEQ_PAD] for x in wq[lv]], axis=0) for lv in range(nv)]
            o = [jnp.concatenate([x[SEQ_PAD:] for x in wq[lv]], axis=0) + _dot(qk_d[lv], v_new[lv])
                 for lv in range(nv)]
            seq_of_lane = _iota2((GDN_HEAD, CHUNK_T), 1) // SEQ_PAD
            for lv in range(nv):
                v_new_bf = v_new[lv].astype(BF16)
                for s in range(SEQS_PER_CHUNK):
                    dec = jnp.exp(gl_c[lv][s * SEQ_PAD:s * SEQ_PAD + 1, :])
                    upd = _dot(jnp.where(seq_of_lane == s, kd_t[lv], 0.0), v_new_bf)
                    ss_ref[s, vcols[lv], :] = s0_ref[s, vcols[lv], :] * dec + upd
        else:
            s_old = [s_scr[lv] for lv in range(nv)]
            wq = [_dot(jnp.concatenate([w[lv], q_dec[lv]], axis=0), s_old[lv]) for lv in range(nv)]
            v_new = [u[lv] - wq[lv][:CHUNK_T] for lv in range(nv)]
            o = [wq[lv][CHUNK_T:] + _dot(qk_d[lv], v_new[lv]) for lv in range(nv)]
            for lv in range(nv):
                s_new = s_old[lv] * jnp.exp(gl_c[lv]) + _dot(kd_t[lv], v_new[lv])
                s_scr[lv] = s_new
                sp_ref[vcols[lv], :] = s_new

        for lv in range(nv):
            y = _rmsnorm_rows(o[lv], nw) * jax.nn.silu(z_ref[:, vcols[lv]])
            y_ref[:, vcols[lv]] = y.astype(y_ref.dtype)

    if sample:
        real = _real_row_mask((CHUNK_T, 1), 0)
        _zero_conv_history((bq, bk, bv))
        body(jnp.where(real, q_ref[...], cq_ref[...]), jnp.where(real, k_ref[...], ck_ref[...]),
             jnp.where(real, v_ref[...], cv_ref[...]))
    else:
        @pl.when(pl.program_id(1) % chunks_per_seq == 0)
        def _():
            _zero_conv_history((bq, bk, bv))
            s_scr[...] = jnp.zeros(s_scr.shape, F32)
        body(q_ref[...], k_ref[...], v_ref[...])


class _Dims(NamedTuple):
    n_prompt_seqs: int
    chunks_per_seq: int
    n_sample_seqs: int

    @property
    def n_prompt_chunks(self):
        return self.n_prompt_seqs * self.chunks_per_seq

    @property
    def n_steps(self):
        return self.n_prompt_chunks + self.n_sample_seqs // SEQS_PER_CHUNK

    @property
    def n_rows(self):
        return self.n_steps * CHUNK_T


def _gdn_mixer(proj, gates, conv_w, norm_w, cache8, states, layer, prev, dims):
    beta, gc, gct = gates
    hb = GDN_HEAD
    n_pc = dims.n_prompt_chunks
    n_layers, n_seq = states.shape[:2]
    state_rows = GDN_V_HEADS * hb
    shift = _conv_shift_matrix()
    y_shape = jax.ShapeDtypeStruct((dims.n_rows, GDN_V_W), BF16)

    def call(sample, heads, first_step, n_steps, extra_specs, extra_args, state_spec, state_shape,
             aliased):
        qw, vw = heads * hb, 2 * heads * hb
        qkb = GDN_QK_W // qw
        vb = (2 * GDN_QK_W) // vw
        zb = GDN_CONV_DIM // vw
        gct_rows = max(SUBLANES, 2 * heads)

        def spec(width, col_off, row_off=first_step):
            return pl.BlockSpec((CHUNK_T, width), lambda g, t: (row_off + t, col_off + g))

        row128 = pl.BlockSpec((CHUNK_T, LANES), lambda g, t: (first_step + t, 0))
        in_specs = [pl.BlockSpec(shift.shape, lambda g, t: (0, 0)),
                    spec(qw, 0), spec(qw, qkb), spec(vw, vb), spec(vw, zb), row128, row128,
                    pl.BlockSpec((gct_rows, CHUNK_T),
                                 lambda g, t: ((2 * heads * g) // gct_rows, first_step + t)),
                    pl.BlockSpec((CONV_W, qw), lambda g, t: (0, g)),
                    pl.BlockSpec((CONV_W, qw), lambda g, t: (0, qkb + g)),
                    pl.BlockSpec((CONV_W, vw), lambda g, t: (0, vb + g)),
                    pl.BlockSpec((1, hb), lambda g, t: (0, 0))]
        args = [shift, proj, proj, proj, proj, beta, gc, gct, conv_w, conv_w, conv_w,
                norm_w.reshape(1, hb)]
        in_specs += extra_specs(spec, vw) + [pl.BlockSpec(memory_space=pl.ANY)] * len(aliased)
        args += extra_args
        n_in = len(args)
        return pl.pallas_call(
            functools.partial(_gdn_kernel, heads=heads, sample=sample,
                              chunks_per_seq=dims.chunks_per_seq, n_prev=len(aliased)),
            grid=(GDN_QK_HEADS // heads, n_steps),
            in_specs=in_specs,
            out_specs=[pl.BlockSpec((CHUNK_T, vw), lambda g, t: (first_step + t, g)), state_spec(vw)],
            out_shape=[y_shape, state_shape],
            input_output_aliases={n_in + i: out for i, out in enumerate(sorted(aliased))},
            scratch_shapes=[pltpu.VMEM((CONV_HIST, qw), F32), pltpu.VMEM((CONV_HIST, qw), F32),
                            pltpu.VMEM((CONV_HIST, vw), F32), pltpu.VMEM((2 * heads, hb, hb), F32)],
            compiler_params=_cparams(("parallel", "arbitrary")),
            name="gdn_sample" if sample else "gdn_prompt",
        )(*args, *[aliased[k] for k in sorted(aliased)])

    prev_p, prev_s = prev if prev else (None, None)
    cps = dims.chunks_per_seq
    y, sp = call(
        False, GDN_HEADS_PER_PROMPT_STEP, 0, n_pc, lambda spec, vw: [], [],
        lambda vw: pl.BlockSpec((None, None, vw, hb), lambda g, t: (layer, t // cps, g, 0)),
        jax.ShapeDtypeStruct((n_layers, dims.n_prompt_seqs, state_rows, hb), F32),
        {} if prev_p is None else {1: prev_p})
    s_block = lambda vw: pl.BlockSpec((None, SEQS_PER_CHUNK, vw, hb), lambda g, t: (layer, t, g, 0))
    y, ss = call(
        True, GDN_HEADS_PER_STEP, n_pc, n_seq // SEQS_PER_CHUNK,
        lambda spec, vw: [spec(vw // 2, 0, 0), spec(vw // 2, GDN_QK_W // (vw // 2), 0),
                          spec(vw, (2 * GDN_QK_W) // vw, 0), s_block(vw)],
        [cache8, cache8, cache8, states], s_block,
        jax.ShapeDtypeStruct((n_layers, n_seq, state_rows, hb), F32),
        {0: y} if prev_s is None else {0: y, 1: prev_s})
    return y, sp, ss


def _ssd_kernel(*refs, groups, sample, chunks_per_seq, n_prev):
    (shift_ref, z_ref, x_ref, b_ref, c_ref, dt_ref, acs_ref, acst_ref, cwx_ref, cwb_ref, cwc_ref,
     cbx_ref, cbb_ref, cbc_ref, de_ref, nw_ref) = refs[:16]
    n_in = 16
    if sample:
        cx_ref, cb_ref, cc_ref, s0_ref = refs[16:20]
        n_in = 20
    y_ref, st_ref, bx, bb, bc, s_scr = refs[n_in + n_prev:]
    ss_ref = sp_ref = st_ref
    grp = pl.program_id(0)
    t = CHUNK_T
    gw, n = SSD_GW, SSD_STATE
    gcols = [slice(gi * gw, (gi + 1) * gw) for gi in range(groups)]
    ncols = [slice(gi * n, (gi + 1) * n) for gi in range(groups)]
    n_heads = groups * SSD_HPG

    def scaled_by_head(s_mat, col):
        eb = jnp.broadcast_to(jnp.exp(acst_ref[:, col:col + 1]), (n_heads, n))
        return jnp.concatenate([s_mat[r * SSD_HEAD:(r + 1) * SSD_HEAD] * eb[r:r + 1, :]
                                for r in range(s_mat.shape[0] // SSD_HEAD)], axis=0)

    def body(xx, xb, xc):
        xg = jax.nn.silu(_causal_conv(bx, xx, cwx_ref, shift_ref) + cbx_ref[...])
        bg = jax.nn.silu(_causal_conv(bb, xb, cwb_ref, shift_ref) + cbb_ref[...])
        cg = jax.nn.silu(_causal_conv(bc, xc, cwc_ref, shift_ref) + cbc_ref[...])

        dt_tile, acs_tile = dt_ref[...], acs_ref[...]
        sel_shape = (LANES, groups * gw)
        sel = _iota2(sel_shape, 0) == grp * n_heads + _iota2(sel_shape, 1) // SSD_HEAD
        parts = jnp.concatenate(_split3(dt_tile) + _split3(acs_tile), axis=0)
        ex = jnp.dot(parts, sel.astype(BF16), preferred_element_type=F32)
        dt_e = ex[0:t] + ex[t:2 * t] + ex[2 * t:3 * t]
        acs_e = ex[3 * t:4 * t] + ex[4 * t:5 * t] + ex[5 * t:6 * t]
        acs_last_e = _seq_last_rows(acs_e) if sample else acs_e[t - 1:t, :]

        xdt = xg * dt_e
        incl, _ = _chunk_masks(sample)
        cbs = [_dot_nt(cg[:, ncols[gi]], bg[:, ncols[gi]]) for gi in range(groups)]
        acs_c = [_pick_lane(acs_tile, grp * n_heads + r) for r in range(n_heads)]
        ms_ = [cbs[r // SSD_HPG] * jnp.exp(jnp.where(incl, acs_c[r] - acst_ref[r:r + 1, :], NEG_BIG))
               for r in range(n_heads)]
        lane_lo = _iota2((t, 2 * SSD_HEAD), 1) < SSD_HEAD
        pairs = []
        for pr in range(n_heads // 2):
            xpair = xdt[:, pr * 2 * SSD_HEAD:(pr + 1) * 2 * SSD_HEAD]
            rhs = jnp.concatenate([jnp.where(lane_lo, xpair, 0.0), jnp.where(lane_lo, 0.0, xpair)],
                                  axis=0)
            pairs.append(_dot(jnp.concatenate([ms_[2 * pr], ms_[2 * pr + 1]], axis=1), rhs))
        y_diag = jnp.concatenate(pairs, axis=1)

        xd_t = (xdt * jnp.exp(acs_last_e - acs_e)).T
        if sample:
            offs = []
            zeros = jnp.zeros((SEQ_PAD, n), F32)
            seq_of_lane = _iota2((groups * gw, t), 1) // SEQ_PAD
            bg_bf = bg.astype(BF16)
            for s in range(SEQS_PER_CHUNK):
                rows = slice(s * SEQ_PAD, (s + 1) * SEQ_PAD)
                s_old = s0_ref[s]
                offs.append(jnp.concatenate(
                    [_dot_nt(jnp.concatenate([cg[rows, ncols[gi]], zeros], axis=0),
                             s_old[gcols[gi]])[:SEQ_PAD] for gi in range(groups)], axis=1))
                xd_s = jnp.where(seq_of_lane == s, xd_t, 0.0)
                upd = jnp.concatenate([_dot(xd_s[gcols[gi]], bg_bf[:, ncols[gi]])
                                       for gi in range(groups)], axis=0)
                ss_ref[s] = scaled_by_head(s_old, (s + 1) * SEQ_PAD - 1) + upd
            y_off = jnp.concatenate(offs, axis=0)
        else:
            s_old = s_scr[...]
            y_off = jnp.concatenate([_dot_nt(cg[:, ncols[gi]], s_old[gcols[gi]])
                                     for gi in range(groups)], axis=1)
            upd = jnp.concatenate([_dot(xd_t[gcols[gi]], bg[:, ncols[gi]]) for gi in range(groups)],
                                  axis=0)
            s_new = scaled_by_head(s_old, t - 1) + upd
            s_scr[...] = s_new
            sp_ref[...] = s_new

        y = y_diag + y_off * jnp.exp(acs_e) + de_ref[...] * xg
        y = y * jax.nn.silu(z_ref[...])
        nw = nw_ref[...]
        for gi in range(groups):
            yg = y[:, gcols[gi]]
            yg = yg * lax.rsqrt(jnp.mean(yg * yg, axis=-1, keepdims=True) + EPS) * nw[:, gcols[gi]]
            y_ref[:, gcols[gi]] = yg.astype(y_ref.dtype)

    if sample:
        real = _real_row_mask((CHUNK_T, 1), 0)
        _zero_conv_history((bx, bb, bc))
        body(jnp.where(real, x_ref[...], cx_ref[...]), jnp.where(real, b_ref[...], cb_ref[...]),
             jnp.where(real, c_ref[...], cc_ref[...]))
    else:
        @pl.when(pl.program_id(1) % chunks_per_seq == 0)
        def _():
            _zero_conv_history((bx, bb, bc))
            s_scr[...] = jnp.zeros(s_scr.shape, F32)
        body(x_ref[...], b_ref[...], c_ref[...])


def _ssd_mixer(proj, gates, conv_w, conv_b, d_skip, norm_w, cache8, states, layer, prev, dims):
    dt, acs, acst = gates
    n_pc = dims.n_prompt_chunks
    n_layers, n_seq = states.shape[:2]
    state_rows = SSD_HEADS * SSD_HEAD
    shift = _conv_shift_matrix()
    conv_b2 = conv_b.reshape(1, -1)
    d_e = jnp.repeat(d_skip, SSD_HEAD).reshape(1, SSD_D_INNER)
    nw2 = norm_w.reshape(1, SSD_D_INNER)

    def call(sample, groups, first_step, n_steps, extra_specs, extra_args, state_spec, state_shape,
             aliased):
        gw, n = groups * SSD_GW, groups * SSD_STATE
        xb = SSD_D_INNER // gw
        bb = (2 * SSD_D_INNER) // n
        cb = bb + SSD_GN // n
        wbb = SSD_D_INNER // n
        wcb = wbb + SSD_GN // n

        def spec(width, col_off, row_off=first_step):
            return pl.BlockSpec((CHUNK_T, width), lambda g, t: (row_off + t, col_off + g))

        def wspec(rows, width, col_off):
            return pl.BlockSpec((rows, width), lambda g, t: (0, col_off + g))

        row128 = pl.BlockSpec((CHUNK_T, LANES), lambda g, t: (first_step + t, 0))
        in_specs = [pl.BlockSpec(shift.shape, lambda g, t: (0, 0)),
                    spec(gw, 0), spec(gw, xb), spec(n, bb), spec(n, cb), row128, row128,
                    pl.BlockSpec((groups * SSD_HPG, CHUNK_T), lambda g, t: (g, first_step + t)),
                    wspec(CONV_W, gw, 0), wspec(CONV_W, n, wbb), wspec(CONV_W, n, wcb),
                    wspec(1, gw, 0), wspec(1, n, wbb), wspec(1, n, wcb),
                    wspec(1, gw, 0), wspec(1, gw, 0)]
        args = [shift, proj, proj, proj, proj, dt, acs, acst, conv_w, conv_w, conv_w,
                conv_b2, conv_b2, conv_b2, d_e, nw2]
        in_specs += (extra_specs(spec, gw, n, wbb, wcb)
                     + [pl.BlockSpec(memory_space=pl.ANY)] * len(aliased))
        args += extra_args
        n_in = len(args)
        return pl.pallas_call(
            functools.partial(_ssd_kernel, groups=groups, sample=sample,
                              chunks_per_seq=dims.chunks_per_seq, n_prev=len(aliased)),
            grid=(SSD_GROUPS // groups, n_steps),
            in_specs=in_specs,
            out_specs=[pl.BlockSpec((CHUNK_T, gw), lambda g, t: (first_step + t, g)), state_spec(gw)],
            out_shape=[jax.ShapeDtypeStruct((dims.n_rows, SSD_D_INNER), BF16), state_shape],
            input_output_aliases={n_in + i: out for i, out in enumerate(sorted(aliased))},
            scratch_shapes=[pltpu.VMEM((CONV_HIST, gw), F32), pltpu.VMEM((CONV_HIST, n), F32),
                            pltpu.VMEM((CONV_HIST, n), F32), pltpu.VMEM((gw, SSD_STATE), F32)],
            compiler_params=_cparams(("parallel", "arbitrary")),
            name="ssd_sample" if sample else "ssd_prompt",
        )(*args, *[aliased[k] for k in sorted(aliased)])

    prev_p, prev_s = prev if prev else (None, None)
    cps = dims.chunks_per_seq
    y, sp = call(
        False, SSD_GROUPS_PER_PROMPT_STEP, 0, n_pc, lambda *a: [], [],
        lambda gw: pl.BlockSpec((None, None, gw, SSD_STATE), lambda g, t: (layer, t // cps, g, 0)),
        jax.ShapeDtypeStruct((n_layers, dims.n_prompt_seqs, state_rows, SSD_STATE), F32),
        {} if prev_p is None else {1: prev_p})
    s_block = lambda gw: pl.BlockSpec((None, SEQS_PER_CHUNK, gw, SSD_STATE),
                                      lambda g, t: (layer, t, g, 0))
    y, ss = call(
        True, SSD_GROUPS_PER_STEP, n_pc, n_seq // SEQS_PER_CHUNK,
        lambda spec, gw, n, wbb, wcb: [spec(gw, 0, 0), spec(n, wbb, 0), spec(n, wcb, 0), s_block(gw)],
        [cache8, cache8, cache8, states], s_block,
        jax.ShapeDtypeStruct((n_layers, n_seq, state_rows, SSD_STATE), F32),
        {0: y} if prev_s is None else {0: y, 1: prev_s})
    return y, sp, ss


def _pad_sequences(x, n_real):
    pads = [(0, 0)] * (x.ndim - 2) + [(SEQ_PAD - n_real, 0), (0, 0)]
    xp = jnp.pad(x, pads)
    return xp.reshape(x.shape[:-3] + (x.shape[-3] * SEQ_PAD, x.shape[-1]))


def _history_rows(cache):
    n_real = SEQ_PAD // 2
    lo = SEQ_PAD - n_real - (CONV_W - 1)
    xp = jnp.pad(cache, ((0, 0), (lo, n_real), (0, 0)))
    return xp.reshape(cache.shape[0] * SEQ_PAD, cache.shape[2])


def kernel(x_prompt, x_sample, state_gdn, cache_gdn_conv, state_ssd, cache_ssd_conv, p_prompt, p_sample,
           norm_w, gdn_w_in, gdn_conv_w, gdn_A_log, gdn_dt_bias, gdn_norm_w, gdn_w_out,
           ssd_w_in, ssd_conv_w, ssd_conv_b, ssd_A_log, ssd_dt_bias, ssd_D, ssd_norm_w, ssd_w_out,
           ple_w_proj, ple_w_gate, ple_norm_w, final_norm_w):
    bp, lp, d = x_prompt.shape
    bs, ls, _ = x_sample.shape
    depth = norm_w.shape[0]
    assert d == D_MODEL and ls == SEQ_PAD // 2 and lp % CHUNK_T == 0 and bs % SEQS_PER_CHUNK == 0
    mp, ms = bp * lp, bs * SEQ_PAD
    assert mp % ROW_TILE == 0 and ms % ROW_TILE == 0

    h = jnp.concatenate([x_prompt.reshape(mp, d), _pad_sequences(x_sample, ls)], axis=0)
    p = jnp.concatenate([p_prompt.reshape(depth, mp, PLE_DIM), _pad_sequences(p_sample, ls)], axis=1)

    def conv_caches(proj, lo, hi):
        c_p = jnp.stack([proj[(b + 1) * lp - (CONV_W - 1):(b + 1) * lp, lo:hi] for b in range(bp)])
        c_s = proj[mp:, lo:hi].reshape(bs, SEQ_PAD, hi - lo)[:, SEQ_PAD - (CONV_W - 1):, :]
        return c_p, c_s

    dims = _Dims(n_prompt_seqs=bp, chunks_per_seq=lp // CHUNK_T, n_sample_seqs=bs)
    gdn_states = state_gdn.reshape(state_gdn.shape[0], bs, GDN_V_HEADS * GDN_HEAD, GDN_HEAD)
    ssd_states = state_ssd.reshape(state_ssd.shape[0], bs, SSD_HEADS * SSD_HEAD, SSD_STATE)
    gdn_w_in_t = jnp.swapaxes(gdn_w_in, 1, 2)
    ssd_w_in_t = jnp.swapaxes(ssd_w_in, 1, 2)
    w_gate_bf = ple_w_gate.astype(BF16)
    caches = {k: [] for k in ("cg_p", "cs_p", "cg_s", "cs_s")}
    gdn_prev, ssd_prev = (), ()
    hn = _norm_rows(h, norm_w[0], BF16)
    for i in range(depth):
        j = i // 2
        if i % 2 == 0:
            proj = _matmul(hn, gdn_w_in_t, j, GDN_MAIN, "plain", w_nk=True)
            gates = _gdn_gates(hn, gdn_w_in[j, :, GDN_MAIN:], gdn_A_log[j], gdn_dt_bias[j], mp)
            y, *gdn_prev = _gdn_mixer(proj, gates, gdn_conv_w[j], gdn_norm_w[j],
                                      _history_rows(cache_gdn_conv[j]), gdn_states, j, gdn_prev, dims)
            c_p, c_s = conv_caches(proj, 0, GDN_CONV_DIM)
            w_out = gdn_w_out
            keys = ("cg_p", "cg_s")
        else:
            proj = _matmul(hn, ssd_w_in_t, j, SSD_MAIN, "plain", w_nk=True)
            gates = _ssd_gates(hn, ssd_w_in[j, :, SSD_MAIN:], ssd_A_log[j], ssd_dt_bias[j], mp)
            y, *ssd_prev = _ssd_mixer(proj, gates, ssd_conv_w[j], ssd_conv_b[j], ssd_D[j], ssd_norm_w[j],
                                      _history_rows(cache_ssd_conv[j]), ssd_states, j, ssd_prev, dims)
            c_p, c_s = conv_caches(proj, SSD_D_INNER, SSD_MAIN)
            w_out = ssd_w_out
            keys = ("cs_p", "cs_s")
        for key, val in zip(keys, (c_p, c_s)):
            caches[key].append(val)
        h_mid, h_mid_bf = _matmul(y, w_out, j, D_MODEL, "residual", extras=(h,))
        if i + 1 < depth:
            h, hn = _ple_add(h_mid_bf, h_mid, p, w_gate_bf, ple_w_proj, i, ple_norm_w[i], norm_w[i + 1])
        else:
            y_p, y_s = _ple_add(h_mid_bf, h_mid, p, w_gate_bf, ple_w_proj, i, ple_norm_w[i],
                                final_norm_w, mp)

    y_prompt = y_p.reshape(bp, lp, d)
    y_sample = y_s.reshape(bs, SEQ_PAD, d)[:, SEQ_PAD - ls:, :]
    cc = {k: jnp.stack(v) for k, v in caches.items()}
    sg_p, sg_s = (s.reshape(s.shape[:2] + (GDN_V_HEADS, GDN_HEAD, GDN_HEAD)) for s in gdn_prev)
    ss_p, ss_s = (s.reshape(s.shape[:2] + (SSD_HEADS, SSD_HEAD, SSD_STATE)) for s in ssd_prev)
    return (y_prompt, y_sample, sg_p, cc["cg_p"], ss_p, cc["cs_p"], sg_s, cc["cg_s"], ss_s, cc["cs_s"])
```

```python
import functools
from typing import NamedTuple

import jax
import jax.numpy as jnp
import numpy as np
from jax import lax
from jax.experimental import pallas as pl
from jax.experimental.pallas import tpu as pltpu

F32 = jnp.float32
BF16 = jnp.bfloat16

EPS = 1e-6
CONV_W = 4
D_MODEL = 2048
PLE_DIM = 256

GDN_HEAD = 128
GDN_QK_HEADS = 16
GDN_V_HEADS = 32
GDN_QK_W = GDN_QK_HEADS * GDN_HEAD
GDN_V_W = GDN_V_HEADS * GDN_HEAD
GDN_CONV_DIM = 2 * GDN_QK_W + GDN_V_W
GDN_MAIN = GDN_CONV_DIM + GDN_V_W

SSD_D_INNER = 4096
SSD_HEAD = 64
SSD_HEADS = 64
SSD_STATE = 128
SSD_GROUPS = 8
SSD_HPG = SSD_HEADS // SSD_GROUPS
SSD_GN = SSD_GROUPS * SSD_STATE
SSD_CONV_DIM = SSD_D_INNER + 2 * SSD_GN
SSD_MAIN = SSD_D_INNER + SSD_CONV_DIM
SSD_GW = SSD_HPG * SSD_HEAD

LANES = 128
SUBLANES = 8
CHUNK_T = 128
SEQ_PAD = 8
CONV_HIST = 16
SEQS_PER_CHUNK = CHUNK_T // SEQ_PAD
INV_BLOCK = 16
GDN_HEADS_PER_STEP = 4
GDN_HEADS_PER_PROMPT_STEP = 16
SSD_GROUPS_PER_STEP = 2
SSD_GROUPS_PER_PROMPT_STEP = 8
ROW_TILE = 512
MM_TILES = {"plain": (1024, 2048), "residual": (512, 1024)}
VMEM_LIMIT = 56 * 1024 * 1024
NEG_BIG = -1e30


def _cparams(sem):
    return pltpu.CompilerParams(dimension_semantics=sem, vmem_limit_bytes=VMEM_LIMIT)


def _dot(a, b):
    return jnp.dot(a.astype(BF16), b.astype(BF16), preferred_element_type=F32)


def _dot_nt(a, b):
    return lax.dot_general(a.astype(BF16), b.astype(BF16), (((1,), (1,)), ((), ())),
                           preferred_element_type=F32)


def _split3(x):
    x1 = x.astype(BF16)
    r1 = x - x1.astype(F32)
    x2 = r1.astype(BF16)
    r2 = r1 - x2.astype(F32)
    return [x1, x2, r2.astype(BF16)]


def _softplus(x):
    return jnp.maximum(x, 0.0) + jnp.log1p(jnp.exp(-jnp.abs(x)))


def _iota2(shape, dim):
    return lax.broadcasted_iota(jnp.int32, shape, dim)


def _pick_lane(x, idx):
    lane = _iota2(x.shape, 1)
    return jnp.sum(jnp.where(lane == idx, x, 0.0), axis=1, keepdims=True)


def _chunk_masks(sample):
    ii = _iota2((CHUNK_T, CHUNK_T), 0)
    jj = _iota2((CHUNK_T, CHUNK_T), 1)
    incl = ii >= jj
    strict = ii > jj
    if sample:
        same = (ii // SEQ_PAD) == (jj // SEQ_PAD)
        incl = incl & same
        strict = strict & same
    return incl, strict


def _seq_last_rows(x):
    rows, cols = x.shape
    x3 = x.reshape(rows // SEQ_PAD, SEQ_PAD, cols)
    last = jnp.broadcast_to(x3[:, SEQ_PAD - 1:SEQ_PAD, :], x3.shape)
    return last.reshape(rows, cols)


def _unit_lower_inverses(mats, sample):
    ii = _iota2(mats[0].shape, 0)
    jj = _iota2(mats[0].shape, 1)
    eye = (ii == jj).astype(F32)

    def neumann(xs, squarings):
        ts = [eye - x for x in xs]
        ps = xs
        for _ in range(squarings):
            ps = [_dot(p, p) for p in ps]
            ts = [t + _dot(t, p) for t, p in zip(ts, ps)]
        return ts

    if sample:
        assert SEQ_PAD == 8
        return neumann(mats, 2)
    assert INV_BLOCK == 16 and CHUNK_T // INV_BLOCK == 8
    blk = (ii // INV_BLOCK) == (jj // INV_BLOCK)
    ds = [jnp.where(blk, a, 0.0) for a in mats]
    tds = neumann(ds, 3)
    ns = [_dot(td, a - d) for td, a, d in zip(tds, mats, ds)]
    return [_dot(p, td) for p, td in zip(neumann(ns, 2), tds)]


def _conv_shift_matrix():
    r = np.arange((CONV_W - 1) * CHUNK_T)[:, None]
    c = np.arange(CONV_HIST + CHUNK_T)[None, :]
    return jnp.asarray(c == CONV_HIST - (CONV_W - 1) + r % CHUNK_T + r // CHUNK_T, dtype=BF16)


def _causal_conv(hist_ref, x, w_ref, shift_ref):
    t = x.shape[0]
    xb = jnp.concatenate([hist_ref[...], x], axis=0).astype(BF16)
    taps = jnp.dot(shift_ref[...], xb, preferred_element_type=F32)
    w = w_ref[...]
    y = taps[0:t] * w[0:1, :]
    for j in range(1, CONV_W - 1):
        y = y + taps[j * t:(j + 1) * t] * w[j:j + 1, :]
    y = y + x * w[CONV_W - 1:CONV_W, :]
    hist_ref[...] = x[t - CONV_HIST:, :]
    return y


def _zero_conv_history(bufs):
    for b in bufs:
        b[...] = jnp.zeros(b.shape, F32)


def _real_row_mask(shape, dim):
    return (_iota2(shape, dim) % SEQ_PAD) >= (SEQ_PAD // 2)


def _rmsnorm_rows(x, w):
    return x * lax.rsqrt(jnp.mean(x * x, axis=-1, keepdims=True) + EPS) * w


def _norm_kernel(x_ref, w_ref, o_ref):
    o_ref[...] = _rmsnorm_rows(x_ref[...], w_ref[...]).astype(o_ref.dtype)


def _norm_rows(x, w, out_dtype):
    m, d = x.shape
    return pl.pallas_call(
        _norm_kernel,
        grid=(m // ROW_TILE,),
        in_specs=[pl.BlockSpec((ROW_TILE, d), lambda i: (i, 0)),
                  pl.BlockSpec((1, d), lambda i: (0, 0))],
        out_specs=pl.BlockSpec((ROW_TILE, d), lambda i: (i, 0)),
        out_shape=jax.ShapeDtypeStruct((m, d), out_dtype),
        compiler_params=_cparams(("parallel",)),
        name="norm_rows",
    )(x, w.reshape(1, d))


def _ple_kernel(hbf_ref, wg_ref, h_ref, p_ref, wp_ref, pnw_ref, nnw_ref, oa_ref, ob_ref, *,
                n_prompt_tiles):
    gate = jax.nn.sigmoid(jnp.dot(hbf_ref[...], wg_ref[...], preferred_element_type=F32))
    e = _rmsnorm_rows(_dot(p_ref[...], wp_ref[...]), pnw_ref[...])
    h = h_ref[...] + e * gate
    hn = _rmsnorm_rows(h, nnw_ref[...])
    if n_prompt_tiles is None:
        oa_ref[...] = h
        ob_ref[...] = hn.astype(ob_ref.dtype)
    else:
        is_prompt = pl.program_id(0) < n_prompt_tiles

        @pl.when(is_prompt)
        def _():
            oa_ref[...] = hn

        @pl.when(jnp.logical_not(is_prompt))
        def _():
            ob_ref[...] = hn


def _ple_add(h_bf, h, p, w_gate_bf, w_proj, layer, ple_norm_w, next_norm_w, n_prompt_rows=None):
    m = h.shape[0]
    row = lambda width: pl.BlockSpec((ROW_TILE, width), lambda i: (i, 0))
    vec = pl.BlockSpec((1, D_MODEL), lambda i: (0, 0))
    once = dict(pipeline_mode=pl.Buffered(1))
    if n_prompt_rows is None:
        n_pt = None
        out_specs = [row(D_MODEL), row(D_MODEL)]
        out_shape = [jax.ShapeDtypeStruct((m, D_MODEL), F32), jax.ShapeDtypeStruct((m, D_MODEL), BF16)]
    else:
        n_pt = n_prompt_rows // ROW_TILE
        out_specs = [pl.BlockSpec((ROW_TILE, D_MODEL), lambda i: (jnp.minimum(i, n_pt - 1), 0)),
                     pl.BlockSpec((ROW_TILE, D_MODEL), lambda i: (jnp.maximum(i - n_pt, 0), 0))]
        out_shape = [jax.ShapeDtypeStruct((n_prompt_rows, D_MODEL), F32),
                     jax.ShapeDtypeStruct((m - n_prompt_rows, D_MODEL), F32)]
    return pl.pallas_call(
        functools.partial(_ple_kernel, n_prompt_tiles=n_pt),
        grid=(m // ROW_TILE,),
        in_specs=[row(D_MODEL),
                  pl.BlockSpec((None, D_MODEL, D_MODEL), lambda i: (layer, 0, 0), **once),
                  row(D_MODEL),
                  pl.BlockSpec((None, ROW_TILE, PLE_DIM), lambda i: (layer, i, 0)),
                  pl.BlockSpec((None, PLE_DIM, D_MODEL), lambda i: (layer, 0, 0), **once),
                  vec, vec],
        out_specs=out_specs,
        out_shape=out_shape,
        compiler_params=_cparams(("arbitrary",)),
        name="ple_add",
    )(h_bf, w_gate_bf, h, p, w_proj, ple_norm_w.reshape(1, D_MODEL), next_norm_w.reshape(1, D_MODEL))


def _mm_kernel(x_ref, w_ref, *rest, epilogue, w_nk):
    wbf_ref = rest[-1]
    refs = rest[:-1]

    @pl.when(pl.program_id(1) == 0)
    def _():
        wbf_ref[...] = w_ref[...].astype(BF16)

    x = x_ref[...]
    acc = _dot_nt(x, wbf_ref[...]) if w_nk else jnp.dot(x, wbf_ref[...], preferred_element_type=F32)
    if epilogue == "plain":
        (o_ref,) = refs
        o_ref[...] = acc
    elif epilogue == "residual":
        res_ref, o_ref, obf_ref = refs
        h = res_ref[...] + acc
        o_ref[...] = h
        obf_ref[...] = h.astype(BF16)
    else:
        raise ValueError(epilogue)


def _matmul(x, w, layer, n_cols, epilogue, extras=(), w_nk=False):
    m, k = x.shape
    tm, tn = MM_TILES[epilogue]
    while m % tm:
        tm //= 2
    assert n_cols % tn == 0
    tile = pl.BlockSpec((tm, tn), lambda j, i: (i, j))
    w_block = (None, tn, k) if w_nk else (None, k, tn)
    w_index = (lambda j, i: (layer, j, 0)) if w_nk else (lambda j, i: (layer, 0, j))
    w_spec = pl.BlockSpec(w_block, w_index, pipeline_mode=pl.Buffered(1))
    in_specs = [pl.BlockSpec((tm, k), lambda j, i: (i, 0)), w_spec] + [tile] * len(extras)
    out_shape = [jax.ShapeDtypeStruct((m, n_cols), F32)]
    if epilogue == "residual":
        out_shape.append(jax.ShapeDtypeStruct((m, n_cols), BF16))
    out = pl.pallas_call(
        functools.partial(_mm_kernel, epilogue=epilogue, w_nk=w_nk),
        grid=(n_cols // tn, m // tm),
        in_specs=in_specs,
        out_specs=[tile] * len(out_shape),
        out_shape=out_shape,
        scratch_shapes=[pltpu.VMEM(w_block[1:], BF16)],
        compiler_params=_cparams(("parallel", "arbitrary")),
        name="matmul_" + epilogue,
    )(x, w, *extras)
    return out if epilogue == "residual" else out[0]


def _dot_f32(a, b):
    return jnp.dot(a, b, preferred_element_type=F32, precision=lax.Precision.HIGHEST)


def _chunk_cumsums(a, n_heads, is_prompt):
    shift = jnp.where(is_prompt, CHUNK_T.bit_length() - 1, SEQ_PAD.bit_length() - 1)
    ii = _iota2((CHUNK_T, CHUNK_T), 0)
    jj = _iota2((CHUNK_T, CHUNK_T), 1)
    same = lax.shift_right_logical(ii, shift) == lax.shift_right_logical(jj, shift)
    lower = (same & (jj <= ii)).astype(F32)
    upper = (same & (ii <= jj)).astype(F32)
    a_t = a.T[:n_heads]
    blocks = [slice(b * CHUNK_T, (b + 1) * CHUNK_T) for b in range(ROW_TILE // CHUNK_T)]
    sums = jnp.concatenate([_dot_f32(lower, a[b]) for b in blocks], axis=0)
    sums_t = jnp.concatenate([_dot_f32(a_t[:, b], upper) for b in blocks], axis=1)
    return sums, sums_t


def _gdn_gate_kernel(x_ref, wa_ref, wb_ref, alog_ref, dtb_ref, beta_ref, gc_ref, gct_ref, *,
                     n_prompt_tiles):
    x = x_ref[...]
    is_prompt = pl.program_id(0) < n_prompt_tiles
    live_rows = is_prompt | _real_row_mask((ROW_TILE, LANES), 0)
    a = _dot(x, wa_ref[...])
    g = jnp.where(live_rows, -jnp.exp(alog_ref[...]) * _softplus(a + dtb_ref[...]), 0.0)
    beta_ref[...] = jnp.where(live_rows, jax.nn.sigmoid(_dot(x, wb_ref[...])), 0.0)
    gc_ref[...], gct_ref[...] = _chunk_cumsums(g, GDN_V_HEADS, is_prompt)


def _pad_lanes(x):
    return jnp.pad(x, ((0, 0), (0, LANES - x.shape[1])))


def _gdn_gates(hn, w_ab, a_log, dt_bias, n_prompt_rows):
    m = hn.shape[0]
    wa = w_ab[:, :GDN_V_HEADS]
    wb = w_ab[:, GDN_V_HEADS:]
    row = pl.BlockSpec((ROW_TILE, LANES), lambda i: (i, 0))
    full = lambda shape: pl.BlockSpec(shape, lambda i: (0, 0))
    return pl.pallas_call(
        functools.partial(_gdn_gate_kernel, n_prompt_tiles=n_prompt_rows // ROW_TILE),
        grid=(m // ROW_TILE,),
        in_specs=[pl.BlockSpec((ROW_TILE, D_MODEL), lambda i: (i, 0)),
                  full((D_MODEL, LANES)), full((D_MODEL, LANES)), full((1, LANES)), full((1, LANES))],
        out_specs=[row, row, pl.BlockSpec((GDN_V_HEADS, ROW_TILE), lambda i: (0, i))],
        out_shape=[jax.ShapeDtypeStruct((m, LANES), F32)] * 2
        + [jax.ShapeDtypeStruct((GDN_V_HEADS, m), F32)],
        compiler_params=_cparams(("parallel",)),
        name="gdn_gates",
    )(hn, _pad_lanes(wa), _pad_lanes(wb),
      _pad_lanes(a_log.reshape(1, -1)), _pad_lanes(dt_bias.reshape(1, -1)))


def _ssd_gate_kernel(x_ref, w_ref, alog_ref, dtb_ref, dt_ref, acs_ref, acst_ref, *, n_prompt_tiles):
    x = x_ref[...]
    is_prompt = pl.program_id(0) < n_prompt_tiles
    live_rows = is_prompt | _real_row_mask((ROW_TILE, LANES), 0)
    dt = jnp.where(live_rows, _softplus(_dot(x, w_ref[...]) + dtb_ref[...]), 0.0)
    dt_ref[...] = dt
    acs_ref[...], acst_ref[...] = _chunk_cumsums(dt * -jnp.exp(alog_ref[...]), SSD_HEADS, is_prompt)


def _ssd_gates(hn, w, a_log, dt_bias, n_prompt_rows):
    m = hn.shape[0]
    row = pl.BlockSpec((ROW_TILE, LANES), lambda i: (i, 0))
    full = lambda shape: pl.BlockSpec(shape, lambda i: (0, 0))
    return pl.pallas_call(
        functools.partial(_ssd_gate_kernel, n_prompt_tiles=n_prompt_rows // ROW_TILE),
        grid=(m // ROW_TILE,),
        in_specs=[pl.BlockSpec((ROW_TILE, D_MODEL), lambda i: (i, 0)),
                  full((D_MODEL, LANES)), full((1, LANES)), full((1, LANES))],
        out_specs=[row, row, pl.BlockSpec((SSD_HEADS, ROW_TILE), lambda i: (0, i))],
        out_shape=[jax.ShapeDtypeStruct((m, LANES), F32)] * 2
        + [jax.ShapeDtypeStruct((SSD_HEADS, m), F32)],
        compiler_params=_cparams(("parallel",)),
        name="ssd_gates",
    )(hn, _pad_lanes(w), _pad_lanes(a_log.reshape(1, -1)), _pad_lanes(dt_bias.reshape(1, -1)))


def _l2norm_rows(x):
    return x * lax.rsqrt(jnp.sum(x * x, axis=-1, keepdims=True) + EPS)


def _gdn_kernel(*refs, heads, sample, chunks_per_seq, n_prev):
    (shift_ref, q_ref, k_ref, v_ref, z_ref, beta_ref, gc_ref, gct_ref, cwq_ref, cwk_ref, cwv_ref,
     nw_ref) = refs[:12]
    n_in = 12
    if sample:
        cq_ref, ck_ref, cv_ref, s0_ref = refs[12:16]
        n_in = 16
    y_ref, st_ref, bq, bk, bv, s_scr = refs[n_in + n_prev:]
    ss_ref = sp_ref = st_ref
    hg = pl.program_id(0)
    gct_row0 = (2 * heads * hg) % gct_ref.shape[0]

    def body(xq, xk, xv):
        qc = jax.nn.silu(_causal_conv(bq, xq, cwq_ref, shift_ref))
        kc = jax.nn.silu(_causal_conv(bk, xk, cwk_ref, shift_ref))
        vc = jax.nn.silu(_causal_conv(bv, xv, cwv_ref, shift_ref))
        incl, strict = _chunk_masks(sample)
        beta_tile, gc_tile = beta_ref[...], gc_ref[...]
        nw = nw_ref[...]
        nv = 2 * heads
        vcols = [slice(lv * GDN_HEAD, (lv + 1) * GDN_HEAD) for lv in range(nv)]
        qs_ = [_l2norm_rows(qc[:, vcols[hh]]) * (GDN_HEAD ** -0.5) for hh in range(heads)]
        ks_ = [_l2norm_rows(kc[:, vcols[hh]]) for hh in range(heads)]
        kks = [_dot_nt(k, k) for k in ks_]
        qks = [_dot_nt(q, k) for q, k in zip(qs_, ks_)]
        beta_c = [_pick_lane(beta_tile, nv * hg + lv) for lv in range(nv)]
        gc_c = [_pick_lane(gc_tile, nv * hg + lv) for lv in range(nv)]
        gc_r = [gct_ref[pl.ds(gct_row0 + lv, 1), :] for lv in range(nv)]
        decay = [jnp.exp(jnp.where(incl, c - r, NEG_BIG)) for c, r in zip(gc_c, gc_r)]
        a_mats = [jnp.where(strict, beta_c[lv] * kks[lv // 2] * decay[lv], 0.0) for lv in range(nv)]
        t_mats = _unit_lower_inverses(a_mats, sample)
        egc = [jnp.exp(c) for c in gc_c]
        uw = [_dot(t_mats[lv], jnp.concatenate([vc[:, vcols[lv]] * beta_c[lv],
                                                ks_[lv // 2] * (beta_c[lv] * egc[lv])], axis=1))
              for lv in range(nv)]
        u = [x[:, :GDN_HEAD] for x in uw]
        w = [x[:, GDN_HEAD:] for x in uw]
        qk_d = [qks[lv // 2] * decay[lv] for lv in range(nv)]
        q_dec = [qs_[lv // 2] * egc[lv] for lv in range(nv)]
        gl_c = [_seq_last_rows(c) if sample else c[CHUNK_T - 1:CHUNK_T, :] for c in gc_c]
        kd_t = [(ks_[lv // 2] * jnp.exp(gl_c[lv] - gc_c[lv])).T for lv in range(nv)]

        if sample:
            wq = [[_dot(jnp.concatenate([w[lv][s * SEQ_PAD:(s + 1) * SEQ_PAD],
                                         q_dec[lv][s * SEQ_PAD:(s + 1) * SEQ_PAD]], axis=0),
                        s0_ref[s, vcols[lv], :]) for s in range(SEQS_PER_CHUNK)] for lv in range(nv)]
            v_new = [u[lv] - jnp.concatenate([x[:SEQ_PAD] for x in wq[lv]], axis=0) for lv in range(nv)]
            o = [jnp.concatenate([x[SEQ_PAD:] for x in wq[lv]], axis=0) + _dot(qk_d[lv], v_new[lv])
                 for lv in range(nv)]
            seq_of_lane = _iota2((GDN_HEAD, CHUNK_T), 1) // SEQ_PAD
            for lv in range(nv):
                v_new_bf = v_new[lv].astype(BF16)
                for s in range(SEQS_PER_CHUNK):
                    dec = jnp.exp(gl_c[lv][s * SEQ_PAD:s * SEQ_PAD + 1, :])
                    upd = _dot(jnp.where(seq_of_lane == s, kd_t[lv], 0.0), v_new_bf)
                    ss_ref[s, vcols[lv], :] = s0_ref[s, vcols[lv], :] * dec + upd
        else:
            s_old = [s_scr[lv] for lv in range(nv)]
            wq = [_dot(jnp.concatenate([w[lv], q_dec[lv]], axis=0), s_old[lv]) for lv in range(nv)]
            v_new = [u[lv] - wq[lv][:CHUNK_T] for lv in range(nv)]
            o = [wq[lv][CHUNK_T:] + _dot(qk_d[lv], v_new[lv]) for lv in range(nv)]
            for lv in range(nv):
                s_new = s_old[lv] * jnp.exp(gl_c[lv]) + _dot(kd_t[lv], v_new[lv])
                s_scr[lv] = s_new
                sp_ref[vcols[lv], :] = s_new

        for lv in range(nv):
            y = _rmsnorm_rows(o[lv], nw) * jax.nn.silu(z_ref[:, vcols[lv]])
            y_ref[:, vcols[lv]] = y.astype(y_ref.dtype)

    if sample:
        real = _real_row_mask((CHUNK_T, 1), 0)
        _zero_conv_history((bq, bk, bv))
        body(jnp.where(real, q_ref[...], cq_ref[...]), jnp.where(real, k_ref[...], ck_ref[...]),
             jnp.where(real, v_ref[...], cv_ref[...]))
    else:
        @pl.when(pl.program_id(1) % chunks_per_seq == 0)
        def _():
            _zero_conv_history((bq, bk, bv))
            s_scr[...] = jnp.zeros(s_scr.shape, F32)
        body(q_ref[...], k_ref[...], v_ref[...])


class _Dims(NamedTuple):
    n_prompt_seqs: int
    chunks_per_seq: int
    n_sample_seqs: int

    @property
    def n_prompt_chunks(self):
        return self.n_prompt_seqs * self.chunks_per_seq

    @property
    def n_steps(self):
        return self.n_prompt_chunks + self.n_sample_seqs // SEQS_PER_CHUNK

    @property
    def n_rows(self):
        return self.n_steps * CHUNK_T


def _gdn_mixer(proj, gates, conv_w, norm_w, cache8, states, layer, prev, dims):
    beta, gc, gct = gates
    hb = GDN_HEAD
    n_pc = dims.n_prompt_chunks
    n_layers, n_seq = states.shape[:2]
    state_rows = GDN_V_HEADS * hb
    shift = _conv_shift_matrix()
    y_shape = jax.ShapeDtypeStruct((dims.n_rows, GDN_V_W), BF16)

    def call(sample, heads, first_step, n_steps, extra_specs, extra_args, state_spec, state_shape,
             aliased):
        qw, vw = heads * hb, 2 * heads * hb
        qkb = GDN_QK_W // qw
        vb = (2 * GDN_QK_W) // vw
        zb = GDN_CONV_DIM // vw
        gct_rows = max(SUBLANES, 2 * heads)

        def spec(width, col_off, row_off=first_step):
            return pl.BlockSpec((CHUNK_T, width), lambda g, t: (row_off + t, col_off + g))

        row128 = pl.BlockSpec((CHUNK_T, LANES), lambda g, t: (first_step + t, 0))
        in_specs = [pl.BlockSpec(shift.shape, lambda g, t: (0, 0)),
                    spec(qw, 0), spec(qw, qkb), spec(vw, vb), spec(vw, zb), row128, row128,
                    pl.BlockSpec((gct_rows, CHUNK_T),
                                 lambda g, t: ((2 * heads * g) // gct_rows, first_step + t)),
                    pl.BlockSpec((CONV_W, qw), lambda g, t: (0, g)),
                    pl.BlockSpec((CONV_W, qw), lambda g, t: (0, qkb + g)),
                    pl.BlockSpec((CONV_W, vw), lambda g, t: (0, vb + g)),
                    pl.BlockSpec((1, hb), lambda g, t: (0, 0))]
        args = [shift, proj, proj, proj, proj, beta, gc, gct, conv_w, conv_w, conv_w,
                norm_w.reshape(1, hb)]
        in_specs += extra_specs(spec, vw) + [pl.BlockSpec(memory_space=pl.ANY)] * len(aliased)
        args += extra_args
        n_in = len(args)
        return pl.pallas_call(
            functools.partial(_gdn_kernel, heads=heads, sample=sample,
                              chunks_per_seq=dims.chunks_per_seq, n_prev=len(aliased)),
            grid=(GDN_QK_HEADS // heads, n_steps),
            in_specs=in_specs,
            out_specs=[pl.BlockSpec((CHUNK_T, vw), lambda g, t: (first_step + t, g)), state_spec(vw)],
            out_shape=[y_shape, state_shape],
            input_output_aliases={n_in + i: out for i, out in enumerate(sorted(aliased))},
            scratch_shapes=[pltpu.VMEM((CONV_HIST, qw), F32), pltpu.VMEM((CONV_HIST, qw), F32),
                            pltpu.VMEM((CONV_HIST, vw), F32), pltpu.VMEM((2 * heads, hb, hb), F32)],
            compiler_params=_cparams(("parallel", "arbitrary")),
            name="gdn_sample" if sample else "gdn_prompt",
        )(*args, *[aliased[k] for k in sorted(aliased)])

    prev_p, prev_s = prev if prev else (None, None)
    cps = dims.chunks_per_seq
    y, sp = call(
        False, GDN_HEADS_PER_PROMPT_STEP, 0, n_pc, lambda spec, vw: [], [],
        lambda vw: pl.BlockSpec((None, None, vw, hb), lambda g, t: (layer, t // cps, g, 0)),
        jax.ShapeDtypeStruct((n_layers, dims.n_prompt_seqs, state_rows, hb), F32),
        {} if prev_p is None else {1: prev_p})
    s_block = lambda vw: pl.BlockSpec((None, SEQS_PER_CHUNK, vw, hb), lambda g, t: (layer, t, g, 0))
    y, ss = call(
        True, GDN_HEADS_PER_STEP, n_pc, n_seq // SEQS_PER_CHUNK,
        lambda spec, vw: [spec(vw // 2, 0, 0), spec(vw // 2, GDN_QK_W // (vw // 2), 0),
                          spec(vw, (2 * GDN_QK_W) // vw, 0), s_block(vw)],
        [cache8, cache8, cache8, states], s_block,
        jax.ShapeDtypeStruct((n_layers, n_seq, state_rows, hb), F32),
        {0: y} if prev_s is None else {0: y, 1: prev_s})
    return y, sp, ss


def _ssd_kernel(*refs, groups, sample, chunks_per_seq, n_prev):
    (shift_ref, z_ref, x_ref, b_ref, c_ref, dt_ref, acs_ref, acst_ref, cwx_ref, cwb_ref, cwc_ref,
     cbx_ref, cbb_ref, cbc_ref, de_ref, nw_ref) = refs[:16]
    n_in = 16
    if sample:
        cx_ref, cb_ref, cc_ref, s0_ref = refs[16:20]
        n_in = 20
    y_ref, st_ref, bx, bb, bc, s_scr = refs[n_in + n_prev:]
    ss_ref = sp_ref = st_ref
    grp = pl.program_id(0)
    t = CHUNK_T
    gw, n = SSD_GW, SSD_STATE
    gcols = [slice(gi * gw, (gi + 1) * gw) for gi in range(groups)]
    ncols = [slice(gi * n, (gi + 1) * n) for gi in range(groups)]
    n_heads = groups * SSD_HPG

    def scaled_by_head(s_mat, col):
        eb = jnp.broadcast_to(jnp.exp(acst_ref[:, col:col + 1]), (n_heads, n))
        return jnp.concatenate([s_mat[r * SSD_HEAD:(r + 1) * SSD_HEAD] * eb[r:r + 1, :]
                                for r in range(s_mat.shape[0] // SSD_HEAD)], axis=0)

    def body(xx, xb, xc):
        xg = jax.nn.silu(_causal_conv(bx, xx, cwx_ref, shift_ref) + cbx_ref[...])
        bg = jax.nn.silu(_causal_conv(bb, xb, cwb_ref, shift_ref) + cbb_ref[...])
        cg = jax.nn.silu(_causal_conv(bc, xc, cwc_ref, shift_ref) + cbc_ref[...])

        dt_tile, acs_tile = dt_ref[...], acs_ref[...]
        sel_shape = (LANES, groups * gw)
        sel = _iota2(sel_shape, 0) == grp * n_heads + _iota2(sel_shape, 1) // SSD_HEAD
        parts = jnp.concatenate(_split3(dt_tile) + _split3(acs_tile), axis=0)
        ex = jnp.dot(parts, sel.astype(BF16), preferred_element_type=F32)
        dt_e = ex[0:t] + ex[t:2 * t] + ex[2 * t:3 * t]
        acs_e = ex[3 * t:4 * t] + ex[4 * t:5 * t] + ex[5 * t:6 * t]
        acs_last_e = _seq_last_rows(acs_e) if sample else acs_e[t - 1:t, :]

        xdt = xg * dt_e
        incl, _ = _chunk_masks(sample)
        cbs = [_dot_nt(cg[:, ncols[gi]], bg[:, ncols[gi]]) for gi in range(groups)]
        acs_c = [_pick_lane(acs_tile, grp * n_heads + r) for r in range(n_heads)]
        ms_ = [cbs[r // SSD_HPG] * jnp.exp(jnp.where(incl, acs_c[r] - acst_ref[r:r + 1, :], NEG_BIG))
               for r in range(n_heads)]
        lane_lo = _iota2((t, 2 * SSD_HEAD), 1) < SSD_HEAD
        pairs = []
        for pr in range(n_heads // 2):
            xpair = xdt[:, pr * 2 * SSD_HEAD:(pr + 1) * 2 * SSD_HEAD]
            rhs = jnp.concatenate([jnp.where(lane_lo, xpair, 0.0), jnp.where(lane_lo, 0.0, xpair)],
                                  axis=0)
            pairs.append(_dot(jnp.concatenate([ms_[2 * pr], ms_[2 * pr + 1]], axis=1), rhs))
        y_diag = jnp.concatenate(pairs, axis=1)

        xd_t = (xdt * jnp.exp(acs_last_e - acs_e)).T
        if sample:
            offs = []
            zeros = jnp.zeros((SEQ_PAD, n), F32)
            seq_of_lane = _iota2((groups * gw, t), 1) // SEQ_PAD
            bg_bf = bg.astype(BF16)
            for s in range(SEQS_PER_CHUNK):
                rows = slice(s * SEQ_PAD, (s + 1) * SEQ_PAD)
                s_old = s0_ref[s]
                offs.append(jnp.concatenate(
                    [_dot_nt(jnp.concatenate([cg[rows, ncols[gi]], zeros], axis=0),
                             s_old[gcols[gi]])[:SEQ_PAD] for gi in range(groups)], axis=1))
                xd_s = jnp.where(seq_of_lane == s, xd_t, 0.0)
                upd = jnp.concatenate([_dot(xd_s[gcols[gi]], bg_bf[:, ncols[gi]])
                                       for gi in range(groups)], axis=0)
                ss_ref[s] = scaled_by_head(s_old, (s + 1) * SEQ_PAD - 1) + upd
            y_off = jnp.concatenate(offs, axis=0)
        else:
            s_old = s_scr[...]
            y_off = jnp.concatenate([_dot_nt(cg[:, ncols[gi]], s_old[gcols[gi]])
                                     for gi in range(groups)], axis=1)
            upd = jnp.concatenate([_dot(xd_t[gcols[gi]], bg[:, ncols[gi]]) for gi in range(groups)],
                                  axis=0)
            s_new = scaled_by_head(s_old, t - 1) + upd
            s_scr[...] = s_new
            sp_ref[...] = s_new

        y = y_diag + y_off * jnp.exp(acs_e) + de_ref[...] * xg
        y = y * jax.nn.silu(z_ref[...])
        nw = nw_ref[...]
        for gi in range(groups):
            yg = y[:, gcols[gi]]
            yg = yg * lax.rsqrt(jnp.mean(yg * yg, axis=-1, keepdims=True) + EPS) * nw[:, gcols[gi]]
            y_ref[:, gcols[gi]] = yg.astype(y_ref.dtype)

    if sample:
        real = _real_row_mask((CHUNK_T, 1), 0)
        _zero_conv_history((bx, bb, bc))
        body(jnp.where(real, x_ref[...], cx_ref[...]), jnp.where(real, b_ref[...], cb_ref[...]),
             jnp.where(real, c_ref[...], cc_ref[...]))
    else:
        @pl.when(pl.program_id(1) % chunks_per_seq == 0)
        def _():
            _zero_conv_history((bx, bb, bc))
            s_scr[...] = jnp.zeros(s_scr.shape, F32)
        body(x_ref[...], b_ref[...], c_ref[...])


def _ssd_mixer(proj, gates, conv_w, conv_b, d_skip, norm_w, cache8, states, layer, prev, dims):
    dt, acs, acst = gates
    n_pc = dims.n_prompt_chunks
    n_layers, n_seq = states.shape[:2]
    state_rows = SSD_HEADS * SSD_HEAD
    shift = _conv_shift_matrix()
    conv_b2 = conv_b.reshape(1, -1)
    d_e = jnp.repeat(d_skip, SSD_HEAD).reshape(1, SSD_D_INNER)
    nw2 = norm_w.reshape(1, SSD_D_INNER)

    def call(sample, groups, first_step, n_steps, extra_specs, extra_args, state_spec, state_shape,
             aliased):
        gw, n = groups * SSD_GW, groups * SSD_STATE
        xb = SSD_D_INNER // gw
        bb = (2 * SSD_D_INNER) // n
        cb = bb + SSD_GN // n
        wbb = SSD_D_INNER // n
        wcb = wbb + SSD_GN // n

        def spec(width, col_off, row_off=first_step):
            return pl.BlockSpec((CHUNK_T, width), lambda g, t: (row_off + t, col_off + g))

        def wspec(rows, width, col_off):
            return pl.BlockSpec((rows, width), lambda g, t: (0, col_off + g))

        row128 = pl.BlockSpec((CHUNK_T, LANES), lambda g, t: (first_step + t, 0))
        in_specs = [pl.BlockSpec(shift.shape, lambda g, t: (0, 0)),
                    spec(gw, 0), spec(gw, xb), spec(n, bb), spec(n, cb), row128, row128,
                    pl.BlockSpec((groups * SSD_HPG, CHUNK_T), lambda g, t: (g, first_step + t)),
                    wspec(CONV_W, gw, 0), wspec(CONV_W, n, wbb), wspec(CONV_W, n, wcb),
                    wspec(1, gw, 0), wspec(1, n, wbb), wspec(1, n, wcb),
                    wspec(1, gw, 0), wspec(1, gw, 0)]
        args = [shift, proj, proj, proj, proj, dt, acs, acst, conv_w, conv_w, conv_w,
                conv_b2, conv_b2, conv_b2, d_e, nw2]
        in_specs += (extra_specs(spec, gw, n, wbb, wcb)
                     + [pl.BlockSpec(memory_space=pl.ANY)] * len(aliased))
        args += extra_args
        n_in = len(args)
        return pl.pallas_call(
            functools.partial(_ssd_kernel, groups=groups, sample=sample,
                              chunks_per_seq=dims.chunks_per_seq, n_prev=len(aliased)),
            grid=(SSD_GROUPS // groups, n_steps),
            in_specs=in_specs,
            out_specs=[pl.BlockSpec((CHUNK_T, gw), lambda g, t: (first_step + t, g)), state_spec(gw)],
            out_shape=[jax.ShapeDtypeStruct((dims.n_rows, SSD_D_INNER), BF16), state_shape],
            input_output_aliases={n_in + i: out for i, out in enumerate(sorted(aliased))},
            scratch_shapes=[pltpu.VMEM((CONV_HIST, gw), F32), pltpu.VMEM((CONV_HIST, n), F32),
                            pltpu.VMEM((CONV_HIST, n), F32), pltpu.VMEM((gw, SSD_STATE), F32)],
            compiler_params=_cparams(("parallel", "arbitrary")),
            name="ssd_sample" if sample else "ssd_prompt",
        )(*args, *[aliased[k] for k in sorted(aliased)])

    prev_p, prev_s = prev if prev else (None, None)
    cps = dims.chunks_per_seq
    y, sp = call(
        False, SSD_GROUPS_PER_PROMPT_STEP, 0, n_pc, lambda *a: [], [],
        lambda gw: pl.BlockSpec((None, None, gw, SSD_STATE), lambda g, t: (layer, t // cps, g, 0)),
        jax.ShapeDtypeStruct((n_layers, dims.n_prompt_seqs, state_rows, SSD_STATE), F32),
        {} if prev_p is None else {1: prev_p})
    s_block = lambda gw: pl.BlockSpec((None, SEQS_PER_CHUNK, gw, SSD_STATE),
                                      lambda g, t: (layer, t, g, 0))
    y, ss = call(
        True, SSD_GROUPS_PER_STEP, n_pc, n_seq // SEQS_PER_CHUNK,
        lambda spec, gw, n, wbb, wcb: [spec(gw, 0, 0), spec(n, wbb, 0), spec(n, wcb, 0), s_block(gw)],
        [cache8, cache8, cache8, states], s_block,
        jax.ShapeDtypeStruct((n_layers, n_seq, state_rows, SSD_STATE), F32),
        {0: y} if prev_s is None else {0: y, 1: prev_s})
    return y, sp, ss


def _pad_sequences(x, n_real):
    pads = [(0, 0)] * (x.ndim - 2) + [(SEQ_PAD - n_real, 0), (0, 0)]
    xp = jnp.pad(x, pads)
    return xp.reshape(x.shape[:-3] + (x.shape[-3] * SEQ_PAD, x.shape[-1]))


def _history_rows(cache):
    n_real = SEQ_PAD // 2
    lo = SEQ_PAD - n_real - (CONV_W - 1)
    xp = jnp.pad(cache, ((0, 0), (lo, n_real), (0, 0)))
    return xp.reshape(cache.shape[0] * SEQ_PAD, cache.shape[2])


def kernel(x_prompt, x_sample, state_gdn, cache_gdn_conv, state_ssd, cache_ssd_conv, p_prompt, p_sample,
           norm_w, gdn_w_in, gdn_conv_w, gdn_A_log, gdn_dt_bias, gdn_norm_w, gdn_w_out,
           ssd_w_in, ssd_conv_w, ssd_conv_b, ssd_A_log, ssd_dt_bias, ssd_D, ssd_norm_w, ssd_w_out,
           ple_w_proj, ple_w_gate, ple_norm_w, final_norm_w):
    bp, lp, d = x_prompt.shape
    bs, ls, _ = x_sample.shape
    depth = norm_w.shape[0]
    assert d == D_MODEL and ls == SEQ_PAD // 2 and lp % CHUNK_T == 0 and bs % SEQS_PER_CHUNK == 0
    mp, ms = bp * lp, bs * SEQ_PAD
    assert mp % ROW_TILE == 0 and ms % ROW_TILE == 0

    h = jnp.concatenate([x_prompt.reshape(mp, d), _pad_sequences(x_sample, ls)], axis=0)
    p = jnp.concatenate([p_prompt.reshape(depth, mp, PLE_DIM), _pad_sequences(p_sample, ls)], axis=1)

    def conv_caches(proj, lo, hi):
        c_p = jnp.stack([proj[(b + 1) * lp - (CONV_W - 1):(b + 1) * lp, lo:hi] for b in range(bp)])
        c_s = proj[mp:, lo:hi].reshape(bs, SEQ_PAD, hi - lo)[:, SEQ_PAD - (CONV_W - 1):, :]
        return c_p, c_s

    dims = _Dims(n_prompt_seqs=bp, chunks_per_seq=lp // CHUNK_T, n_sample_seqs=bs)
    gdn_states = state_gdn.reshape(state_gdn.shape[0], bs, GDN_V_HEADS * GDN_HEAD, GDN_HEAD)
    ssd_states = state_ssd.reshape(state_ssd.shape[0], bs, SSD_HEADS * SSD_HEAD, SSD_STATE)
    gdn_w_in_t = jnp.swapaxes(gdn_w_in, 1, 2)
    ssd_w_in_t = jnp.swapaxes(ssd_w_in, 1, 2)
    w_gate_bf = ple_w_gate.astype(BF16)
    caches = {k: [] for k in ("cg_p", "cs_p", "cg_s", "cs_s")}
    gdn_prev, ssd_prev = (), ()
    hn = _norm_rows(h, norm_w[0], BF16)
    for i in range(depth):
        j = i // 2
        if i % 2 == 0:
            proj = _matmul(hn, gdn_w_in_t, j, GDN_MAIN, "plain", w_nk=True)
            gates = _gdn_gates(hn, gdn_w_in[j, :, GDN_MAIN:], gdn_A_log[j], gdn_dt_bias[j], mp)
            y, *gdn_prev = _gdn_mixer(proj, gates, gdn_conv_w[j], gdn_norm_w[j],
                                      _history_rows(cache_gdn_conv[j]), gdn_states, j, gdn_prev, dims)
            c_p, c_s = conv_caches(proj, 0, GDN_CONV_DIM)
            w_out = gdn_w_out
            keys = ("cg_p", "cg_s")
        else:
            proj = _matmul(hn, ssd_w_in_t, j, SSD_MAIN, "plain", w_nk=True)
            gates = _ssd_gates(hn, ssd_w_in[j, :, SSD_MAIN:], ssd_A_log[j], ssd_dt_bias[j], mp)
            y, *ssd_prev = _ssd_mixer(proj, gates, ssd_conv_w[j], ssd_conv_b[j], ssd_D[j], ssd_norm_w[j],
                                      _history_rows(cache_ssd_conv[j]), ssd_states, j, ssd_prev, dims)
            c_p, c_s = conv_caches(proj, SSD_D_INNER, SSD_MAIN)
            w_out = ssd_w_out
            keys = ("cs_p", "cs_s")
        for key, val in zip(keys, (c_p, c_s)):
            caches[key].append(val)
        h_mid, h_mid_bf = _matmul(y, w_out, j, D_MODEL, "residual", extras=(h,))
        if i + 1 < depth:
            h, hn = _ple_add(h_mid_bf, h_mid, p, w_gate_bf, ple_w_proj, i, ple_norm_w[i], norm_w[i + 1])
        else:
            y_p, y_s = _ple_add(h_mid_bf, h_mid, p, w_gate_bf, ple_w_proj, i, ple_norm_w[i],
                                final_norm_w, mp)

    y_prompt = y_p.reshape(bp, lp, d)
    y_sample = y_s.reshape(bs, SEQ_PAD, d)[:, SEQ_PAD - ls:, :]
    cc = {k: jnp.stack(v) for k, v in caches.items()}
    sg_p, sg_s = (s.reshape(s.shape[:2] + (GDN_V_HEADS, GDN_HEAD, GDN_HEAD)) for s in gdn_prev)
    ss_p, ss_s = (s.reshape(s.shape[:2] + (SSD_HEADS, SSD_HEAD, SSD_STATE)) for s in ssd_prev)
    return (y_prompt, y_sample, sg_p, cc["cg_p"], ss_p, cc["cs_p"], sg_s, cc["cg_s"], ss_s, cc["cs_s"])
```

```python
import functools
from typing import NamedTuple

import jax
import jax.numpy as jnp
import numpy as np
from jax import lax
from jax.experimental import pallas as pl
from jax.experimental.pallas import tpu as pltpu

F32 = jnp.float32
BF16 = jnp.bfloat16

EPS = 1e-6
CONV_W = 4
D_MODEL = 2048
PLE_DIM = 256

GDN_HEAD = 128
GDN_QK_HEADS = 16
GDN_V_HEADS = 32
GDN_QK_W = GDN_QK_HEADS * GDN_HEAD
GDN_V_W = GDN_V_HEADS * GDN_HEAD
GDN_CONV_DIM = 2 * GDN_QK_W + GDN_V_W
GDN_MAIN = GDN_CONV_DIM + GDN_V_W

SSD_D_INNER = 4096
SSD_HEAD = 64
SSD_HEADS = 64
SSD_STATE = 128
SSD_GROUPS = 8
SSD_HPG = SSD_HEADS // SSD_GROUPS
SSD_GN = SSD_GROUPS * SSD_STATE
SSD_CONV_DIM = SSD_D_INNER + 2 * SSD_GN
SSD_MAIN = SSD_D_INNER + SSD_CONV_DIM
SSD_GW = SSD_HPG * SSD_HEAD

LANES = 128
SUBLANES = 8
CHUNK_T = 128
SEQ_PAD = 8
CONV_HIST = 16
SEQS_PER_CHUNK = CHUNK_T // SEQ_PAD
INV_BLOCK = 16
GDN_HEADS_PER_STEP = 4
GDN_HEADS_PER_PROMPT_STEP = 8
SSD_GROUPS_PER_STEP = 2
SSD_GROUPS_PER_PROMPT_STEP = 4
ROW_TILE = 512
MM_TILES = {"plain": (1024, 2048), "residual": (512, 1024)}
VMEM_LIMIT = 56 * 1024 * 1024
NEG_BIG = -1e30


def _cparams(sem):
    return pltpu.CompilerParams(dimension_semantics=sem, vmem_limit_bytes=VMEM_LIMIT)


def _dot(a, b):
    return jnp.dot(a.astype(BF16), b.astype(BF16), preferred_element_type=F32)


def _dot_nt(a, b):
    return lax.dot_general(a.astype(BF16), b.astype(BF16), (((1,), (1,)), ((), ())),
                           preferred_element_type=F32)


def _split3(x):
    x1 = x.astype(BF16)
    r1 = x - x1.astype(F32)
    x2 = r1.astype(BF16)
    r2 = r1 - x2.astype(F32)
    return [x1, x2, r2.astype(BF16)]


def _softplus(x):
    return jnp.maximum(x, 0.0) + jnp.log1p(jnp.exp(-jnp.abs(x)))


def _iota2(shape, dim):
    return lax.broadcasted_iota(jnp.int32, shape, dim)


def _pick_lane(x, idx):
    lane = _iota2(x.shape, 1)
    return jnp.sum(jnp.where(lane == idx, x, 0.0), axis=1, keepdims=True)


def _chunk_masks(sample):
    ii = _iota2((CHUNK_T, CHUNK_T), 0)
    jj = _iota2((CHUNK_T, CHUNK_T), 1)
    incl = ii >= jj
    strict = ii > jj
    if sample:
        same = (ii // SEQ_PAD) == (jj // SEQ_PAD)
        incl = incl & same
        strict = strict & same
    return incl, strict


def _seq_last_rows(x):
    rows, cols = x.shape
    x3 = x.reshape(rows // SEQ_PAD, SEQ_PAD, cols)
    last = jnp.broadcast_to(x3[:, SEQ_PAD - 1:SEQ_PAD, :], x3.shape)
    return last.reshape(rows, cols)


def _unit_lower_inverses(mats, sample):
    ii = _iota2(mats[0].shape, 0)
    jj = _iota2(mats[0].shape, 1)
    eye = (ii == jj).astype(F32)

    def neumann(xs, squarings):
        ts = [eye - x for x in xs]
        ps = xs
        for _ in range(squarings):
            ps = [_dot(p, p) for p in ps]
            ts = [t + _dot(t, p) for t, p in zip(ts, ps)]
        return ts

    if sample:
        assert SEQ_PAD == 8
        return neumann(mats, 2)
    assert INV_BLOCK == 16 and CHUNK_T // INV_BLOCK == 8
    blk = (ii // INV_BLOCK) == (jj // INV_BLOCK)
    ds = [jnp.where(blk, a, 0.0) for a in mats]
    tds = neumann(ds, 3)
    ns = [_dot(td, a - d) for td, a, d in zip(tds, mats, ds)]
    return [_dot(p, td) for p, td in zip(neumann(ns, 2), tds)]


def _conv_shift_matrix():
    r = np.arange((CONV_W - 1) * CHUNK_T)[:, None]
    c = np.arange(CONV_HIST + CHUNK_T)[None, :]
    return jnp.asarray(c == CONV_HIST - (CONV_W - 1) + r % CHUNK_T + r // CHUNK_T, dtype=BF16)


def _causal_conv(hist_ref, x, w_ref, shift_ref):
    t = x.shape[0]
    xb = jnp.concatenate([hist_ref[...], x], axis=0).astype(BF16)
    taps = jnp.dot(shift_ref[...], xb, preferred_element_type=F32)
    w = w_ref[...]
    y = taps[0:t] * w[0:1, :]
    for j in range(1, CONV_W - 1):
        y = y + taps[j * t:(j + 1) * t] * w[j:j + 1, :]
    y = y + x * w[CONV_W - 1:CONV_W, :]
    hist_ref[...] = x[t - CONV_HIST:, :]
    return y


def _zero_conv_history(bufs):
    for b in bufs:
        b[...] = jnp.zeros(b.shape, F32)


def _real_row_mask(shape, dim):
    return (_iota2(shape, dim) % SEQ_PAD) >= (SEQ_PAD // 2)


def _rmsnorm_rows(x, w):
    return x * lax.rsqrt(jnp.mean(x * x, axis=-1, keepdims=True) + EPS) * w


def _norm_kernel(x_ref, w_ref, o_ref):
    o_ref[...] = _rmsnorm_rows(x_ref[...], w_ref[...]).astype(o_ref.dtype)


def _norm_rows(x, w, out_dtype):
    m, d = x.shape
    return pl.pallas_call(
        _norm_kernel,
        grid=(m // ROW_TILE,),
        in_specs=[pl.BlockSpec((ROW_TILE, d), lambda i: (i, 0)),
                  pl.BlockSpec((1, d), lambda i: (0, 0))],
        out_specs=pl.BlockSpec((ROW_TILE, d), lambda i: (i, 0)),
        out_shape=jax.ShapeDtypeStruct((m, d), out_dtype),
        compiler_params=_cparams(("parallel",)),
        name="norm_rows",
    )(x, w.reshape(1, d))


def _ple_kernel(wg_ref, h_ref, p_ref, wp_ref, pnw_ref, nnw_ref, oa_ref, ob_ref, *, n_prompt_tiles):
    h_in = h_ref[...]
    gate = jax.nn.sigmoid(jnp.dot(h_in.astype(BF16), wg_ref[...], preferred_element_type=F32))
    e = _rmsnorm_rows(_dot(p_ref[...], wp_ref[...]), pnw_ref[...])
    h = h_in + e * gate
    hn = _rmsnorm_rows(h, nnw_ref[...])
    if n_prompt_tiles is None:
        oa_ref[...] = h
        ob_ref[...] = hn.astype(ob_ref.dtype)
    else:
        is_prompt = pl.program_id(0) < n_prompt_tiles

        @pl.when(is_prompt)
        def _():
            oa_ref[...] = hn

        @pl.when(jnp.logical_not(is_prompt))
        def _():
            ob_ref[...] = hn


def _ple_add(h, p, w_gate_bf, w_proj, layer, ple_norm_w, next_norm_w, n_prompt_rows=None):
    m = h.shape[0]
    row = lambda width: pl.BlockSpec((ROW_TILE, width), lambda i: (i, 0))
    vec = pl.BlockSpec((1, D_MODEL), lambda i: (0, 0))
    once = dict(pipeline_mode=pl.Buffered(1))
    if n_prompt_rows is None:
        n_pt = None
        out_specs = [row(D_MODEL), row(D_MODEL)]
        out_shape = [jax.ShapeDtypeStruct((m, D_MODEL), F32), jax.ShapeDtypeStruct((m, D_MODEL), BF16)]
    else:
        n_pt = n_prompt_rows // ROW_TILE
        out_specs = [pl.BlockSpec((ROW_TILE, D_MODEL), lambda i: (jnp.minimum(i, n_pt - 1), 0)),
                     pl.BlockSpec((ROW_TILE, D_MODEL), lambda i: (jnp.maximum(i - n_pt, 0), 0))]
        out_shape = [jax.ShapeDtypeStruct((n_prompt_rows, D_MODEL), F32),
                     jax.ShapeDtypeStruct((m - n_prompt_rows, D_MODEL), F32)]
    return pl.pallas_call(
        functools.partial(_ple_kernel, n_prompt_tiles=n_pt),
        grid=(m // ROW_TILE,),
        in_specs=[pl.BlockSpec((None, D_MODEL, D_MODEL), lambda i: (layer, 0, 0), **once),
                  row(D_MODEL),
                  pl.BlockSpec((None, ROW_TILE, PLE_DIM), lambda i: (layer, i, 0)),
                  pl.BlockSpec((None, PLE_DIM, D_MODEL), lambda i: (layer, 0, 0), **once),
                  vec, vec],
        out_specs=out_specs,
        out_shape=out_shape,
        compiler_params=_cparams(("arbitrary",)),
        name="ple_add",
    )(w_gate_bf, h, p, w_proj, ple_norm_w.reshape(1, D_MODEL), next_norm_w.reshape(1, D_MODEL))


def _mm_kernel(x_ref, w_ref, *rest, epilogue, w_nk):
    wbf_ref = rest[-1]
    refs = rest[:-1]

    @pl.when(pl.program_id(1) == 0)
    def _():
        wbf_ref[...] = w_ref[...].astype(BF16)

    x = x_ref[...]
    acc = _dot_nt(x, wbf_ref[...]) if w_nk else jnp.dot(x, wbf_ref[...], preferred_element_type=F32)
    if epilogue == "plain":
        (o_ref,) = refs
        o_ref[...] = acc
    elif epilogue == "residual":
        res_ref, o_ref = refs
        o_ref[...] = res_ref[...] + acc
    else:
        raise ValueError(epilogue)


def _matmul(x, w, layer, n_cols, epilogue, extras=(), w_nk=False):
    m, k = x.shape
    tm, tn = MM_TILES[epilogue]
    while m % tm:
        tm //= 2
    assert n_cols % tn == 0
    tile = pl.BlockSpec((tm, tn), lambda j, i: (i, j))
    w_block = (None, tn, k) if w_nk else (None, k, tn)
    w_index = (lambda j, i: (layer, j, 0)) if w_nk else (lambda j, i: (layer, 0, j))
    w_spec = pl.BlockSpec(w_block, w_index, pipeline_mode=pl.Buffered(1))
    in_specs = [pl.BlockSpec((tm, k), lambda j, i: (i, 0)), w_spec] + [tile] * len(extras)
    return pl.pallas_call(
        functools.partial(_mm_kernel, epilogue=epilogue, w_nk=w_nk),
        grid=(n_cols // tn, m // tm),
        in_specs=in_specs,
        out_specs=tile,
        out_shape=jax.ShapeDtypeStruct((m, n_cols), F32),
        scratch_shapes=[pltpu.VMEM(w_block[1:], BF16)],
        compiler_params=_cparams(("parallel", "arbitrary")),
        name="matmul_" + epilogue,
    )(x, w, *extras)


def _dot_f32(a, b):
    return jnp.dot(a, b, preferred_element_type=F32, precision=lax.Precision.HIGHEST)


def _chunk_cumsums(a, n_heads, is_prompt):
    shift = jnp.where(is_prompt, CHUNK_T.bit_length() - 1, SEQ_PAD.bit_length() - 1)
    ii = _iota2((CHUNK_T, CHUNK_T), 0)
    jj = _iota2((CHUNK_T, CHUNK_T), 1)
    same = lax.shift_right_logical(ii, shift) == lax.shift_right_logical(jj, shift)
    lower = (same & (jj <= ii)).astype(F32)
    upper = (same & (ii <= jj)).astype(F32)
    a_t = a.T[:n_heads]
    blocks = [slice(b * CHUNK_T, (b + 1) * CHUNK_T) for b in range(ROW_TILE // CHUNK_T)]
    sums = jnp.concatenate([_dot_f32(lower, a[b]) for b in blocks], axis=0)
    sums_t = jnp.concatenate([_dot_f32(a_t[:, b], upper) for b in blocks], axis=1)
    return sums, sums_t


def _gdn_gate_kernel(x_ref, wa_ref, wb_ref, alog_ref, dtb_ref, beta_ref, gc_ref, gct_ref, *,
                     n_prompt_tiles):
    x = x_ref[...]
    is_prompt = pl.program_id(0) < n_prompt_tiles
    live_rows = is_prompt | _real_row_mask((ROW_TILE, LANES), 0)
    a = _dot(x, wa_ref[...])
    g = jnp.where(live_rows, -jnp.exp(alog_ref[...]) * _softplus(a + dtb_ref[...]), 0.0)
    beta_ref[...] = jnp.where(live_rows, jax.nn.sigmoid(_dot(x, wb_ref[...])), 0.0)
    gc_ref[...], gct_ref[...] = _chunk_cumsums(g, GDN_V_HEADS, is_prompt)


def _pad_lanes(x):
    return jnp.pad(x, ((0, 0), (0, LANES - x.shape[1])))


def _gdn_gates(hn, w_ab, a_log, dt_bias, n_prompt_rows):
    m = hn.shape[0]
    wa = w_ab[:, :GDN_V_HEADS]
    wb = w_ab[:, GDN_V_HEADS:]
    row = pl.BlockSpec((ROW_TILE, LANES), lambda i: (i, 0))
    full = lambda shape: pl.BlockSpec(shape, lambda i: (0, 0))
    return pl.pallas_call(
        functools.partial(_gdn_gate_kernel, n_prompt_tiles=n_prompt_rows // ROW_TILE),
        grid=(m // ROW_TILE,),
        in_specs=[pl.BlockSpec((ROW_TILE, D_MODEL), lambda i: (i, 0)),
                  full((D_MODEL, LANES)), full((D_MODEL, LANES)), full((1, LANES)), full((1, LANES))],
        out_specs=[row, row, pl.BlockSpec((GDN_V_HEADS, ROW_TILE), lambda i: (0, i))],
        out_shape=[jax.ShapeDtypeStruct((m, LANES), F32)] * 2
        + [jax.ShapeDtypeStruct((GDN_V_HEADS, m), F32)],
        compiler_params=_cparams(("parallel",)),
        name="gdn_gates",
    )(hn, _pad_lanes(wa), _pad_lanes(wb),
      _pad_lanes(a_log.reshape(1, -1)), _pad_lanes(dt_bias.reshape(1, -1)))


def _ssd_gate_kernel(x_ref, w_ref, alog_ref, dtb_ref, dt_ref, acs_ref, acst_ref, *, n_prompt_tiles):
    x = x_ref[...]
    is_prompt = pl.program_id(0) < n_prompt_tiles
    live_rows = is_prompt | _real_row_mask((ROW_TILE, LANES), 0)
    dt = jnp.where(live_rows, _softplus(_dot(x, w_ref[...]) + dtb_ref[...]), 0.0)
    dt_ref[...] = dt
    acs_ref[...], acst_ref[...] = _chunk_cumsums(dt * -jnp.exp(alog_ref[...]), SSD_HEADS, is_prompt)


def _ssd_gates(hn, w, a_log, dt_bias, n_prompt_rows):
    m = hn.shape[0]
    row = pl.BlockSpec((ROW_TILE, LANES), lambda i: (i, 0))
    full = lambda shape: pl.BlockSpec(shape, lambda i: (0, 0))
    return pl.pallas_call(
        functools.partial(_ssd_gate_kernel, n_prompt_tiles=n_prompt_rows // ROW_TILE),
        grid=(m // ROW_TILE,),
        in_specs=[pl.BlockSpec((ROW_TILE, D_MODEL), lambda i: (i, 0)),
                  full((D_MODEL, LANES)), full((1, LANES)), full((1, LANES))],
        out_specs=[row, row, pl.BlockSpec((SSD_HEADS, ROW_TILE), lambda i: (0, i))],
        out_shape=[jax.ShapeDtypeStruct((m, LANES), F32)] * 2
        + [jax.ShapeDtypeStruct((SSD_HEADS, m), F32)],
        compiler_params=_cparams(("parallel",)),
        name="ssd_gates",
    )(hn, _pad_lanes(w), _pad_lanes(a_log.reshape(1, -1)), _pad_lanes(dt_bias.reshape(1, -1)))


def _l2norm_rows(x):
    return x * lax.rsqrt(jnp.sum(x * x, axis=-1, keepdims=True) + EPS)


def _gdn_kernel(*refs, heads, sample, chunks_per_seq, n_prev):
    (shift_ref, q_ref, k_ref, v_ref, z_ref, beta_ref, gc_ref, gct_ref, cwq_ref, cwk_ref, cwv_ref,
     nw_ref) = refs[:12]
    n_in = 12
    if sample:
        cq_ref, ck_ref, cv_ref, s0_ref = refs[12:16]
        n_in = 16
    y_ref, st_ref, bq, bk, bv, s_scr = refs[n_in + n_prev:]
    ss_ref = sp_ref = st_ref
    hg = pl.program_id(0)
    gct_row0 = (2 * heads * hg) % gct_ref.shape[0]

    def body(xq, xk, xv):
        qc = jax.nn.silu(_causal_conv(bq, xq, cwq_ref, shift_ref))
        kc = jax.nn.silu(_causal_conv(bk, xk, cwk_ref, shift_ref))
        vc = jax.nn.silu(_causal_conv(bv, xv, cwv_ref, shift_ref))
        incl, strict = _chunk_masks(sample)
        beta_tile, gc_tile = beta_ref[...], gc_ref[...]
        nw = nw_ref[...]
        nv = 2 * heads
        vcols = [slice(lv * GDN_HEAD, (lv + 1) * GDN_HEAD) for lv in range(nv)]
        qs_ = [_l2norm_rows(qc[:, vcols[hh]]) * (GDN_HEAD ** -0.5) for hh in range(heads)]
        ks_ = [_l2norm_rows(kc[:, vcols[hh]]) for hh in range(heads)]
        kks = [_dot_nt(k, k) for k in ks_]
        qks = [_dot_nt(q, k) for q, k in zip(qs_, ks_)]
        beta_c = [_pick_lane(beta_tile, nv * hg + lv) for lv in range(nv)]
        gc_c = [_pick_lane(gc_tile, nv * hg + lv) for lv in range(nv)]
        gc_r = [gct_ref[pl.ds(gct_row0 + lv, 1), :] for lv in range(nv)]
        decay = [jnp.exp(jnp.where(incl, c - r, NEG_BIG)) for c, r in zip(gc_c, gc_r)]
        a_mats = [jnp.where(strict, beta_c[lv] * kks[lv // 2] * decay[lv], 0.0) for lv in range(nv)]
        t_mats = _unit_lower_inverses(a_mats, sample)
        egc = [jnp.exp(c) for c in gc_c]
        uw = [_dot(t_mats[lv], jnp.concatenate([vc[:, vcols[lv]] * beta_c[lv],
                                                ks_[lv // 2] * (beta_c[lv] * egc[lv])], axis=1))
              for lv in range(nv)]
        u = [x[:, :GDN_HEAD] for x in uw]
        w = [x[:, GDN_HEAD:] for x in uw]
        qk_d = [qks[lv // 2] * decay[lv] for lv in range(nv)]
        q_dec = [qs_[lv // 2] * egc[lv] for lv in range(nv)]
        gl_c = [_seq_last_rows(c) if sample else c[CHUNK_T - 1:CHUNK_T, :] for c in gc_c]
        kd_t = [(ks_[lv // 2] * jnp.exp(gl_c[lv] - gc_c[lv])).T for lv in range(nv)]

        if sample:
            wq = [[_dot(jnp.concatenate([w[lv][s * SEQ_PAD:(s + 1) * SEQ_PAD],
                                         q_dec[lv][s * SEQ_PAD:(s + 1) * SEQ_PAD]], axis=0),
                        s0_ref[s, vcols[lv], :]) for s in range(SEQS_PER_CHUNK)] for lv in range(nv)]
            v_new = [u[lv] - jnp.concatenate([x[:SEQ_PAD] for x in wq[lv]], axis=0) for lv in range(nv)]
            o = [jnp.concatenate([x[SEQ_PAD:] for x in wq[lv]], axis=0) + _dot(qk_d[lv], v_new[lv])
                 for lv in range(nv)]
            seq_of_lane = _iota2((GDN_HEAD, CHUNK_T), 1) // SEQ_PAD
            for lv in range(nv):
                v_new_bf = v_new[lv].astype(BF16)
                for s in range(SEQS_PER_CHUNK):
                    dec = jnp.exp(gl_c[lv][s * SEQ_PAD:s * SEQ_PAD + 1, :])
                    upd = _dot(jnp.where(seq_of_lane == s, kd_t[lv], 0.0), v_new_bf)
                    ss_ref[s, vcols[lv], :] = s0_ref[s, vcols[lv], :] * dec + upd
        else:
            s_old = [s_scr[lv] for lv in range(nv)]
            wq = [_dot(jnp.concatenate([w[lv], q_dec[lv]], axis=0), s_old[lv]) for lv in range(nv)]
            v_new = [u[lv] - wq[lv][:CHUNK_T] for lv in range(nv)]
            o = [wq[lv][CHUNK_T:] + _dot(qk_d[lv], v_new[lv]) for lv in range(nv)]
            for lv in range(nv):
                s_new = s_old[lv] * jnp.exp(gl_c[lv]) + _dot(kd_t[lv], v_new[lv])
                s_scr[lv] = s_new
                sp_ref[vcols[lv], :] = s_new

        for lv in range(nv):
            y = _rmsnorm_rows(o[lv], nw) * jax.nn.silu(z_ref[:, vcols[lv]])
            y_ref[:, vcols[lv]] = y.astype(y_ref.dtype)

    if sample:
        real = _real_row_mask((CHUNK_T, 1), 0)
        _zero_conv_history((bq, bk, bv))
        body(jnp.where(real, q_ref[...], cq_ref[...]), jnp.where(real, k_ref[...], ck_ref[...]),
             jnp.where(real, v_ref[...], cv_ref[...]))
    else:
        @pl.when(pl.program_id(1) % chunks_per_seq == 0)
        def _():
            _zero_conv_history((bq, bk, bv))
            s_scr[...] = jnp.zeros(s_scr.shape, F32)
        body(q_ref[...], k_ref[...], v_ref[...])


class _Dims(NamedTuple):
    n_prompt_seqs: int
    chunks_per_seq: int
    n_sample_seqs: int

    @property
    def n_prompt_chunks(self):
        return self.n_prompt_seqs * self.chunks_per_seq

    @property
    def n_steps(self):
        return self.n_prompt_chunks + self.n_sample_seqs // SEQS_PER_CHUNK

    @property
    def n_rows(self):
        return self.n_steps * CHUNK_T


def _gdn_mixer(proj, gates, conv_w, norm_w, cache8, states, layer, prev, dims):
    beta, gc, gct = gates
    hb = GDN_HEAD
    n_pc = dims.n_prompt_chunks
    n_layers, n_seq = states.shape[:2]
    state_rows = GDN_V_HEADS * hb
    shift = _conv_shift_matrix()
    y_shape = jax.ShapeDtypeStruct((dims.n_rows, GDN_V_W), BF16)

    def call(sample, heads, first_step, n_steps, extra_specs, extra_args, state_spec, state_shape,
             aliased):
        qw, vw = heads * hb, 2 * heads * hb
        qkb = GDN_QK_W // qw
        vb = (2 * GDN_QK_W) // vw
        zb = GDN_CONV_DIM // vw
        gct_rows = max(SUBLANES, 2 * heads)

        def spec(width, col_off, row_off=first_step):
            return pl.BlockSpec((CHUNK_T, width), lambda g, t: (row_off + t, col_off + g))

        row128 = pl.BlockSpec((CHUNK_T, LANES), lambda g, t: (first_step + t, 0))
        in_specs = [pl.BlockSpec(shift.shape, lambda g, t: (0, 0)),
                    spec(qw, 0), spec(qw, qkb), spec(vw, vb), spec(vw, zb), row128, row128,
                    pl.BlockSpec((gct_rows, CHUNK_T),
                                 lambda g, t: ((2 * heads * g) // gct_rows, first_step + t)),
                    pl.BlockSpec((CONV_W, qw), lambda g, t: (0, g)),
                    pl.BlockSpec((CONV_W, qw), lambda g, t: (0, qkb + g)),
                    pl.BlockSpec((CONV_W, vw), lambda g, t: (0, vb + g)),
                    pl.BlockSpec((1, hb), lambda g, t: (0, 0))]
        args = [shift, proj, proj, proj, proj, beta, gc, gct, conv_w, conv_w, conv_w,
                norm_w.reshape(1, hb)]
        in_specs += extra_specs(spec, vw) + [pl.BlockSpec(memory_space=pl.ANY)] * len(aliased)
        args += extra_args
        n_in = len(args)
        return pl.pallas_call(
            functools.partial(_gdn_kernel, heads=heads, sample=sample,
                              chunks_per_seq=dims.chunks_per_seq, n_prev=len(aliased)),
            grid=(GDN_QK_HEADS // heads, n_steps),
            in_specs=in_specs,
            out_specs=[pl.BlockSpec((CHUNK_T, vw), lambda g, t: (first_step + t, g)), state_spec(vw)],
            out_shape=[y_shape, state_shape],
            input_output_aliases={n_in + i: out for i, out in enumerate(sorted(aliased))},
            scratch_shapes=[pltpu.VMEM((CONV_HIST, qw), F32), pltpu.VMEM((CONV_HIST, qw), F32),
                            pltpu.VMEM((CONV_HIST, vw), F32), pltpu.VMEM((2 * heads, hb, hb), F32)],
            compiler_params=_cparams(("parallel", "arbitrary")),
            name="gdn_sample" if sample else "gdn_prompt",
        )(*args, *[aliased[k] for k in sorted(aliased)])

    prev_p, prev_s = prev if prev else (None, None)
    cps = dims.chunks_per_seq
    y, sp = call(
        False, GDN_HEADS_PER_PROMPT_STEP, 0, n_pc, lambda spec, vw: [], [],
        lambda vw: pl.BlockSpec((None, None, vw, hb), lambda g, t: (layer, t // cps, g, 0)),
        jax.ShapeDtypeStruct((n_layers, dims.n_prompt_seqs, state_rows, hb), F32),
        {} if prev_p is None else {1: prev_p})
    s_block = lambda vw: pl.BlockSpec((None, SEQS_PER_CHUNK, vw, hb), lambda g, t: (layer, t, g, 0))
    y, ss = call(
        True, GDN_HEADS_PER_STEP, n_pc, n_seq // SEQS_PER_CHUNK,
        lambda spec, vw: [spec(vw // 2, 0, 0), spec(vw // 2, GDN_QK_W // (vw // 2), 0),
                          spec(vw, (2 * GDN_QK_W) // vw, 0), s_block(vw)],
        [cache8, cache8, cache8, states], s_block,
        jax.ShapeDtypeStruct((n_layers, n_seq, state_rows, hb), F32),
        {0: y} if prev_s is None else {0: y, 1: prev_s})
    return y, sp, ss


def _ssd_kernel(*refs, groups, sample, chunks_per_seq, n_prev):
    (shift_ref, z_ref, x_ref, b_ref, c_ref, dt_ref, acs_ref, acst_ref, cwx_ref, cwb_ref, cwc_ref,
     cbx_ref, cbb_ref, cbc_ref, de_ref, nw_ref) = refs[:16]
    n_in = 16
    if sample:
        cx_ref, cb_ref, cc_ref, s0_ref = refs[16:20]
        n_in = 20
    y_ref, st_ref, bx, bb, bc, s_scr = refs[n_in + n_prev:]
    ss_ref = sp_ref = st_ref
    grp = pl.program_id(0)
    t = CHUNK_T
    gw, n = SSD_GW, SSD_STATE
    gcols = [slice(gi * gw, (gi + 1) * gw) for gi in range(groups)]
    ncols = [slice(gi * n, (gi + 1) * n) for gi in range(groups)]
    n_heads = groups * SSD_HPG

    def scaled_by_head(s_mat, col):
        eb = jnp.broadcast_to(jnp.exp(acst_ref[:, col:col + 1]), (n_heads, n))
        return jnp.concatenate([s_mat[r * SSD_HEAD:(r + 1) * SSD_HEAD] * eb[r:r + 1, :]
                                for r in range(s_mat.shape[0] // SSD_HEAD)], axis=0)

    def body(xx, xb, xc):
        xg = jax.nn.silu(_causal_conv(bx, xx, cwx_ref, shift_ref) + cbx_ref[...])
        bg = jax.nn.silu(_causal_conv(bb, xb, cwb_ref, shift_ref) + cbb_ref[...])
        cg = jax.nn.silu(_causal_conv(bc, xc, cwc_ref, shift_ref) + cbc_ref[...])

        dt_tile, acs_tile = dt_ref[...], acs_ref[...]
        sel_shape = (LANES, groups * gw)
        sel = _iota2(sel_shape, 0) == grp * n_heads + _iota2(sel_shape, 1) // SSD_HEAD
        parts = jnp.concatenate(_split3(dt_tile) + _split3(acs_tile), axis=0)
        ex = jnp.dot(parts, sel.astype(BF16), preferred_element_type=F32)
        dt_e = ex[0:t] + ex[t:2 * t] + ex[2 * t:3 * t]
        acs_e = ex[3 * t:4 * t] + ex[4 * t:5 * t] + ex[5 * t:6 * t]
        acs_last_e = _seq_last_rows(acs_e) if sample else acs_e[t - 1:t, :]

        xdt = xg * dt_e
        incl, _ = _chunk_masks(sample)
        cbs = [_dot_nt(cg[:, ncols[gi]], bg[:, ncols[gi]]) for gi in range(groups)]
        acs_c = [_pick_lane(acs_tile, grp * n_heads + r) for r in range(n_heads)]
        ms_ = [cbs[r // SSD_HPG] * jnp.exp(jnp.where(incl, acs_c[r] - acst_ref[r:r + 1, :], NEG_BIG))
               for r in range(n_heads)]
        lane_lo = _iota2((t, 2 * SSD_HEAD), 1) < SSD_HEAD
        pairs = []
        for pr in range(n_heads // 2):
            xpair = xdt[:, pr * 2 * SSD_HEAD:(pr + 1) * 2 * SSD_HEAD]
            rhs = jnp.concatenate([jnp.where(lane_lo, xpair, 0.0), jnp.where(lane_lo, 0.0, xpair)],
                                  axis=0)
            pairs.append(_dot(jnp.concatenate([ms_[2 * pr], ms_[2 * pr + 1]], axis=1), rhs))
        y_diag = jnp.concatenate(pairs, axis=1)

        xd_t = (xdt * jnp.exp(acs_last_e - acs_e)).T
        if sample:
            offs = []
            zeros = jnp.zeros((SEQ_PAD, n), F32)
            seq_of_lane = _iota2((groups * gw, t), 1) // SEQ_PAD
            bg_bf = bg.astype(BF16)
            for s in range(SEQS_PER_CHUNK):
                rows = slice(s * SEQ_PAD, (s + 1) * SEQ_PAD)
                s_old = s0_ref[s]
                offs.append(jnp.concatenate(
                    [_dot_nt(jnp.concatenate([cg[rows, ncols[gi]], zeros], axis=0),
                             s_old[gcols[gi]])[:SEQ_PAD] for gi in range(groups)], axis=1))
                xd_s = jnp.where(seq_of_lane == s, xd_t, 0.0)
                upd = jnp.concatenate([_dot(xd_s[gcols[gi]], bg_bf[:, ncols[gi]])
                                       for gi in range(groups)], axis=0)
                ss_ref[s] = scaled_by_head(s_old, (s + 1) * SEQ_PAD - 1) + upd
            y_off = jnp.concatenate(offs, axis=0)
        else:
            s_old = s_scr[...]
            y_off = jnp.concatenate([_dot_nt(cg[:, ncols[gi]], s_old[gcols[gi]])
                                     for gi in range(groups)], axis=1)
            upd = jnp.concatenate([_dot(xd_t[gcols[gi]], bg[:, ncols[gi]]) for gi in range(groups)],
                                  axis=0)
            s_new = scaled_by_head(s_old, t - 1) + upd
            s_scr[...] = s_new
            sp_ref[...] = s_new

        y = y_diag + y_off * jnp.exp(acs_e) + de_ref[...] * xg
        y = y * jax.nn.silu(z_ref[...])
        nw = nw_ref[...]
        for gi in range(groups):
            yg = y[:, gcols[gi]]
            yg = yg * lax.rsqrt(jnp.mean(yg * yg, axis=-1, keepdims=True) + EPS) * nw[:, gcols[gi]]
            y_ref[:, gcols[gi]] = yg.astype(y_ref.dtype)

    if sample:
        real = _real_row_mask((CHUNK_T, 1), 0)
        _zero_conv_history((bx, bb, bc))
        body(jnp.where(real, x_ref[...], cx_ref[...]), jnp.where(real, b_ref[...], cb_ref[...]),
             jnp.where(real, c_ref[...], cc_ref[...]))
    else:
        @pl.when(pl.program_id(1) % chunks_per_seq == 0)
        def _():
            _zero_conv_history((bx, bb, bc))
            s_scr[...] = jnp.zeros(s_scr.shape, F32)
        body(x_ref[...], b_ref[...], c_ref[...])


def _ssd_mixer(proj, gates, conv_w, conv_b, d_skip, norm_w, cache8, states, layer, prev, dims):
    dt, acs, acst = gates
    n_pc = dims.n_prompt_chunks
    n_layers, n_seq = states.shape[:2]
    state_rows = SSD_HEADS * SSD_HEAD
    shift = _conv_shift_matrix()
    conv_b2 = conv_b.reshape(1, -1)
    d_e = jnp.repeat(d_skip, SSD_HEAD).reshape(1, SSD_D_INNER)
    nw2 = norm_w.reshape(1, SSD_D_INNER)

    def call(sample, groups, first_step, n_steps, extra_specs, extra_args, state_spec, state_shape,
             aliased):
        gw, n = groups * SSD_GW, groups * SSD_STATE
        xb = SSD_D_INNER // gw
        bb = (2 * SSD_D_INNER) // n
        cb = bb + SSD_GN // n
        wbb = SSD_D_INNER // n
        wcb = wbb + SSD_GN // n

        def spec(width, col_off, row_off=first_step):
            return pl.BlockSpec((CHUNK_T, width), lambda g, t: (row_off + t, col_off + g))

        def wspec(rows, width, col_off):
            return pl.BlockSpec((rows, width), lambda g, t: (0, col_off + g))

        row128 = pl.BlockSpec((CHUNK_T, LANES), lambda g, t: (first_step + t, 0))
        in_specs = [pl.BlockSpec(shift.shape, lambda g, t: (0, 0)),
                    spec(gw, 0), spec(gw, xb), spec(n, bb), spec(n, cb), row128, row128,
                    pl.BlockSpec((groups * SSD_HPG, CHUNK_T), lambda g, t: (g, first_step + t)),
                    wspec(CONV_W, gw, 0), wspec(CONV_W, n, wbb), wspec(CONV_W, n, wcb),
                    wspec(1, gw, 0), wspec(1, n, wbb), wspec(1, n, wcb),
                    wspec(1, gw, 0), wspec(1, gw, 0)]
        args = [shift, proj, proj, proj, proj, dt, acs, acst, conv_w, conv_w, conv_w,
                conv_b2, conv_b2, conv_b2, d_e, nw2]
        in_specs += (extra_specs(spec, gw, n, wbb, wcb)
                     + [pl.BlockSpec(memory_space=pl.ANY)] * len(aliased))
        args += extra_args
        n_in = len(args)
        return pl.pallas_call(
            functools.partial(_ssd_kernel, groups=groups, sample=sample,
                              chunks_per_seq=dims.chunks_per_seq, n_prev=len(aliased)),
            grid=(SSD_GROUPS // groups, n_steps),
            in_specs=in_specs,
            out_specs=[pl.BlockSpec((CHUNK_T, gw), lambda g, t: (first_step + t, g)), state_spec(gw)],
            out_shape=[jax.ShapeDtypeStruct((dims.n_rows, SSD_D_INNER), BF16), state_shape],
            input_output_aliases={n_in + i: out for i, out in enumerate(sorted(aliased))},
            scratch_shapes=[pltpu.VMEM((CONV_HIST, gw), F32), pltpu.VMEM((CONV_HIST, n), F32),
                            pltpu.VMEM((CONV_HIST, n), F32), pltpu.VMEM((gw, SSD_STATE), F32)],
            compiler_params=_cparams(("parallel", "arbitrary")),
            name="ssd_sample" if sample else "ssd_prompt",
        )(*args, *[aliased[k] for k in sorted(aliased)])

    prev_p, prev_s = prev if prev else (None, None)
    cps = dims.chunks_per_seq
    y, sp = call(
        False, SSD_GROUPS_PER_PROMPT_STEP, 0, n_pc, lambda *a: [], [],
        lambda gw: pl.BlockSpec((None, None, gw, SSD_STATE), lambda g, t: (layer, t // cps, g, 0)),
        jax.ShapeDtypeStruct((n_layers, dims.n_prompt_seqs, state_rows, SSD_STATE), F32),
        {} if prev_p is None else {1: prev_p})
    s_block = lambda gw: pl.BlockSpec((None, SEQS_PER_CHUNK, gw, SSD_STATE),
                                      lambda g, t: (layer, t, g, 0))
    y, ss = call(
        True, SSD_GROUPS_PER_STEP, n_pc, n_seq // SEQS_PER_CHUNK,
        lambda spec, gw, n, wbb, wcb: [spec(gw, 0, 0), spec(n, wbb, 0), spec(n, wcb, 0), s_block(gw)],
        [cache8, cache8, cache8, states], s_block,
        jax.ShapeDtypeStruct((n_layers, n_seq, state_rows, SSD_STATE), F32),
        {0: y} if prev_s is None else {0: y, 1: prev_s})
    return y, sp, ss


def _pad_sequences(x, n_real):
    pads = [(0, 0)] * (x.ndim - 2) + [(SEQ_PAD - n_real, 0), (0, 0)]
    xp = jnp.pad(x, pads)
    return xp.reshape(x.shape[:-3] + (x.shape[-3] * SEQ_PAD, x.shape[-1]))


def _history_rows(cache):
    n_real = SEQ_PAD // 2
    lo = SEQ_PAD - n_real - (CONV_W - 1)
    xp = jnp.pad(cache, ((0, 0), (lo, n_real), (0, 0)))
    return xp.reshape(cache.shape[0] * SEQ_PAD, cache.shape[2])


def kernel(x_prompt, x_sample, state_gdn, cache_gdn_conv, state_ssd, cache_ssd_conv, p_prompt, p_sample,
           norm_w, gdn_w_in, gdn_conv_w, gdn_A_log, gdn_dt_bias, gdn_norm_w, gdn_w_out,
           ssd_w_in, ssd_conv_w, ssd_conv_b, ssd_A_log, ssd_dt_bias, ssd_D, ssd_norm_w, ssd_w_out,
           ple_w_proj, ple_w_gate, ple_norm_w, final_norm_w):
    bp, lp, d = x_prompt.shape
    bs, ls, _ = x_sample.shape
    depth = norm_w.shape[0]
    assert d == D_MODEL and ls == SEQ_PAD // 2 and lp % CHUNK_T == 0 and bs % SEQS_PER_CHUNK == 0
    mp, ms = bp * lp, bs * SEQ_PAD
    assert mp % ROW_TILE == 0 and ms % ROW_TILE == 0

    h = jnp.concatenate([x_prompt.reshape(mp, d), _pad_sequences(x_sample, ls)], axis=0)
    p = jnp.concatenate([p_prompt.reshape(depth, mp, PLE_DIM), _pad_sequences(p_sample, ls)], axis=1)

    def conv_caches(proj, lo, hi):
        c_p = jnp.stack([proj[(b + 1) * lp - (CONV_W - 1):(b + 1) * lp, lo:hi] for b in range(bp)])
        c_s = proj[mp:, lo:hi].reshape(bs, SEQ_PAD, hi - lo)[:, SEQ_PAD - (CONV_W - 1):, :]
        return c_p, c_s

    dims = _Dims(n_prompt_seqs=bp, chunks_per_seq=lp // CHUNK_T, n_sample_seqs=bs)
    gdn_states = state_gdn.reshape(state_gdn.shape[0], bs, GDN_V_HEADS * GDN_HEAD, GDN_HEAD)
    ssd_states = state_ssd.reshape(state_ssd.shape[0], bs, SSD_HEADS * SSD_HEAD, SSD_STATE)
    gdn_w_in_t = jnp.swapaxes(gdn_w_in, 1, 2)
    ssd_w_in_t = jnp.swapaxes(ssd_w_in, 1, 2)
    w_gate_bf = ple_w_gate.astype(BF16)
    caches = {k: [] for k in ("cg_p", "cs_p", "cg_s", "cs_s")}
    gdn_prev, ssd_prev = (), ()
    hn = _norm_rows(h, norm_w[0], BF16)
    for i in range(depth):
        j = i // 2
        if i % 2 == 0:
            proj = _matmul(hn, gdn_w_in_t, j, GDN_MAIN, "plain", w_nk=True)
            gates = _gdn_gates(hn, gdn_w_in[j, :, GDN_MAIN:], gdn_A_log[j], gdn_dt_bias[j], mp)
            y, *gdn_prev = _gdn_mixer(proj, gates, gdn_conv_w[j], gdn_norm_w[j],
                                      _history_rows(cache_gdn_conv[j]), gdn_states, j, gdn_prev, dims)
            c_p, c_s = conv_caches(proj, 0, GDN_CONV_DIM)
            w_out = gdn_w_out
            keys = ("cg_p", "cg_s")
        else:
            proj = _matmul(hn, ssd_w_in_t, j, SSD_MAIN, "plain", w_nk=True)
            gates = _ssd_gates(hn, ssd_w_in[j, :, SSD_MAIN:], ssd_A_log[j], ssd_dt_bias[j], mp)
            y, *ssd_prev = _ssd_mixer(proj, gates, ssd_conv_w[j], ssd_conv_b[j], ssd_D[j], ssd_norm_w[j],
                                      _history_rows(cache_ssd_conv[j]), ssd_states, j, ssd_prev, dims)
            c_p, c_s = conv_caches(proj, SSD_D_INNER, SSD_MAIN)
            w_out = ssd_w_out
            keys = ("cs_p", "cs_s")
        for key, val in zip(keys, (c_p, c_s)):
            caches[key].append(val)
        h_mid = _matmul(y, w_out, j, D_MODEL, "residual", extras=(h,))
        if i + 1 < depth:
            h, hn = _ple_add(h_mid, p, w_gate_bf, ple_w_proj, i, ple_norm_w[i], norm_w[i + 1])
        else:
            y_p, y_s = _ple_add(h_mid, p, w_gate_bf, ple_w_proj, i, ple_norm_w[i], final_norm_w, mp)

    y_prompt = y_p.reshape(bp, lp, d)
    y_sample = y_s.reshape(bs, SEQ_PAD, d)[:, SEQ_PAD - ls:, :]
    cc = {k: jnp.stack(v) for k, v in caches.items()}
    sg_p, sg_s = (s.reshape(s.shape[:2] + (GDN_V_HEADS, GDN_HEAD, GDN_HEAD)) for s in gdn_prev)
    ss_p, ss_s = (s.reshape(s.shape[:2] + (SSD_HEADS, SSD_HEAD, SSD_STATE)) for s in ssd_prev)
    return (y_prompt, y_sample, sg_p, cc["cg_p"], ss_p, cc["cs_p"], sg_s, cc["cg_s"], ss_s, cc["cs_s"])
```
